```python
import math
import jax, jax.numpy as jnp
from jax import lax
import numpy as np

D_MODEL = 1024
BATCH = 16
SEQ = 2048
DEPTH = 2

N_A = DEPTH // 2
N_B = DEPTH - N_A
N_DENSE = (DEPTH + 1) // 2
N_MOE = DEPTH // 2

PLE_DIM = 256
EPS = 1e-6

SSM_EXPAND = 2
D_INNER = SSM_EXPAND * D_MODEL
SSM_HEADDIM = 64
SSM_HEADS = D_INNER // SSM_HEADDIM
SSM_GROUPS = 4
SSM_STATE = 128
CONV_WIDTH = 4
CHUNK = 128
CONV_DIM = D_INNER + 2 * SSM_GROUPS * SSM_STATE
D_IN_PROJ = 2 * D_INNER + 2 * SSM_GROUPS * SSM_STATE + SSM_HEADS

ATT_HEADS = 16
ATT_HEADDIM = 64
ATT_WIDTH = ATT_HEADS * ATT_HEADDIM
Q_BLOCK = 128
KV_PROJ = 2 * ATT_WIDTH + ATT_HEADS

D_FF = 3584
N_EXPERTS = 8
TOP_K = 2

kernel_name = "ssd_fox_yoco_moe_ple_block"


def rmsnorm(x, w):
    x32 = x.astype(jnp.float32)
    y = x32 * lax.rsqrt(jnp.mean(x32 * x32, axis=-1, keepdims=True) + EPS)
    return y.astype(x.dtype) * w


def swiglu(u, w_gate, w_up, w_down):
    return (jax.nn.silu(u @ w_gate) * (u @ w_up)) @ w_down


def causal_dwconv(u, w, b):
    out = lax.conv_general_dilated(
        u, w[:, None, :].astype(u.dtype), window_strides=(1,),
        padding=[(CONV_WIDTH - 1, 0)],
        dimension_numbers=('NWC', 'WIO', 'NWC'),
        feature_group_count=u.shape[-1])
    return out + b


def ssd_scan(xh, dt, A, Bg, Cg):
    b, L, H, P = xh.shape
    G, N = Bg.shape[-2:]
    K = H // G
    c = L // CHUNK
    xc = (xh * dt[..., None]).reshape(b, c, CHUNK, G, K, P)
    a = jnp.moveaxis((dt * A).reshape(b, c, CHUNK, G, K), 2, -1)
    a_cs = jnp.cumsum(a, axis=-1)
    Bc = Bg.reshape(b, c, CHUNK, G, N)
    Cc = Cg.reshape(b, c, CHUNK, G, N)
    causal = jnp.tril(jnp.ones((CHUNK, CHUNK), dtype=bool))
    seg = a_cs[..., :, None] - a_cs[..., None, :]
    decay_ls = jnp.exp(jnp.where(causal, seg, -jnp.inf))
    scores = jnp.einsum('bclgn,bcsgn->bcgls', Cc, Bc)
    y_diag = jnp.einsum('bcgls,bcgkls,bcsgkp->bclgkp', scores, decay_ls, xc)
    decay_states = jnp.exp(a_cs[..., -1:] - a_cs)
    states = jnp.einsum('bclgn,bcgkl,bclgkp->bcgkpn', Bc, decay_states, xc)
    chunk_decay = jnp.exp(a_cs[..., -1])

    def step(carry, inp):
        st, dec = inp
        return carry * dec[..., None, None] + st, carry

    init = jnp.zeros((b, G, K, P, N), dtype=states.dtype)
    _, prev = lax.scan(step, init, (jnp.moveaxis(states, 1, 0), jnp.moveaxis(chunk_decay, 1, 0)))
    prev = jnp.moveaxis(prev, 0, 1)
    y_off = jnp.einsum('bclgn,bcgkpn,bcgkl->bclgkp', Cc, prev, jnp.exp(a_cs))
    return (y_diag + y_off).reshape(b, L, H, P)


def mamba2_mixer(u, w_in, conv_w, conv_b, dt_bias, a_log, d_skip, gn_w, w_out):
    b, L, _ = u.shape
    zxbcdt = u @ w_in
    z = zxbcdt[..., :D_INNER]
    xbc = zxbcdt[..., D_INNER:D_INNER + CONV_DIM]
    dt = zxbcdt[..., D_INNER + CONV_DIM:]
    xbc = jax.nn.silu(causal_dwconv(xbc, conv_w, conv_b))
    xs = xbc[..., :D_INNER]
    Bs = xbc[..., D_INNER:D_INNER + SSM_GROUPS * SSM_STATE]
    Cs = xbc[..., D_INNER + SSM_GROUPS * SSM_STATE:]
    xh = xs.reshape(b, L, SSM_HEADS, SSM_HEADDIM)
    Bg = Bs.reshape(b, L, SSM_GROUPS, SSM_STATE)
    Cg = Cs.reshape(b, L, SSM_GROUPS, SSM_STATE)
    dt = jax.nn.softplus((dt + dt_bias).astype(jnp.float32))
    A = -jnp.exp(a_log.astype(jnp.float32))
    y = ssd_scan(xh, dt, A, Bg, Cg) + xh * d_skip[:, None]
    y = y.reshape(b, L, D_INNER)
    y = rmsnorm(y * jax.nn.silu(z), gn_w)
    return (y @ w_out).astype(u.dtype)


def fox_shared_kv(h, kv_norm_w, w_kv, b_f, k_norm_w):
    b, L, _ = h.shape
    kvf = rmsnorm(h, kv_norm_w) @ w_kv
    k = rmsnorm(kvf[..., :ATT_WIDTH].reshape(b, L, ATT_HEADS, ATT_HEADDIM), k_norm_w)
    v = kvf[..., ATT_WIDTH:2 * ATT_WIDTH].reshape(b, L, ATT_HEADS, ATT_HEADDIM)
    log_f = jax.nn.log_sigmoid((kvf[..., 2 * ATT_WIDTH:] + b_f).astype(jnp.float32))
    cum = jnp.cumsum(log_f, axis=1)
    return k, v, cum


def fox_attention(u, w_q, q_norm_w, w_o, k, v, cum):
    b, L, _ = u.shape
    q = rmsnorm((u @ w_q).reshape(b, L, ATT_HEADS, ATT_HEADDIM), q_norm_w)
    nblk = L // Q_BLOCK
    qb = jnp.moveaxis(q.reshape(b, nblk, Q_BLOCK, ATT_HEADS, ATT_HEADDIM), 1, 0)
    cumT = jnp.transpose(cum, (0, 2, 1))
    cq = jnp.moveaxis(cumT.reshape(b, ATT_HEADS, nblk, Q_BLOCK), 2, 0)
    kpos = jnp.arange(L)
    scale = ATT_HEADDIM ** -0.5

    def block(args):
        qi, ci, i = args
        qpos = i * Q_BLOCK + jnp.arange(Q_BLOCK)
        logits = jnp.einsum('bqhd,bkhd->bhqk', qi, k, preferred_element_type=jnp.float32) * scale
        logits = logits + (ci[..., :, None] - cumT[..., None, :])
        logits = jnp.where(kpos[None, :] <= qpos[:, None], logits, -jnp.inf)
        probs = jax.nn.softmax(logits, axis=-1)
        return jnp.einsum('bhqk,bkhd->bqhd', probs.astype(v.dtype), v)

    o = lax.map(block, (qb, cq, jnp.arange(nblk)))
    o = jnp.moveaxis(o, 0, 1).reshape(b, L, ATT_WIDTH)
    return (o @ w_o).astype(u.dtype)


def moe_swiglu(u, w_router, w_gate, w_up, w_down):
    logits = (u @ w_router).astype(jnp.float32)
    top_vals, top_idx = lax.top_k(logits, TOP_K)
    top_w = jax.nn.softmax(top_vals, axis=-1)
    gates = jnp.sum(jax.nn.one_hot(top_idx, N_EXPERTS, dtype=jnp.float32) * top_w[..., None], axis=-2)
    out = jnp.zeros_like(u)
    for e in range(N_EXPERTS):
        out = out + gates[..., e:e + 1].astype(u.dtype) * swiglu(u, w_gate[e], w_up[e], w_down[e])
    return out


def per_layer_embedding(h, p_i, norm_w, w_gate, w_proj):
    g = jax.nn.sigmoid(rmsnorm(h, norm_w) @ w_gate)
    return (p_i @ w_proj) * g


def setup_inputs(seed: int = 0) -> dict:
    key = jax.random.key(seed)
    ks = iter(jax.random.split(key, 40))
    f32 = jnp.float32

    def nrm(shape, fan_in):
        return jax.random.normal(next(ks), shape, f32) * (fan_in ** -0.5)

    def gain(shape):
        return 1.0 + 0.05 * jax.random.normal(next(ks), shape, f32)

    x = jax.random.normal(next(ks), (BATCH, SEQ, D_MODEL), f32)
    p = jax.random.normal(next(ks), (DEPTH, BATCH, SEQ, PLE_DIM), f32)

    dt0 = jnp.exp(jax.random.uniform(next(ks), (N_A, SSM_HEADS), f32, math.log(1e-3), math.log(1e-1)))
    ssm_dt_bias = dt0 + jnp.log(-jnp.expm1(-dt0))
    ssm_a_log = jnp.log(jax.random.uniform(next(ks), (N_A, SSM_HEADS), f32, 1.0, 16.0))

    return {
        "x": x,
        "p": p,
        "ssm_norm_w": gain((N_A, D_MODEL)),
        "ssm_w_in": nrm((N_A, D_MODEL, D_IN_PROJ), D_MODEL),
        "ssm_conv_w": nrm((N_A, CONV_WIDTH, CONV_DIM), CONV_WIDTH),
        "ssm_conv_b": 0.02 * jax.random.normal(next(ks), (N_A, CONV_DIM), f32),
        "ssm_dt_bias": ssm_dt_bias,
        "ssm_a_log": ssm_a_log,
        "ssm_d": gain((N_A, SSM_HEADS)),
        "ssm_gn_w": gain((N_A, D_INNER)),
        "ssm_w_out": nrm((N_A, D_INNER, D_MODEL), D_INNER),
        "kv_norm_w": gain((D_MODEL,)),
        "w_kv": nrm((D_MODEL, KV_PROJ), D_MODEL),
        "b_f": 2.0 + 0.5 * jax.random.normal(next(ks), (ATT_HEADS,), f32),
        "k_norm_w": gain((ATT_HEADDIM,)),
        "att_norm_w": gain((N_B, D_MODEL)),
        "att_w_q": nrm((N_B, D_MODEL, ATT_WIDTH), D_MODEL),
        "q_norm_w": gain((N_B, ATT_HEADDIM)),
        "att_w_o": nrm((N_B, ATT_WIDTH, D_MODEL), ATT_WIDTH),
        "ffn_norm_w": gain((N_DENSE, D_MODEL)),
        "ffn_w_gate": nrm((N_DENSE, D_MODEL, D_FF), D_MODEL),
        "ffn_w_up": nrm((N_DENSE, D_MODEL, D_FF), D_MODEL),
        "ffn_w_down": nrm((N_DENSE, D_FF, D_MODEL), D_FF),
        "moe_norm_w": gain((N_MOE, D_MODEL)),
        "moe_w_router": nrm((N_MOE, D_MODEL, N_EXPERTS), D_MODEL),
        "moe_w_gate": nrm((N_MOE, N_EXPERTS, D_MODEL, D_FF), D_MODEL),
        "moe_w_up": nrm((N_MOE, N_EXPERTS, D_MODEL, D_FF), D_MODEL),
        "moe_w_down": nrm((N_MOE, N_EXPERTS, D_FF, D_MODEL), D_FF),
        "ple_norm_w": gain((DEPTH, D_MODEL)),
        "ple_w_gate": nrm((DEPTH, D_MODEL, D_MODEL), D_MODEL),
        "ple_w_proj": nrm((DEPTH, PLE_DIM, D_MODEL), PLE_DIM),
    }


def reference(x, p, ssm_norm_w, ssm_w_in, ssm_conv_w, ssm_conv_b, ssm_dt_bias, ssm_a_log,
              ssm_d, ssm_gn_w, ssm_w_out, kv_norm_w, w_kv, b_f, k_norm_w,
              att_norm_w, att_w_q, q_norm_w, att_w_o,
              ffn_norm_w, ffn_w_gate, ffn_w_up, ffn_w_down,
              moe_norm_w, moe_w_router, moe_w_gate, moe_w_up, moe_w_down,
              ple_norm_w, ple_w_gate, ple_w_proj):
    h = x
    k_sh = v_sh = cum_sh = None
    for i in range(DEPTH):
        if i < N_A:
            h = h + mamba2_mixer(rmsnorm(h, ssm_norm_w[i]), ssm_w_in[i], ssm_conv_w[i], ssm_conv_b[i],
                                 ssm_dt_bias[i], ssm_a_log[i], ssm_d[i], ssm_gn_w[i], ssm_w_out[i])
        else:
            if i == N_A:
                k_sh, v_sh, cum_sh = fox_shared_kv(h, kv_norm_w, w_kv, b_f, k_norm_w)
            j = i - N_A
            h = h + fox_attention(rmsnorm(h, att_norm_w[j]), att_w_q[j], q_norm_w[j], att_w_o[j],
                                  k_sh, v_sh, cum_sh)
        if i % 2 == 0:
            d = i // 2
            h = h + swiglu(rmsnorm(h, ffn_norm_w[d]), ffn_w_gate[d], ffn_w_up[d], ffn_w_down[d])
        else:
            m = i // 2
            h = h + moe_swiglu(rmsnorm(h, moe_norm_w[m]), moe_w_router[m], moe_w_gate[m],
                               moe_w_up[m], moe_w_down[m])
        h = h + per_layer_embedding(h, p[i], ple_norm_w[i], ple_w_gate[i], ple_w_proj[i])
    return h
```

```python
import functools

import jax
import jax.numpy as jnp
from jax import lax
from jax.experimental import pallas as pl
from jax.experimental.pallas import tpu as pltpu

F32 = jnp.float32
BF16 = jnp.bfloat16
EPS = 1e-6
HIGHEST = lax.Precision.HIGHEST

LANES = 128
MIB = 1024 * 1024

SSM_HEADDIM = 64
SSM_GROUPS = 4
SSM_STATE = 128
CONV_WIDTH = 4
CHUNK = 128
ATT_HEADDIM = 64
N_EXPERTS = 8


def _params(sem, vmem_mib):
    return pltpu.CompilerParams(dimension_semantics=sem, vmem_limit_bytes=vmem_mib * MIB)


def _rms_scale(x):
    return lax.rsqrt(jnp.mean(x * x, axis=-1, keepdims=True) + EPS)


def _sigmoid(x):
    return 1.0 / (1.0 + jnp.exp(-x))


def _softplus(x):
    return jnp.maximum(x, 0.0) + jnp.log(1.0 + jnp.exp(-jnp.abs(x)))


def _norm_matmul_kernel(h_ref, nw_ref, w_ref, o_ref, xn_ref):
    @pl.when(pl.program_id(1) == 0)
    def _():
        x = h_ref[...]
        xn_ref[...] = (x * _rms_scale(x) * nw_ref[...]).astype(xn_ref.dtype)

    o_ref[...] = jnp.dot(xn_ref[...], w_ref[...], preferred_element_type=F32).astype(o_ref.dtype)


def norm_matmul(h, nw, w, out_dtype, tm, tn):
    t, d = h.shape
    n = w.shape[1]
    return pl.pallas_call(
        _norm_matmul_kernel,
        grid=(t // tm, n // tn),
        in_specs=[
            pl.BlockSpec((tm, d), lambda i, j: (i, 0)),
            pl.BlockSpec((1, d), lambda i, j: (0, 0)),
            pl.BlockSpec((d, tn), lambda i, j: (0, j)),
        ],
        out_specs=pl.BlockSpec((tm, tn), lambda i, j: (i, j)),
        out_shape=jax.ShapeDtypeStruct((t, n), out_dtype),
        scratch_shapes=[pltpu.VMEM((tm, d), BF16)],
        compiler_params=_params(("parallel", "arbitrary"), 40),
        name="norm_matmul",
    )(h, nw.reshape(1, d), w)


def _matmul_residual_kernel(h_ref, y_ref, w_ref, o_ref):
    o_ref[...] = h_ref[...] + jnp.dot(y_ref[...], w_ref[...], preferred_element_type=F32)


def matmul_residual(h, y, w, tm):
    t, n = h.shape
    k = y.shape[1]
    return pl.pallas_call(
        _matmul_residual_kernel,
        grid=(t // tm,),
        in_specs=[
            pl.BlockSpec((tm, n), lambda i: (i, 0)),
            pl.BlockSpec((tm, k), lambda i: (i, 0)),
            pl.BlockSpec((k, n), lambda i: (0, 0)),
        ],
        out_specs=pl.BlockSpec((tm, n), lambda i: (i, 0)),
        out_shape=jax.ShapeDtypeStruct((t, n), F32),
        compiler_params=_params(("parallel",), 48),
        name="matmul_residual",
    )(h, y, w)


def _ffn_kernel(h_ref, nw_ref, wg_ref, wu_ref, wd_ref, o_ref, xn_ref, acc_ref):
    j = pl.program_id(1)

    @pl.when(j == 0)
    def _():
        x = h_ref[...]
        xn_ref[...] = (x * _rms_scale(x) * nw_ref[...]).astype(xn_ref.dtype)
        acc_ref[...] = x

    xn = xn_ref[...]
    g = jnp.dot(xn, wg_ref[...], preferred_element_type=F32)
    u = jnp.dot(xn, wu_ref[...], preferred_element_type=F32)
    a = (g * _sigmoid(g) * u).astype(BF16)
    acc_ref[...] += jnp.dot(a, wd_ref[...], preferred_element_type=F32)

    @pl.when(j == pl.num_programs(1) - 1)
    def _():
        o_ref[...] = acc_ref[...]


def ffn(h, nw, wg, wu, wd, tm, tf):
    t, d = h.shape
    f = wg.shape[1]
    return pl.pallas_call(
        _ffn_kernel,
        grid=(t // tm, f // tf),
        in_specs=[
            pl.BlockSpec((tm, d), lambda i, j: (i, 0)),
            pl.BlockSpec((1, d), lambda i, j: (0, 0)),
            pl.BlockSpec((d, tf), lambda i, j: (0, j)),
            pl.BlockSpec((d, tf), lambda i, j: (0, j)),
            pl.BlockSpec((tf, d), lambda i, j: (j, 0)),
        ],
        out_specs=pl.BlockSpec((tm, d), lambda i, j: (i, 0)),
        out_shape=jax.ShapeDtypeStruct((t, d), F32),
        scratch_shapes=[pltpu.VMEM((tm, d), BF16), pltpu.VMEM((tm, d), F32)],
        compiler_params=_params(("parallel", "arbitrary"), 48),
        name="ffn",
    )(h, nw.reshape(1, d), wg, wu, wd)


def _ple_kernel(h_ref, p_ref, nw_ref, wg_ref, wp_ref, o_ref):
    x = h_ref[...]
    xn = (x * _rms_scale(x) * nw_ref[...]).astype(BF16)
    gate = _sigmoid(jnp.dot(xn, wg_ref[...], preferred_element_type=F32))
    proj = jnp.dot(p_ref[...].astype(BF16), wp_ref[...], preferred_element_type=F32)
    o_ref[...] = x + proj * gate


def ple(h, p, nw, wg, wp, tm):
    t, d = h.shape
    pd = p.shape[1]
    return pl.pallas_call(
        _ple_kernel,
        grid=(t // tm,),
        in_specs=[
            pl.BlockSpec((tm, d), lambda i: (i, 0)),
            pl.BlockSpec((tm, pd), lambda i: (i, 0)),
            pl.BlockSpec((1, d), lambda i: (0, 0)),
            pl.BlockSpec((d, d), lambda i: (0, 0)),
            pl.BlockSpec((pd, d), lambda i: (0, 0)),
        ],
        out_specs=pl.BlockSpec((tm, d), lambda i: (i, 0)),
        out_shape=jax.ShapeDtypeStruct((t, d), F32),
        compiler_params=_params(("parallel",), 40),
        name="ple",
    )(h, p, nw.reshape(1, d), wg, wp)


def _split3(v):
    v1 = v.astype(BF16)
    r1 = v - v1.astype(F32)
    v2 = r1.astype(BF16)
    v3 = (r1 - v2.astype(F32)).astype(BF16)
    lane = lax.broadcasted_iota(jnp.int32, v.shape, 1)
    return jnp.where(lane < 32, v1, jnp.where(lane < 64, v2, v3))


def _ssd_kernel(z_ref, xbc_ref, dt_ref, cw_ref, cb_ref, dtb_ref, alog_ref, dskip_ref, gnw_ref, exp_ref,
                o_ref, cbuf, xact, state, acs_t, y_scr):
    d_inner = z_ref.shape[-1]
    gw = d_inner // SSM_GROUPS
    q = CHUNK

    @pl.when(pl.program_id(1) == 0)
    def _():
        cbuf[0:8, :] = jnp.zeros((8, cbuf.shape[1]), F32)
        state[...] = jnp.zeros(state.shape, F32)

    cbuf[8:8 + q, :] = xbc_ref[0].astype(F32)
    cdim = cbuf.shape[1]
    for j in range(cdim // 512):
        sl = slice(j * 512, (j + 1) * 512)
        acc = jnp.broadcast_to(cb_ref[:, sl], (q, 512))
        for w in range(CONV_WIDTH):
            acc = acc + cbuf[5 + w:5 + w + q, sl] * cw_ref[w:w + 1, sl]
        xact[:, sl] = acc * _sigmoid(acc)
    cbuf[0:8, :] = cbuf[q:q + 8, :]

    dt = _softplus(dt_ref[0] + dtb_ref[...])
    a = dt * (-jnp.exp(alog_ref[...]))
    row = lax.broadcasted_iota(jnp.int32, (q, q), 0)
    col = lax.broadcasted_iota(jnp.int32, (q, q), 1)
    causal = row >= col
    tril = jnp.where(causal, 1.0, 0.0).astype(F32)
    acs = jnp.dot(tril, a, preferred_element_type=F32, precision=HIGHEST)
    acs_t[...] = acs.T
    a_last = acs[q - 1:q, :]
    dt3 = _split3(dt)
    dec3 = _split3(jnp.exp(a_last - acs))
    e3 = _split3(jnp.exp(acs))

    for g in range(SSM_GROUPS):
        gl = slice(g * gw, (g + 1) * gw)
        ex = exp_ref[:, gl]
        xs = xact[:, gl]
        b_g = xact[:, d_inner + g * SSM_STATE:d_inner + (g + 1) * SSM_STATE].astype(BF16)
        c_g = xact[:, d_inner + (SSM_GROUPS + g) * SSM_STATE:d_inner + (SSM_GROUPS + g + 1) * SSM_STATE].astype(BF16)
        dt_e = jnp.dot(dt3, ex, preferred_element_type=F32)
        dec_e = jnp.dot(dec3, ex, preferred_element_type=F32)
        e_e = jnp.dot(e3, ex, preferred_element_type=F32)
        xc = xs * dt_e
        xc_b = xc.astype(BF16)
        xdec_b = (xc * dec_e).astype(BF16)
        scores = lax.dot_general(c_g, b_g, (((1,), (1,)), ((), ())), preferred_element_type=F32)
        st = state[g]
        y = jnp.dot(c_g, st.astype(BF16), preferred_element_type=F32) * e_e
        y = y + xs * dskip_ref[:, gl]
        lane = lax.broadcasted_iota(jnp.int32, (q, LANES), 1)
        ydiag = []
        for pr in range(gw // LANES):
            ms = []
            for hh in range(2):
                h = g * (gw // SSM_HEADDIM) + 2 * pr + hh
                seg = acs[:, h:h + 1] - acs_t[h:h + 1, :]
                ms.append(scores * jnp.exp(jnp.where(causal, seg, -jnp.inf)))
            lhs = jnp.concatenate(ms, axis=1).astype(BF16)
            xp = xc_b[:, pr * LANES:(pr + 1) * LANES]
            zero = jnp.zeros_like(xp)
            rhs = jnp.concatenate([jnp.where(lane < SSM_HEADDIM, xp, zero),
                                   jnp.where(lane >= SSM_HEADDIM, xp, zero)], axis=0)
            ydiag.append(jnp.dot(lhs, rhs, preferred_element_type=F32))
        y_scr[:, gl] = y + jnp.concatenate(ydiag, axis=1)
        upd = lax.dot_general(b_g, xdec_b, (((0,), (0,)), ((), ())), preferred_element_type=F32)
        state[g] = st * e_e[q - 1:q, :] + upd

    zz = z_ref[0].astype(F32)
    y = y_scr[...] * (zz * _sigmoid(zz))
    o_ref[0] = (y * _rms_scale(y) * gnw_ref[...]).astype(o_ref.dtype)


def ssd(z, xbc, dt_raw, conv_w, conv_b, dt_bias3, a_log3, dskip_e, gn_w, expand):
    b, l, d_inner = z.shape
    cdim = xbc.shape[-1]
    gw = d_inner // SSM_GROUPS
    nc = l // CHUNK
    full = lambda shape: pl.BlockSpec(shape, lambda i, c: (0,) * len(shape))
    return pl.pallas_call(
        _ssd_kernel,
        grid=(b, nc),
        in_specs=[
            pl.BlockSpec((1, CHUNK, d_inner), lambda i, c: (i, c, 0)),
            pl.BlockSpec((1, CHUNK, cdim), lambda i, c: (i, c, 0)),
            pl.BlockSpec((1, CHUNK, LANES), lambda i, c: (i, c, 0)),
            full((CONV_WIDTH, cdim)), full((1, cdim)), full((1, LANES)), full((1, LANES)),
            full((1, d_inner)), full((1, d_inner)), full((LANES, d_inner)),
        ],
        out_specs=pl.BlockSpec((1, CHUNK, d_inner), lambda i, c: (i, c, 0)),
        out_shape=jax.ShapeDtypeStruct((b, l, d_inner), BF16),
        scratch_shapes=[
            pltpu.VMEM((CHUNK + 8, cdim), F32),
            pltpu.VMEM((CHUNK, cdim), F32),
            pltpu.VMEM((SSM_GROUPS, SSM_STATE, gw), F32),
            pltpu.VMEM((LANES, CHUNK), F32),
            pltpu.VMEM((CHUNK, d_inner), F32),
        ],
        compiler_params=_params(("parallel", "arbitrary"), 40),
        name="ssd",
    )(z, xbc, dt_raw, conv_w, conv_b, dt_bias3, a_log3, dskip_e, gn_w, expand)


def _cum_kernel(f_ref, bf_ref, o_ref, *, blk):
    l = f_ref.shape[1]
    row = lax.broadcasted_iota(jnp.int32, (blk, blk), 0)
    col = lax.broadcasted_iota(jnp.int32, (blk, blk), 1)
    tril = jnp.where(row >= col, 1.0, 0.0).astype(F32)
    carry = jnp.zeros((1, LANES), F32)
    for i in range(l // blk):
        x = f_ref[0, i * blk:(i + 1) * blk, :] + bf_ref[...]
        lf = -_softplus(-x)
        cs = jnp.dot(tril, lf, preferred_element_type=F32, precision=HIGHEST) + carry
        carry = cs[blk - 1:blk, :]
        o_ref[0, :, i * blk:(i + 1) * blk] = cs.T


def forget_cumsum(f, b_f_pad):
    b, l, _ = f.shape
    return pl.pallas_call(
        functools.partial(_cum_kernel, blk=256),
        grid=(b,),
        in_specs=[pl.BlockSpec((1, l, LANES), lambda i: (i, 0, 0)), pl.BlockSpec((1, LANES), lambda i: (0, 0))],
        out_specs=pl.BlockSpec((1, LANES, l), lambda i: (i, 0, 0)),
        out_shape=jax.ShapeDtypeStruct((b, LANES, l), F32),
        compiler_params=_params(("parallel",), 32),
        name="forget_cumsum",
    )(f, b_f_pad)


def _head_rms(x, lane_lo):
    sq = x * x
    s_lo = jnp.sum(jnp.where(lane_lo, sq, 0.0), axis=-1, keepdims=True)
    s_hi = jnp.sum(jnp.where(lane_lo, 0.0, sq), axis=-1, keepdims=True)
    inv = 1.0 / ATT_HEADDIM
    return jnp.where(lane_lo, lax.rsqrt(s_lo * inv + EPS), lax.rsqrt(s_hi * inv + EPS))


def _attn_kernel(q_ref, k_ref, v_ref, c_ref, qnw_ref, knw_ref, o_ref, kn_ref, *, tq):
    l = q_ref.shape[1]
    nblk = l // tq
    scale = ATT_HEADDIM ** -0.5
    lane_lo_k = lax.broadcasted_iota(jnp.int32, (tq, LANES), 1) < ATT_HEADDIM

    for i in range(nblk):
        kk = k_ref[0, i * tq:(i + 1) * tq, :].astype(F32)
        kn_ref[i * tq:(i + 1) * tq, :] = (kk * _head_rms(kk, lane_lo_k) * knw_ref[...]).astype(BF16)

    row = lax.broadcasted_iota(jnp.int32, (tq, tq), 0)
    col = lax.broadcasted_iota(jnp.int32, (tq, tq), 1)
    eye = row == col
    causal = row >= col

    for qi in range(nblk):
        qs = slice(qi * tq, (qi + 1) * tq)
        qq = q_ref[0, qs, :].astype(F32)
        qn = qq * _head_rms(qq, lane_lo_k) * (qnw_ref[...] * scale)
        zero = jnp.zeros_like(qn)
        q_h = [jnp.where(lane_lo_k, qn, zero).astype(BF16), jnp.where(lane_lo_k, zero, qn).astype(BF16)]
        cq = []
        for hh in range(2):
            crow = c_ref[0, 0, hh * nblk + qi:hh * nblk + qi + 1, :]
            cq.append(jnp.sum(jnp.where(eye, crow, 0.0), axis=1, keepdims=True))

        def kv_step(j, carry, masked):
            ks = pl.ds(j * tq if isinstance(j, int) else pl.multiple_of(j * tq, tq), tq)
            kb = kn_ref[ks, :]
            vb = v_ref[0, ks, :]
            out = []
            for hh in range(2):
                m, lsum, acc = carry[hh]
                s = lax.dot_general(q_h[hh], kb, (((1,), (1,)), ((), ())), preferred_element_type=F32)
                s = s + (cq[hh] - c_ref[0, 0, pl.ds(hh * nblk + j, 1), :])
                if masked:
                    s = jnp.where(causal, s, -jnp.inf)
                m_new = jnp.maximum(m, jnp.max(s, axis=-1, keepdims=True))
                alpha = jnp.exp(m - m_new)
                p = jnp.exp(s - m_new)
                lsum = alpha * lsum + jnp.sum(p, axis=-1, keepdims=True)
                acc = alpha * acc + jnp.dot(p.astype(BF16), vb, preferred_element_type=F32)
                out.append((m_new, lsum, acc))
            return tuple(out)

        init = tuple((jnp.full((tq, 1), -jnp.inf, F32), jnp.zeros((tq, 1), F32), jnp.zeros((tq, LANES), F32))
                     for _ in range(2))
        carry = init
        if qi > 0:
            carry = lax.fori_loop(0, qi, lambda j, c: kv_step(j, c, False), carry)
        carry = kv_step(qi, carry, True)
        o0 = carry[0][2] / carry[0][1]
        o1 = carry[1][2] / carry[1][1]
        o_ref[0, qs, :] = jnp.where(lane_lo_k, o0, o1).astype(o_ref.dtype)


def fox_attention(qp, kv, cum_blocks, qnw2, knw2, att_w, tq):
    b, l, _ = qp.shape
    npair = att_w // LANES
    nblk = l // tq
    return pl.pallas_call(
        functools.partial(_attn_kernel, tq=tq),
        grid=(b, npair),
        in_specs=[
            pl.BlockSpec((1, l, LANES), lambda i, j: (i, 0, j)),
            pl.BlockSpec((1, l, LANES), lambda i, j: (i, 0, j)),
            pl.BlockSpec((1, l, LANES), lambda i, j: (i, 0, j + npair)),
            pl.BlockSpec((1, 1, 2 * nblk, tq), lambda i, j: (i, j, 0, 0)),
            pl.BlockSpec((1, LANES), lambda i, j: (0, 0)),
            pl.BlockSpec((1, LANES), lambda i, j: (0, 0)),
        ],
        out_specs=pl.BlockSpec((1, l, LANES), lambda i, j: (i, 0, j)),
        out_shape=jax.ShapeDtypeStruct((b, l, att_w), BF16),
        scratch_shapes=[pltpu.VMEM((l, LANES), BF16)],
        compiler_params=_params(("parallel", "parallel"), 32),
        name="fox_attention",
    )(qp, kv, kv, cum_blocks, qnw2, knw2)


def _router_kernel(h_ref, nw_ref, wr_ref, g_ref, xn_ref):
    x = h_ref[...]
    u = x * _rms_scale(x) * nw_ref[...]
    xn_ref[...] = u.astype(xn_ref.dtype)
    logits = jnp.dot(u, wr_ref[...], preferred_element_type=F32, precision=HIGHEST)
    lane = lax.broadcasted_iota(jnp.int32, logits.shape, 1)
    neg = jnp.float32(-jnp.inf)
    logits = jnp.where(lane < N_EXPERTS, logits, neg)
    m1 = jnp.max(logits, axis=-1, keepdims=True)
    i1 = jnp.min(jnp.where(logits == m1, lane, LANES), axis=-1, keepdims=True)
    rest = jnp.where(lane == i1, neg, logits)
    m2 = jnp.max(rest, axis=-1, keepdims=True)
    i2 = jnp.min(jnp.where(rest == m2, lane, LANES), axis=-1, keepdims=True)
    e2 = jnp.exp(m2 - m1)
    g1 = 1.0 / (1.0 + e2)
    g2 = e2 / (1.0 + e2)
    g_ref[...] = jnp.where(lane == i1, g1, jnp.where(lane == i2, g2, 0.0))


def router(h, nw, wr_pad, tm):
    t, d = h.shape
    return pl.pallas_call(
        _router_kernel,
        grid=(t // tm,),
        in_specs=[
            pl.BlockSpec((tm, d), lambda i: (i, 0)),
            pl.BlockSpec((1, d), lambda i: (0, 0)),
            pl.BlockSpec((d, LANES), lambda i: (0, 0)),
        ],
        out_specs=[pl.BlockSpec((tm, LANES), lambda i: (i, 0)), pl.BlockSpec((tm, d), lambda i: (i, 0))],
        out_shape=[jax.ShapeDtypeStruct((t, LANES), F32), jax.ShapeDtypeStruct((t, d), BF16)],
        compiler_params=_params(("parallel",), 32),
        name="router",
    )(h, nw.reshape(1, d), wr_pad)


def _moe_dense_kernel(h_ref, u_ref, g_ref, wg_ref, wu_ref, wd_ref, o_ref, acc_ref):
    e = pl.program_id(1)
    j = pl.program_id(2)

    @pl.when((e == 0) & (j == 0))
    def _():
        acc_ref[...] = h_ref[...]

    xn = u_ref[...]
    g = jnp.dot(xn, wg_ref[0], preferred_element_type=F32)
    u = jnp.dot(xn, wu_ref[0], preferred_element_type=F32)
    lane = lax.broadcasted_iota(jnp.int32, g_ref.shape, 1)
    gate = jnp.sum(jnp.where(lane == e, g_ref[...], 0.0), axis=-1, keepdims=True)
    a = (g * _sigmoid(g) * u * gate).astype(BF16)
    acc_ref[...] += jnp.dot(a, wd_ref[0], preferred_element_type=F32)

    @pl.when((e == pl.num_programs(1) - 1) & (j == pl.num_programs(2) - 1))
    def _():
        o_ref[...] = acc_ref[...]


def moe_dense(h, u, gates, wg, wu, wd, tm, tf):
    t, d = h.shape
    ne, _, f = wg.shape
    return pl.pallas_call(
        _moe_dense_kernel,
        grid=(t // tm, ne, f // tf),
        in_specs=[
            pl.BlockSpec((tm, d), lambda i, e, j: (i, 0)),
            pl.BlockSpec((tm, d), lambda i, e, j: (i, 0)),
            pl.BlockSpec((tm, LANES), lambda i, e, j: (i, 0)),
            pl.BlockSpec((1, d, tf), lambda i, e, j: (e, 0, j)),
            pl.BlockSpec((1, d, tf), lambda i, e, j: (e, 0, j)),
            pl.BlockSpec((1, tf, d), lambda i, e, j: (e, j, 0)),
        ],
        out_specs=pl.BlockSpec((tm, d), lambda i, e, j: (i, 0)),
        out_shape=jax.ShapeDtypeStruct((t, d), F32),
        scratch_shapes=[pltpu.VMEM((tm, d), F32)],
        compiler_params=_params(("parallel", "arbitrary", "arbitrary"), 48),
        name="moe_dense",
    )(h, u, gates, wg, wu, wd)


def _pad_cols(w, n):
    return jnp.pad(w, ((0, 0), (0, n - w.shape[1])))


def kernel(x, p, ssm_norm_w, ssm_w_in, ssm_conv_w, ssm_conv_b, ssm_dt_bias, ssm_a_log, ssm_d, ssm_gn_w, ssm_w_out, kv_norm_w, w_kv, b_f, k_norm_w, att_norm_w, att_w_q, q_norm_w, att_w_o, ffn_norm_w, ffn_w_gate, ffn_w_up, ffn_w_down, moe_norm_w, moe_w_router, moe_w_gate, moe_w_up, moe_w_down, ple_norm_w, ple_w_gate, ple_w_proj):
    b, l, d = x.shape
    t = b * l
    n_a = ssm_norm_w.shape[0]
    depth = ple_norm_w.shape[0]
    d_inner = ssm_gn_w.shape[1]
    n_heads = ssm_dt_bias.shape[1]
    cdim = ssm_conv_w.shape[2]
    att_w = att_w_q.shape[2]
    n_att_heads = att_w // ATT_HEADDIM
    tm = min(1024, t)

    h = x.reshape(t, d)
    r = jnp.arange(LANES)[:, None]
    c = jnp.arange(d_inner)[None, :]
    expand = ((r < 3 * n_heads) & ((r % n_heads) == (c // SSM_HEADDIM))).astype(BF16)

    kv = cum_blocks = tq = None
    for i in range(depth):
        if i < n_a:
            w_in = ssm_w_in[i].astype(BF16)
            z = norm_matmul(h, ssm_norm_w[i], w_in[:, :d_inner], BF16, tm, 512)
            xbc = norm_matmul(h, ssm_norm_w[i], w_in[:, d_inner:d_inner + cdim], BF16, tm, 512)
            w_dt = w_in[:, d_inner + cdim:]
            w_dt3 = _pad_cols(jnp.concatenate([w_dt, w_dt, w_dt], axis=1), LANES)
            dt_raw = norm_matmul(h, ssm_norm_w[i], w_dt3, F32, tm, LANES)
            rep3 = lambda v: jnp.pad(jnp.tile(v, 3), (0, LANES - 3 * n_heads)).reshape(1, LANES)
            y = ssd(z.reshape(b, l, d_inner), xbc.reshape(b, l, cdim), dt_raw.reshape(b, l, LANES),
                    ssm_conv_w[i], ssm_conv_b[i].reshape(1, cdim), rep3(ssm_dt_bias[i]), rep3(ssm_a_log[i]),
                    jnp.repeat(ssm_d[i], SSM_HEADDIM).reshape(1, d_inner), ssm_gn_w[i].reshape(1, d_inner), expand)
            h = matmul_residual(h, y.reshape(t, d_inner), ssm_w_out[i].astype(BF16), tm)
        else:
            j = i - n_a
            if i == n_a:
                wkv = w_kv.astype(BF16)
                kv = norm_matmul(h, kv_norm_w, wkv[:, :2 * att_w], BF16, tm, 512).reshape(b, l, 2 * att_w)
                f = norm_matmul(h, kv_norm_w, _pad_cols(wkv[:, 2 * att_w:], LANES), F32, tm, LANES)
                bf_pad = jnp.pad(b_f, (0, LANES - n_att_heads)).reshape(1, LANES)
                cum = forget_cumsum(f.reshape(b, l, LANES), bf_pad)
                tq = min(256, l)
                cum_blocks = cum[:, :n_att_heads, :].reshape(b, n_att_heads // 2, 2 * (l // tq), tq)
            qp = norm_matmul(h, att_norm_w[j], att_w_q[j].astype(BF16), BF16, tm, 512).reshape(b, l, att_w)
            qnw2 = jnp.tile(q_norm_w[j], 2).reshape(1, LANES)
            knw2 = jnp.tile(k_norm_w, 2).reshape(1, LANES)
            o = fox_attention(qp, kv, cum_blocks, qnw2, knw2, att_w, tq)
            h = matmul_residual(h, o.reshape(t, att_w), att_w_o[j].astype(BF16), tm)
        if i % 2 == 0:
            dd = i // 2
            h = ffn(h, ffn_norm_w[dd], ffn_w_gate[dd].astype(BF16), ffn_w_up[dd].astype(BF16),
                    ffn_w_down[dd].astype(BF16), tm, 512)
        else:
            m = i // 2
            gates, u = router(h, moe_norm_w[m], _pad_cols(moe_w_router[m], LANES), tm)
            h = moe_dense(h, u, gates, moe_w_gate[m].astype(BF16), moe_w_up[m].astype(BF16),
                          moe_w_down[m].astype(BF16), tm, 512)
        h = ple(h, p[i].reshape(t, p.shape[-1]), ple_norm_w[i], ple_w_gate[i].astype(BF16),
                ple_w_proj[i].astype(BF16), tm)
    return h.reshape(b, l, d)
```

```python
import functools
import math

import numpy as np
import jax
import jax.numpy as jnp
from jax import lax
from jax.experimental import pallas as pl
from jax.experimental.pallas import tpu as pltpu

F32 = jnp.float32
BF16 = jnp.bfloat16
I32 = jnp.int32
U32 = jnp.uint32
EPS = 1e-6
HIGHEST = lax.Precision.HIGHEST
LOG2E = math.log2(math.e)

LANES = 128
SUBLANES = 8
MIB = 1024 * 1024

SSM_HEADDIM = 64
SSM_GROUPS = 4
SSM_STATE = 128
CONV_WIDTH = 4
CHUNK = 128
ATT_HEADDIM = 64
N_EXPERTS = 8
TOP_K = 2


def _params(sem, vmem_mib):
    return pltpu.CompilerParams(dimension_semantics=sem, vmem_limit_bytes=vmem_mib * MIB)


def _rms_scale(x):
    return lax.rsqrt(jnp.mean(x * x, axis=-1, keepdims=True) + EPS)


def _sigmoid(x):
    return 1.0 / (1.0 + jnp.exp(-x))


def _softplus(x):
    return jnp.maximum(x, 0.0) + jnp.log(1.0 + jnp.exp(-jnp.abs(x)))


def _pack_bf16_pairs(x):
    n = x.shape[1] // 2
    bits = lax.bitcast_convert_type(x.astype(BF16).astype(F32), U32)
    return lax.shift_right_logical(bits[:, :n], jnp.uint32(16)) | (bits[:, n:] & jnp.uint32(0xFFFF0000))


def _unpack_bf16_pairs(w):
    lo = lax.bitcast_convert_type(lax.shift_left(w, jnp.uint32(16)), F32)
    hi = lax.bitcast_convert_type(w & jnp.uint32(0xFFFF0000), F32)
    return jnp.concatenate([lo, hi], axis=1)


def _norm_matmul_kernel(h_ref, nw_ref, *refs):
    n = len(refs) // 2
    x = h_ref[...]
    xn = (x * _rms_scale(x) * nw_ref[...]).astype(BF16)
    for w_ref, o_ref in zip(refs[:n], refs[n:]):
        o_ref[...] = jnp.dot(xn, w_ref[...], preferred_element_type=F32).astype(o_ref.dtype)


def norm_matmul(h, nw, ws, out_dtypes, tm):
    t, d = h.shape
    resident = lambda shape: pl.BlockSpec(shape, lambda i: (0, 0), pipeline_mode=pl.Buffered(1))
    return pl.pallas_call(
        _norm_matmul_kernel,
        grid=(t // tm,),
        in_specs=[pl.BlockSpec((tm, d), lambda i: (i, 0)), resident((1, d))] + [resident(w.shape) for w in ws],
        out_specs=[pl.BlockSpec((tm, w.shape[1]), lambda i: (i, 0)) for w in ws],
        out_shape=[jax.ShapeDtypeStruct((t, w.shape[1]), dt) for w, dt in zip(ws, out_dtypes)],
        compiler_params=_params(("parallel",), 48),
        name="norm_matmul",
    )(h, nw.reshape(1, d), *ws)


def _matmul_residual_kernel(h_ref, y_ref, w_ref, o_ref):
    o_ref[...] = h_ref[...] + jnp.dot(y_ref[...], w_ref[...], preferred_element_type=F32)


def matmul_residual(h, y, w, tm):
    t, n = h.shape
    k = y.shape[1]
    return pl.pallas_call(
        _matmul_residual_kernel,
        grid=(t // tm,),
        in_specs=[
            pl.BlockSpec((tm, n), lambda i: (i, 0)),
            pl.BlockSpec((tm, k), lambda i: (i, 0)),
            pl.BlockSpec((k, n), lambda i: (0, 0)),
        ],
        out_specs=pl.BlockSpec((tm, n), lambda i: (i, 0)),
        out_shape=jax.ShapeDtypeStruct((t, n), F32),
        compiler_params=_params(("parallel",), 48),
        name="matmul_residual",
    )(h, y, w)


def _ffn_kernel(h_ref, nw_ref, wg_ref, wu_ref, wd_ref, o_ref, xn_ref, acc_ref):
    j = pl.program_id(1)

    @pl.when(j == 0)
    def _():
        x = h_ref[...]
        xn_ref[...] = (x * _rms_scale(x) * nw_ref[...]).astype(xn_ref.dtype)
        acc_ref[...] = x

    xn = xn_ref[...]
    g = jnp.dot(xn, wg_ref[...], preferred_element_type=F32)
    u = jnp.dot(xn, wu_ref[...], preferred_element_type=F32)
    a = (g * _sigmoid(g) * u).astype(BF16)
    acc_ref[...] += jnp.dot(a, wd_ref[...], preferred_element_type=F32)

    @pl.when(j == pl.num_programs(1) - 1)
    def _():
        o_ref[...] = acc_ref[...]


def ffn(h, nw, wg, wu, wd, tm, tf):
    t, d = h.shape
    f = wg.shape[1]
    return pl.pallas_call(
        _ffn_kernel,
        grid=(t // tm, f // tf),
        in_specs=[
            pl.BlockSpec((tm, d), lambda i, j: (i, 0)),
            pl.BlockSpec((1, d), lambda i, j: (0, 0)),
            pl.BlockSpec((d, tf), lambda i, j: (0, j)),
            pl.BlockSpec((d, tf), lambda i, j: (0, j)),
            pl.BlockSpec((tf, d), lambda i, j: (j, 0)),
        ],
        out_specs=pl.BlockSpec((tm, d), lambda i, j: (i, 0)),
        out_shape=jax.ShapeDtypeStruct((t, d), F32),
        scratch_shapes=[pltpu.VMEM((tm, d), BF16), pltpu.VMEM((tm, d), F32)],
        compiler_params=_params(("parallel", "arbitrary"), 48),
        name="ffn",
    )(h, nw.reshape(1, d), wg, wu, wd)


def _ple_kernel(h_ref, p_ref, nw_ref, wg_ref, wp_ref, o_ref):
    x = h_ref[...]
    xn = (x * _rms_scale(x) * nw_ref[...]).astype(BF16)
    gate = _sigmoid(jnp.dot(xn, wg_ref[...], preferred_element_type=F32))
    proj = jnp.dot(p_ref[...].astype(BF16), wp_ref[...], preferred_element_type=F32)
    o_ref[...] = x + proj * gate


def ple(h, p, nw, wg, wp, tm):
    t, d = h.shape
    pd = p.shape[1]
    return pl.pallas_call(
        _ple_kernel,
        grid=(t // tm,),
        in_specs=[
            pl.BlockSpec((tm, d), lambda i: (i, 0)),
            pl.BlockSpec((tm, pd), lambda i: (i, 0)),
            pl.BlockSpec((1, d), lambda i: (0, 0)),
            pl.BlockSpec((d, d), lambda i: (0, 0)),
            pl.BlockSpec((pd, d), lambda i: (0, 0)),
        ],
        out_specs=pl.BlockSpec((tm, d), lambda i: (i, 0)),
        out_shape=jax.ShapeDtypeStruct((t, d), F32),
        compiler_params=_params(("parallel",), 40),
        name="ple",
    )(h, p, nw.reshape(1, d), wg, wp)


def _split3(v, n):
    v1 = v.astype(BF16)
    r1 = v - v1.astype(F32)
    v2 = r1.astype(BF16)
    v3 = (r1 - v2.astype(F32)).astype(BF16)
    lane = lax.broadcasted_iota(I32, v.shape, 1)
    return jnp.where(lane < n, v1, jnp.where(lane < 2 * n, v2, v3))


def _ssd_kernel(z_ref, xbc_ref, dt_ref, cw_ref, cb_ref, dtb_ref, alog_ref, dskip_ref, gnw_ref, exp_ref,
                o_ref, cbuf, xact, state, acs_t, y_scr):
    d_inner = z_ref.shape[-1]
    gw = d_inner // SSM_GROUPS
    n_heads = d_inner // SSM_HEADDIM
    q = CHUNK

    @pl.when(pl.program_id(1) == 0)
    def _():
        cbuf[0:8, :] = jnp.zeros((8, cbuf.shape[1]), F32)
        state[...] = jnp.zeros(state.shape, F32)

    cbuf[8:8 + q, :] = xbc_ref[0].astype(F32)
    cdim = cbuf.shape[1]
    for j in range(cdim // 512):
        sl = slice(j * 512, (j + 1) * 512)
        acc = jnp.broadcast_to(cb_ref[:, sl], (q, 512))
        for w in range(CONV_WIDTH):
            acc = acc + cbuf[5 + w:5 + w + q, sl] * cw_ref[w:w + 1, sl]
        xact[:, sl] = acc * _sigmoid(acc)
    cbuf[0:8, :] = cbuf[q:q + 8, :]

    dt = _softplus(dt_ref[0] + dtb_ref[...])
    a = dt * (-jnp.exp(alog_ref[...]))
    row = lax.broadcasted_iota(I32, (q, q), 0)
    col = lax.broadcasted_iota(I32, (q, q), 1)
    causal = row >= col
    tril = jnp.where(causal, 1.0, 0.0).astype(F32)
    acs = jnp.dot(tril, a, preferred_element_type=F32, precision=HIGHEST)
    acs_t[...] = acs.T
    a_last = acs[q - 1:q, :]
    dt3 = _split3(dt, n_heads)
    dec3 = _split3(jnp.exp(a_last - acs), n_heads)
    e3 = _split3(jnp.exp(acs), n_heads)

    for g in range(SSM_GROUPS):
        gl = slice(g * gw, (g + 1) * gw)
        ex = exp_ref[:, gl]
        xs = xact[:, gl]
        b_g = xact[:, d_inner + g * SSM_STATE:d_inner + (g + 1) * SSM_STATE].astype(BF16)
        c_g = xact[:, d_inner + (SSM_GROUPS + g) * SSM_STATE:d_inner + (SSM_GROUPS + g + 1) * SSM_STATE].astype(BF16)
        dt_e = jnp.dot(dt3, ex, preferred_element_type=F32)
        dec_e = jnp.dot(dec3, ex, preferred_element_type=F32)
        e_e = jnp.dot(e3, ex, preferred_element_type=F32)
        xc = xs * dt_e
        xc_b = xc.astype(BF16)
        xdec_b = (xc * dec_e).astype(BF16)
        scores = lax.dot_general(c_g, b_g, (((1,), (1,)), ((), ())), preferred_element_type=F32)
        st = state[g]
        y = jnp.dot(c_g, st.astype(BF16), preferred_element_type=F32) * e_e
        y = y + xs * dskip_ref[:, gl]
        lane = lax.broadcasted_iota(I32, (q, LANES), 1)
        ydiag = []
        for pr in range(gw // LANES):
            ms = []
            for hh in range(2):
                h = g * (gw // SSM_HEADDIM) + 2 * pr + hh
                seg = acs[:, h:h + 1] - acs_t[h:h + 1, :]
                ms.append(scores * jnp.exp(jnp.where(causal, seg, -jnp.inf)))
            lhs = jnp.concatenate(ms, axis=1).astype(BF16)
            xp = xc_b[:, pr * LANES:(pr + 1) * LANES]
            zero = jnp.zeros_like(xp)
            rhs = jnp.concatenate([jnp.where(lane < SSM_HEADDIM, xp, zero),
                                   jnp.where(lane >= SSM_HEADDIM, xp, zero)], axis=0)
            ydiag.append(jnp.dot(lhs, rhs, preferred_element_type=F32))
        y_scr[:, gl] = y + jnp.concatenate(ydiag, axis=1)
        upd = lax.dot_general(b_g, xdec_b, (((0,), (0,)), ((), ())), preferred_element_type=F32)
        state[g] = st * e_e[q - 1:q, :] + upd

    zz = z_ref[0].astype(F32)
    y = y_scr[...] * (zz * _sigmoid(zz))
    o_ref[0] = (y * _rms_scale(y) * gnw_ref[...]).astype(o_ref.dtype)


def ssd(z, xbc, dt_raw, conv_w, conv_b, dt_bias3, a_log3, dskip_e, gn_w, expand):
    b, l, d_inner = z.shape
    cdim = xbc.shape[-1]
    gw = d_inner // SSM_GROUPS
    nc = l // CHUNK
    full = lambda shape: pl.BlockSpec(shape, lambda i, c: (0,) * len(shape))
    return pl.pallas_call(
        _ssd_kernel,
        grid=(b, nc),
        in_specs=[
            pl.BlockSpec((1, CHUNK, d_inner), lambda i, c: (i, c, 0)),
            pl.BlockSpec((1, CHUNK, cdim), lambda i, c: (i, c, 0)),
            pl.BlockSpec((1, CHUNK, LANES), lambda i, c: (i, c, 0)),
            full((CONV_WIDTH, cdim)), full((1, cdim)), full((1, LANES)), full((1, LANES)),
            full((1, d_inner)), full((1, d_inner)), full((LANES, d_inner)),
        ],
        out_specs=pl.BlockSpec((1, CHUNK, d_inner), lambda i, c: (i, c, 0)),
        out_shape=jax.ShapeDtypeStruct((b, l, d_inner), BF16),
        scratch_shapes=[
            pltpu.VMEM((CHUNK + 8, cdim), F32),
            pltpu.VMEM((CHUNK, cdim), F32),
            pltpu.VMEM((SSM_GROUPS, SSM_STATE, gw), F32),
            pltpu.VMEM((LANES, CHUNK), F32),
            pltpu.VMEM((CHUNK, d_inner), F32),
        ],
        compiler_params=_params(("parallel", "arbitrary"), 40),
        name="ssd",
    )(z, xbc, dt_raw, conv_w, conv_b, dt_bias3, a_log3, dskip_e, gn_w, expand)


def _cum_kernel(f_ref, bf_ref, pq_ref, pk_ref, eq_ref, ek_ref, *, blk, n_heads):
    l = f_ref.shape[1]
    row = lax.broadcasted_iota(I32, (blk, blk), 0)
    col = lax.broadcasted_iota(I32, (blk, blk), 1)
    tril = jnp.where(row >= col, 1.0, 0.0).astype(F32)
    lane = lax.broadcasted_iota(I32, (blk, LANES), 1)
    carry = jnp.zeros((1, LANES), F32)
    for i in range(l // blk):
        rs = slice(i * blk, (i + 1) * blk)
        x = f_ref[0, rs, :] + bf_ref[...]
        lf = -_softplus(-x)
        cs = jnp.dot(tril, lf, preferred_element_type=F32, precision=HIGHEST) + carry
        carry = cs[blk - 1:blk, :]
        lhs = jnp.where(lane == 3 * n_heads, jnp.ones((), BF16), _split3(cs * LOG2E, n_heads))
        eq_ref[0, rs, :] = jnp.dot(lhs, pq_ref[...], preferred_element_type=F32).astype(BF16)
        ek_ref[0, rs, :] = jnp.dot(lhs, pk_ref[...], preferred_element_type=F32).astype(BF16)


def _ext_placement(n_heads):
    w = n_heads * ATT_HEADDIM
    pq = np.zeros((LANES, w), np.float32)
    pk = np.zeros((LANES, w), np.float32)
    one = 3 * n_heads
    for h in range(n_heads):
        base = (h // 2) * LANES + (ATT_HEADDIM if h % 2 == 0 else 0)
        for piece in range(3):
            pq[piece * n_heads + h, base + piece] = 1.0
            pq[one, base + 3 + piece] = 1.0
            pk[one, base + piece] = 1.0
            pk[piece * n_heads + h, base + 3 + piece] = -1.0
    return jnp.asarray(pq, BF16), jnp.asarray(pk, BF16)


def forget_ext(f, b_f3, n_heads):
    b, l, _ = f.shape
    w = n_heads * ATT_HEADDIM
    pq, pk = _ext_placement(n_heads)
    out = jax.ShapeDtypeStruct((b, l, w), BF16)
    return pl.pallas_call(
        functools.partial(_cum_kernel, blk=256, n_heads=n_heads),
        grid=(b,),
        in_specs=[
            pl.BlockSpec((1, l, LANES), lambda i: (i, 0, 0)),
            pl.BlockSpec((1, LANES), lambda i: (0, 0)),
            pl.BlockSpec((LANES, w), lambda i: (0, 0)),
            pl.BlockSpec((LANES, w), lambda i: (0, 0)),
        ],
        out_specs=[pl.BlockSpec((1, l, w), lambda i: (i, 0, 0)), pl.BlockSpec((1, l, w), lambda i: (i, 0, 0))],
        out_shape=[out, out],
        compiler_params=_params(("parallel",), 40),
        name="forget_ext",
    )(f, b_f3, pq, pk)


def _head_rms(x, lane_lo):
    sq = x * x
    s_lo = jnp.sum(jnp.where(lane_lo, sq, 0.0), axis=-1, keepdims=True)
    s_hi = jnp.sum(jnp.where(lane_lo, 0.0, sq), axis=-1, keepdims=True)
    inv = 1.0 / ATT_HEADDIM
    return jnp.where(lane_lo, lax.rsqrt(s_lo * inv + EPS), lax.rsqrt(s_hi * inv + EPS))


def _attn_kernel(q_ref, k_ref, v_ref, eq_ref, ek_ref, qnw_ref, knw_ref, o_ref, kaug, vaugt, s_scr, p_scr, *, tq):
    l = q_ref.shape[1]
    nblk = l // tq
    hd = ATT_HEADDIM
    lane_lo = lax.broadcasted_iota(I32, (tq, LANES), 1) < hd
    sub_lo = lax.broadcasted_iota(I32, (LANES, tq), 0) < hd
    row = lax.broadcasted_iota(I32, (tq, tq), 0)
    col = lax.broadcasted_iota(I32, (tq, tq), 1)
    visible = row <= col
    eye_q = jnp.where(row == col, 1.0, 0.0).astype(BF16)
    eye_l = eye_q[:LANES, :LANES]
    nt = (((1,), (1,)), ((), ()))
    qscale = qnw_ref[...] * (hd ** -0.5 * LOG2E)

    for i in range(nblk):
        rs = slice(i * tq, (i + 1) * tq)
        kk = k_ref[0, rs, :].astype(F32)
        kn = (kk * _head_rms(kk, lane_lo) * knw_ref[...]).astype(BF16)
        ek = ek_ref[0, rs, :]
        kaug[0, rs, :] = jnp.where(lane_lo, kn, ek)
        kaug[1, rs, :] = jnp.where(lane_lo, ek, kn)
        vt = lax.dot_general(eye_l, v_ref[0, rs, :], nt, preferred_element_type=F32).astype(BF16)
        one = jnp.ones_like(vt)
        vaugt[0, :, rs] = jnp.where(sub_lo, vt, one)
        vaugt[1, :, rs] = jnp.where(sub_lo, one, vt)

    def scores(qi):
        qs = slice(qi * tq, (qi + 1) * tq)
        nk = (qi + 1) * tq
        qq = q_ref[0, qs, :].astype(F32)
        qn = (qq * _head_rms(qq, lane_lo) * qscale).astype(BF16)
        eq = eq_ref[0, qs, :]
        for hh in range(2):
            qa = jnp.where(lane_lo, qn, eq) if hh == 0 else jnp.where(lane_lo, eq, qn)
            qt = lax.dot_general(eye_l, qa, nt, preferred_element_type=F32).astype(BF16)
            s = jnp.dot(kaug[hh, :nk, :], qt, preferred_element_type=F32)
            s_scr[qi % 2, hh, :nk, :] = s

    def softmax_pv(qi):
        qs = slice(qi * tq, (qi + 1) * tq)
        nk = (qi + 1) * tq
        par = qi % 2
        accs = []
        for hh in range(2):
            s_scr[par, hh, nk - tq:nk, :] = jnp.where(visible, s_scr[par, hh, nk - tq:nk, :], -jnp.inf)
            m = jnp.max(s_scr[par, hh, :nk, :], axis=0, keepdims=True)
            p_scr[par, hh, :nk, :] = jnp.exp2(s_scr[par, hh, :nk, :] - m).astype(BF16)
            accs.append(jnp.dot(vaugt[hh, :, :nk], p_scr[par, hh, :nk, :], preferred_element_type=F32))
        a0, a1 = accs
        o_t = jnp.concatenate([a0[:hd] / a0[hd:], a1[hd:] / a1[:hd]], axis=0).astype(BF16)
        o_ref[0, qs, :] = lax.dot_general(eye_q, o_t, nt, preferred_element_type=F32).astype(o_ref.dtype)

    scores(0)
    for qi in range(nblk):
        if qi + 1 < nblk:
            scores(qi + 1)
        softmax_pv(qi)


def fox_attention(qp, kv, ext_q, ext_k, qnw2, knw2, att_w, tq):
    b, l, _ = qp.shape
    npair = att_w // LANES
    nblk = l // tq
    blk = lambda off: pl.BlockSpec((1, l, LANES), lambda i, j: (i, 0, j + off))
    return pl.pallas_call(
        functools.partial(_attn_kernel, tq=tq),
        grid=(b, npair),
        in_specs=[
            blk(0), blk(0), blk(npair), blk(0), blk(0),
            pl.BlockSpec((1, LANES), lambda i, j: (0, 0)),
            pl.BlockSpec((1, LANES), lambda i, j: (0, 0)),
        ],
        out_specs=blk(0),
        out_shape=jax.ShapeDtypeStruct((b, l, att_w), BF16),
        scratch_shapes=[
            pltpu.VMEM((2, l, LANES), BF16),
            pltpu.VMEM((2, LANES, l), BF16),
            pltpu.VMEM((2, 2, l, tq), F32),
            pltpu.VMEM((2, 2, l, tq), BF16),
        ],
        compiler_params=_params(("parallel", "parallel"), 40),
        name="fox_attention",
    )(qp, kv, kv, ext_q, ext_k, qnw2, knw2)


def _router_kernel(h_ref, nw_ref, wr_ref, xpk_ref, topt_ref, sel_ref, top_ref):
    x = h_ref[...]
    u = x * _rms_scale(x) * nw_ref[...]
    xpk_ref[...] = _pack_bf16_pairs(u)
    logits = jnp.dot(u, wr_ref[...], preferred_element_type=F32, precision=HIGHEST)
    lt = logits.T[:N_EXPERTS, :]
    sub = lax.broadcasted_iota(I32, lt.shape, 0)
    neg = jnp.float32(-jnp.inf)
    m1 = jnp.max(lt, axis=0, keepdims=True)
    i1 = jnp.min(jnp.where(lt == m1, sub, N_EXPERTS), axis=0, keepdims=True)
    rest = jnp.where(sub == i1, neg, lt)
    m2 = jnp.max(rest, axis=0, keepdims=True)
    i2 = jnp.min(jnp.where(rest == m2, sub, N_EXPERTS), axis=0, keepdims=True)
    e2 = jnp.exp(m2 - m1)
    g1 = 1.0 / (1.0 + e2)
    g2 = e2 / (1.0 + e2)
    sel_ref[...] = jnp.where((sub == i1) | (sub == i2), 1.0, 0.0).astype(F32)
    topt = jnp.where(sub == 0, i1.astype(F32),
                     jnp.where(sub == 1, i2.astype(F32), jnp.where(sub == 2, g1, jnp.where(sub == 3, g2, 0.0))))
    topt_ref[...] = topt
    pad = jnp.zeros((LANES - N_EXPERTS, topt.shape[1]), F32)
    top_ref[...] = jnp.concatenate([topt, pad], axis=0).T


def router(h, nw, wr_pad, tm):
    t, d = h.shape
    return pl.pallas_call(
        _router_kernel,
        grid=(t // tm,),
        in_specs=[
            pl.BlockSpec((tm, d), lambda i: (i, 0)),
            pl.BlockSpec((1, d), lambda i: (0, 0)),
            pl.BlockSpec((d, LANES), lambda i: (0, 0)),
        ],
        out_specs=[
            pl.BlockSpec((tm, d // 2), lambda i: (i, 0)),
            pl.BlockSpec((N_EXPERTS, tm), lambda i: (0, i)),
            pl.BlockSpec((N_EXPERTS, tm), lambda i: (0, i)),
            pl.BlockSpec((tm, LANES), lambda i: (i, 0)),
        ],
        out_shape=[
            jax.ShapeDtypeStruct((t, d // 2), U32),
            jax.ShapeDtypeStruct((N_EXPERTS, t), F32),
            jax.ShapeDtypeStruct((N_EXPERTS, t), F32),
            jax.ShapeDtypeStruct((t, LANES), F32),
        ],
        compiler_params=_params(("parallel",), 40),
        name="router",
    )(h, nw.reshape(1, d), wr_pad)


def _rank_kernel(sel_ref, rank_ref, cnt_ref, carry):
    @pl.when(pl.program_id(0) == 0)
    def _():
        carry[...] = jnp.zeros(carry.shape, F32)

    s = sel_ref[...]
    tb = s.shape[1]
    row = lax.broadcasted_iota(I32, (tb, tb), 0)
    col = lax.broadcasted_iota(I32, (tb, tb), 1)
    before = jnp.where(row < col, 1.0, 0.0).astype(BF16)
    c = carry[...]
    rank_ref[...] = jnp.dot(s.astype(BF16), before, preferred_element_type=F32) + c[:, :1]
    c = c + jnp.sum(s, axis=1, keepdims=True)
    carry[...] = c
    cnt_ref[...] = c


def moe_rank(sel, tb):
    ne, t = sel.shape
    return pl.pallas_call(
        _rank_kernel,
        grid=(t // tb,),
        in_specs=[pl.BlockSpec((ne, tb), lambda i: (0, i))],
        out_specs=[pl.BlockSpec((ne, tb), lambda i: (0, i)), pl.BlockSpec((ne, LANES), lambda i: (0, 0))],
        out_shape=[jax.ShapeDtypeStruct((ne, t), F32), jax.ShapeDtypeStruct((ne, LANES), F32)],
        scratch_shapes=[pltpu.VMEM((ne, LANES), F32)],
        compiler_params=_params(("arbitrary",), 32),
        name="moe_rank",
    )(sel)


def _pos_kernel(topt_ref, rank_ref, off_ref, pos_ref):
    posv = rank_ref[...] + off_ref[...][:, :1]
    sub = lax.broadcasted_iota(I32, posv.shape, 0).astype(F32)
    for k in range(TOP_K):
        ek = topt_ref[k:k + 1, :]
        pos_ref[0, k:k + 1, :] = jnp.sum(jnp.where(sub == ek, posv, 0.0), axis=0, keepdims=True).astype(I32)


def moe_pos(topt, rank, off, tb):
    ne, t = rank.shape
    return pl.pallas_call(
        _pos_kernel,
        grid=(t // tb,),
        in_specs=[
            pl.BlockSpec((ne, tb), lambda i: (0, i)),
            pl.BlockSpec((ne, tb), lambda i: (0, i)),
            pl.BlockSpec((ne, LANES), lambda i: (0, 0)),
        ],
        out_specs=pl.BlockSpec((1, TOP_K, tb), lambda i: (i, 0, 0)),
        out_shape=jax.ShapeDtypeStruct((t // tb, TOP_K, tb), I32),
        compiler_params=_params(("parallel",), 32),
        name="moe_pos",
    )(topt, rank, off)


def _dispatch_kernel(x_ref, pos_hbm, init_hbm, xs_hbm, pos_smem, psem, sem, *, tb):
    del init_hbm
    i = pl.program_id(0)
    cp = pltpu.make_async_copy(pos_hbm.at[pl.ds(pl.multiple_of(i * (TOP_K * tb), TOP_K * tb), TOP_K * tb)],
                               pos_smem, psem)
    cp.start()
    cp.wait()

    def row_copy(r, k):
        return pltpu.make_async_copy(x_ref.at[pl.ds(r, 1)], xs_hbm.at[pl.ds(pos_smem[k * tb + r], 1)], sem)

    def issue(r, _):
        for k in range(TOP_K):
            row_copy(r, k).start()
        return 0

    lax.fori_loop(0, tb, issue, 0, unroll=8)

    def drain(r, _):
        for k in range(TOP_K):
            row_copy(r, k).wait()
        return 0

    lax.fori_loop(0, tb, drain, 0, unroll=8)


def moe_dispatch(xpk, pos_flat, rows, tb):
    t, dw = xpk.shape
    init = jnp.zeros((rows, dw), U32)
    return pl.pallas_call(
        functools.partial(_dispatch_kernel, tb=tb),
        grid=(t // tb,),
        in_specs=[
            pl.BlockSpec((tb, dw), lambda i: (i, 0)),
            pl.BlockSpec(memory_space=pl.ANY),
            pl.BlockSpec(memory_space=pl.ANY),
        ],
        out_specs=pl.BlockSpec(memory_space=pl.ANY),
        out_shape=jax.ShapeDtypeStruct((rows, dw), U32),
        scratch_shapes=[pltpu.SMEM((TOP_K * tb,), I32), pltpu.SemaphoreType.DMA, pltpu.SemaphoreType.DMA],
        input_output_aliases={2: 0},
        compiler_params=_params(("arbitrary",), 32),
        name="moe_dispatch",
    )(xpk, pos_flat, init)


def _expert_ffn_kernel(te_ref, nv_ref, x_ref, wg_ref, wu_ref, wd_ref, o_ref, xn_ref, acc_ref):
    del te_ref
    i = pl.program_id(0)
    j = pl.program_id(1)

    @pl.when(i < nv_ref[0])
    def _():
        @pl.when(j == 0)
        def _():
            xn_ref[...] = _unpack_bf16_pairs(x_ref[...]).astype(BF16)
            acc_ref[...] = jnp.zeros(acc_ref.shape, F32)

        xn = xn_ref[...]
        g = jnp.dot(xn, wg_ref[0], preferred_element_type=F32)
        u = jnp.dot(xn, wu_ref[0], preferred_element_type=F32)
        a = (g * _sigmoid(g) * u).astype(BF16)
        acc_ref[...] += jnp.dot(a, wd_ref[0], preferred_element_type=F32)

        @pl.when(j == pl.num_programs(1) - 1)
        def _():
            o_ref[...] = _pack_bf16_pairs(acc_ref[...])

    @pl.when((i >= nv_ref[0]) & (j == pl.num_programs(1) - 1))
    def _():
        o_ref[...] = jnp.zeros(o_ref.shape, o_ref.dtype)


def expert_ffn(xs, tile_expert, n_valid, wg, wu, wd, tm, tf):
    rows, dw = xs.shape
    ne, d, f = wg.shape
    nf = f // tf
    row_blk = lambda i, j, te, nv: (jnp.minimum(i, nv[0] - 1), 0)
    out_blk = lambda i, j, te, nv: (i, 0)
    col_j = lambda i, j, nv: jnp.where(i < nv[0], j, nf - 1)
    return pl.pallas_call(
        _expert_ffn_kernel,
        grid_spec=pltpu.PrefetchScalarGridSpec(
            num_scalar_prefetch=2,
            grid=(rows // tm, nf),
            in_specs=[
                pl.BlockSpec((tm, dw), row_blk),
                pl.BlockSpec((1, d, tf), lambda i, j, te, nv: (te[i], 0, col_j(i, j, nv))),
                pl.BlockSpec((1, d, tf), lambda i, j, te, nv: (te[i], 0, col_j(i, j, nv))),
                pl.BlockSpec((1, tf, d), lambda i, j, te, nv: (te[i], col_j(i, j, nv), 0)),
            ],
            out_specs=pl.BlockSpec((tm, dw), out_blk),
            scratch_shapes=[pltpu.VMEM((tm, d), BF16), pltpu.VMEM((tm, d), F32)],
        ),
        out_shape=jax.ShapeDtypeStruct((rows, dw), U32),
        compiler_params=_params(("arbitrary", "arbitrary"), 48),
        name="expert_ffn",
    )(tile_expert, n_valid, xs, wg, wu, wd)


def _combine_kernel(h_ref, top_ref, pos_hbm, ys_hbm, o_ref, ybuf, pos_smem, psem, sem, *, tb):
    i = pl.program_id(0)
    cp = pltpu.make_async_copy(pos_hbm.at[pl.ds(pl.multiple_of(i * (TOP_K * tb), TOP_K * tb), TOP_K * tb)],
                               pos_smem, psem)
    cp.start()
    cp.wait()

    def row_copy(r, k):
        return pltpu.make_async_copy(ys_hbm.at[pl.ds(pos_smem[k * tb + r], 1)], ybuf.at[k, pl.ds(r, 1)], sem)

    def issue(r, _):
        for k in range(TOP_K):
            row_copy(r, k).start()
        return 0

    lax.fori_loop(0, tb, issue, 0, unroll=8)

    def drain(r, _):
        for k in range(TOP_K):
            row_copy(r, k).wait()
        return 0

    lax.fori_loop(0, tb, drain, 0, unroll=8)

    top = top_ref[...]
    out = h_ref[...]
    for k in range(TOP_K):
        out = out + top[:, TOP_K + k:TOP_K + k + 1] * _unpack_bf16_pairs(ybuf[k])
    o_ref[...] = out


def moe_combine(h, top, pos_flat, ys, tb):
    t, d = h.shape
    dw = ys.shape[1]
    return pl.pallas_call(
        functools.partial(_combine_kernel, tb=tb),
        grid=(t // tb,),
        in_specs=[
            pl.BlockSpec((tb, d), lambda i: (i, 0)),
            pl.BlockSpec((tb, LANES), lambda i: (i, 0)),
            pl.BlockSpec(memory_space=pl.ANY),
            pl.BlockSpec(memory_space=pl.ANY),
        ],
        out_specs=pl.BlockSpec((tb, d), lambda i: (i, 0)),
        out_shape=jax.ShapeDtypeStruct((t, d), F32),
        scratch_shapes=[pltpu.VMEM((TOP_K, tb, dw), U32), pltpu.SMEM((TOP_K * tb,), I32),
                        pltpu.SemaphoreType.DMA, pltpu.SemaphoreType.DMA],
        compiler_params=_params(("arbitrary",), 32),
        name="moe_combine",
    )(h, top, pos_flat, ys)


def moe(h, nw, w_router, wg, wu, wd, tm_route, tm_expert, tb, tf):
    t, d = h.shape
    xpk, topt, sel, top = router(h, nw, _pad_cols(w_router, LANES), tm_route)
    rank, cnt = moe_rank(sel, tb)
    counts = cnt[:, 0].astype(I32)
    padded = ((counts + tm_expert - 1) // tm_expert) * tm_expert
    ends = jnp.cumsum(padded)
    n_tiles = (TOP_K * t) // tm_expert + N_EXPERTS
    n_valid = (ends[-1] // tm_expert).astype(I32)
    tile_start = jnp.arange(n_tiles, dtype=I32) * tm_expert
    tile_expert = jnp.minimum(jnp.searchsorted(ends, jnp.minimum(tile_start, ends[-1] - 1), side="right"),
                              N_EXPERTS - 1).astype(I32)
    off = jnp.broadcast_to((ends - padded).astype(F32)[:, None], (N_EXPERTS, LANES))
    pos_flat = moe_pos(topt, rank, off, tb).reshape(-1)
    xs = moe_dispatch(xpk, pos_flat, n_tiles * tm_expert, tb)
    ys = expert_ffn(xs, tile_expert, n_valid.reshape(1), wg, wu, wd, tm_expert, tf)
    return moe_combine(h, top, pos_flat, ys, tb)


def _pad_cols(w, n):
    return jnp.pad(w, ((0, 0), (0, n - w.shape[1])))


def _tile3(v, n):
    return jnp.pad(jnp.tile(v, 3), (0, LANES - 3 * n)).reshape(1, LANES)


def kernel(x, p, ssm_norm_w, ssm_w_in, ssm_conv_w, ssm_conv_b, ssm_dt_bias, ssm_a_log, ssm_d, ssm_gn_w, ssm_w_out, kv_norm_w, w_kv, b_f, k_norm_w, att_norm_w, att_w_q, q_norm_w, att_w_o, ffn_norm_w, ffn_w_gate, ffn_w_up, ffn_w_down, moe_norm_w, moe_w_router, moe_w_gate, moe_w_up, moe_w_down, ple_norm_w, ple_w_gate, ple_w_proj):
    b, l, d = x.shape
    t = b * l
    n_a = ssm_norm_w.shape[0]
    depth = ple_norm_w.shape[0]
    d_inner = ssm_gn_w.shape[1]
    n_heads = ssm_dt_bias.shape[1]
    cdim = ssm_conv_w.shape[2]
    att_w = att_w_q.shape[2]
    n_att_heads = att_w // ATT_HEADDIM
    tm = min(1024, t)
    tq = min(256, l)

    h = x.reshape(t, d)
    r = jnp.arange(LANES)[:, None]
    c = jnp.arange(d_inner)[None, :]
    expand = ((r < 3 * n_heads) & ((r % n_heads) == (c // SSM_HEADDIM))).astype(BF16)

    kv = ext_q = ext_k = None
    for i in range(depth):
        if i < n_a:
            w_in = ssm_w_in[i].astype(BF16)
            w_dt = w_in[:, d_inner + cdim:]
            w_dt3 = _pad_cols(jnp.concatenate([w_dt, w_dt, w_dt], axis=1), LANES)
            z, xbc, dt_raw = norm_matmul(h, ssm_norm_w[i], [w_in[:, :d_inner], w_in[:, d_inner:d_inner + cdim], w_dt3],
                                         [BF16, BF16, F32], tm // 2)
            y = ssd(z.reshape(b, l, d_inner), xbc.reshape(b, l, cdim), dt_raw.reshape(b, l, LANES),
                    ssm_conv_w[i], ssm_conv_b[i].reshape(1, cdim), _tile3(ssm_dt_bias[i], n_heads),
                    _tile3(ssm_a_log[i], n_heads), jnp.repeat(ssm_d[i], SSM_HEADDIM).reshape(1, d_inner),
                    ssm_gn_w[i].reshape(1, d_inner), expand)
            h = matmul_residual(h, y.reshape(t, d_inner), ssm_w_out[i].astype(BF16), tm)
        else:
            j = i - n_a
            if i == n_a:
                wkv = w_kv.astype(BF16)
                w_f = wkv[:, 2 * att_w:]
                w_f3 = _pad_cols(jnp.concatenate([w_f, w_f, w_f], axis=1), LANES)
                kv, f = norm_matmul(h, kv_norm_w, [wkv[:, :2 * att_w], w_f3], [BF16, F32], tm)
                kv = kv.reshape(b, l, 2 * att_w)
                ext_q, ext_k = forget_ext(f.reshape(b, l, LANES), _tile3(b_f, n_att_heads), n_att_heads)
            qp, = norm_matmul(h, att_norm_w[j], [att_w_q[j].astype(BF16)], [BF16], tm)
            qp = qp.reshape(b, l, att_w)
            qnw2 = jnp.tile(q_norm_w[j], 2).reshape(1, LANES)
            knw2 = jnp.tile(k_norm_w, 2).reshape(1, LANES)
            o = fox_attention(qp, kv, ext_q, ext_k, qnw2, knw2, att_w, tq)
            h = matmul_residual(h, o.reshape(t, att_w), att_w_o[j].astype(BF16), tm)
        if i % 2 == 0:
            dd = i // 2
            h = ffn(h, ffn_norm_w[dd], ffn_w_gate[dd].astype(BF16), ffn_w_up[dd].astype(BF16),
                    ffn_w_down[dd].astype(BF16), tm, 512)
        else:
            m = i // 2
            h = moe(h, moe_norm_w[m], moe_w_router[m], moe_w_gate[m].astype(BF16), moe_w_up[m].astype(BF16),
                    moe_w_down[m].astype(BF16), tm, tm, min(512, t), 512)
        h = ple(h, p[i].reshape(t, p.shape[-1]), ple_norm_w[i], ple_w_gate[i].astype(BF16),
                ple_w_proj[i].astype(BF16), tm)
    return h.reshape(b, l, d)
```

```python
import functools
import math

import numpy as np
import jax
import jax.numpy as jnp
from jax import lax
from jax.experimental import pallas as pl
from jax.experimental.pallas import tpu as pltpu

F32 = jnp.float32
BF16 = jnp.bfloat16
I32 = jnp.int32
U32 = jnp.uint32
EPS = 1e-6
HIGHEST = lax.Precision.HIGHEST
LOG2E = math.log2(math.e)

LANES = 128
SUBLANES = 8
MIB = 1024 * 1024

SSM_HEADDIM = 64
SSM_GROUPS = 4
SSM_STATE = 128
CONV_WIDTH = 4
CHUNK = 128
ATT_HEADDIM = 64
N_EXPERTS = 8
TOP_K = 2


def _params(sem, vmem_mib):
    return pltpu.CompilerParams(dimension_semantics=sem, vmem_limit_bytes=vmem_mib * MIB)


def _rms_scale(x):
    return lax.rsqrt(jnp.mean(x * x, axis=-1, keepdims=True) + EPS)


def _sigmoid(x):
    return 1.0 / (1.0 + jnp.exp2(x * (-LOG2E)))


def _softplus(x):
    return jnp.maximum(x, 0.0) + jnp.log(1.0 + jnp.exp(-jnp.abs(x)))


def _pack_bf16_pairs(x):
    n = x.shape[1] // 2
    bits = lax.bitcast_convert_type(x.astype(BF16).astype(F32), U32)
    return lax.shift_right_logical(bits[:, :n], jnp.uint32(16)) | (bits[:, n:] & jnp.uint32(0xFFFF0000))


def _unpack_bf16_pairs(w):
    lo = lax.bitcast_convert_type(lax.shift_left(w, jnp.uint32(16)), F32)
    hi = lax.bitcast_convert_type(w & jnp.uint32(0xFFFF0000), F32)
    return jnp.concatenate([lo, hi], axis=1)


def _norm_matmul_kernel(h_ref, nw_ref, *refs):
    n = len(refs) // 2
    x = h_ref[...]
    xn = (x * _rms_scale(x) * nw_ref[...]).astype(BF16)
    for w_ref, o_ref in zip(refs[:n], refs[n:]):
        o_ref[...] = jnp.dot(xn, w_ref[...], preferred_element_type=F32).astype(o_ref.dtype)


def norm_matmul(h, nw, ws, out_dtypes, tm):
    t, d = h.shape
    resident = lambda shape: pl.BlockSpec(shape, lambda i: (0, 0), pipeline_mode=pl.Buffered(1))
    return pl.pallas_call(
        _norm_matmul_kernel,
        grid=(t // tm,),
        in_specs=[pl.BlockSpec((tm, d), lambda i: (i, 0)), resident((1, d))] + [resident(w.shape) for w in ws],
        out_specs=[pl.BlockSpec((tm, w.shape[1]), lambda i: (i, 0)) for w in ws],
        out_shape=[jax.ShapeDtypeStruct((t, w.shape[1]), dt) for w, dt in zip(ws, out_dtypes)],
        compiler_params=_params(("parallel",), 48),
        name="norm_matmul",
    )(h, nw.reshape(1, d), *ws)


def _matmul_residual_kernel(h_ref, y_ref, w_ref, o_ref):
    o_ref[...] = h_ref[...] + jnp.dot(y_ref[...], w_ref[...], preferred_element_type=F32)


def matmul_residual(h, y, w, tm):
    t, n = h.shape
    k = y.shape[1]
    return pl.pallas_call(
        _matmul_residual_kernel,
        grid=(t // tm,),
        in_specs=[
            pl.BlockSpec((tm, n), lambda i: (i, 0)),
            pl.BlockSpec((tm, k), lambda i: (i, 0)),
            pl.BlockSpec((k, n), lambda i: (0, 0)),
        ],
        out_specs=pl.BlockSpec((tm, n), lambda i: (i, 0)),
        out_shape=jax.ShapeDtypeStruct((t, n), F32),
        compiler_params=_params(("parallel",), 48),
        name="matmul_residual",
    )(h, y, w)


def _ffn_kernel(h_ref, nw_ref, wg_ref, wu_ref, wd_ref, o_ref, xn_ref, acc_ref):
    j = pl.program_id(1)

    @pl.when(j == 0)
    def _():
        x = h_ref[...]
        xn_ref[...] = (x * _rms_scale(x) * nw_ref[...]).astype(xn_ref.dtype)
        acc_ref[...] = x

    xn = xn_ref[...]
    g = jnp.dot(xn, wg_ref[...], preferred_element_type=F32)
    u = jnp.dot(xn, wu_ref[...], preferred_element_type=F32)
    a = (g * _sigmoid(g) * u).astype(BF16)
    acc_ref[...] += jnp.dot(a, wd_ref[...], preferred_element_type=F32)

    @pl.when(j == pl.num_programs(1) - 1)
    def _():
        o_ref[...] = acc_ref[...]


def ffn(h, nw, wg, wu, wd, tm, tf):
    t, d = h.shape
    f = wg.shape[1]
    return pl.pallas_call(
        _ffn_kernel,
        grid=(t // tm, f // tf),
        in_specs=[
            pl.BlockSpec((tm, d), lambda i, j: (i, 0)),
            pl.BlockSpec((1, d), lambda i, j: (0, 0)),
            pl.BlockSpec((d, tf), lambda i, j: (0, j)),
            pl.BlockSpec((d, tf), lambda i, j: (0, j)),
            pl.BlockSpec((tf, d), lambda i, j: (j, 0)),
        ],
        out_specs=pl.BlockSpec((tm, d), lambda i, j: (i, 0)),
        out_shape=jax.ShapeDtypeStruct((t, d), F32),
        scratch_shapes=[pltpu.VMEM((tm, d), BF16), pltpu.VMEM((tm, d), F32)],
        compiler_params=_params(("parallel", "arbitrary"), 48),
        name="ffn",
    )(h, nw.reshape(1, d), wg, wu, wd)


def _ple_kernel(h_ref, p_ref, nw_ref, wg_ref, wp_ref, o_ref):
    x = h_ref[...]
    xn = (x * _rms_scale(x) * nw_ref[...]).astype(BF16)
    gate = _sigmoid(jnp.dot(xn, wg_ref[...], preferred_element_type=F32))
    proj = jnp.dot(p_ref[...].astype(BF16), wp_ref[...], preferred_element_type=F32)
    o_ref[...] = x + proj * gate


def ple(h, p, nw, wg, wp, tm):
    t, d = h.shape
    pd = p.shape[1]
    return pl.pallas_call(
        _ple_kernel,
        grid=(t // tm,),
        in_specs=[
            pl.BlockSpec((tm, d), lambda i: (i, 0)),
            pl.BlockSpec((tm, pd), lambda i: (i, 0)),
            pl.BlockSpec((1, d), lambda i: (0, 0)),
            pl.BlockSpec((d, d), lambda i: (0, 0)),
            pl.BlockSpec((pd, d), lambda i: (0, 0)),
        ],
        out_specs=pl.BlockSpec((tm, d), lambda i: (i, 0)),
        out_shape=jax.ShapeDtypeStruct((t, d), F32),
        compiler_params=_params(("parallel",), 40),
        name="ple",
    )(h, p, nw.reshape(1, d), wg, wp)


def _split3(v, n):
    v1 = v.astype(BF16)
    r1 = v - v1.astype(F32)
    v2 = r1.astype(BF16)
    v3 = (r1 - v2.astype(F32)).astype(BF16)
    lane = lax.broadcasted_iota(I32, v.shape, 1)
    return jnp.where(lane < n, v1, jnp.where(lane < 2 * n, v2, v3))


def _ssd_kernel(z_ref, xbc_ref, dt_ref, cw_ref, cb_ref, dtb_ref, alog_ref, dskip_ref, gnw_ref, exp_ref,
                o_ref, cbuf, xact, state, acs_t, y_scr):
    d_inner = z_ref.shape[-1]
    gw = d_inner // SSM_GROUPS
    n_heads = d_inner // SSM_HEADDIM
    q = CHUNK

    tail = cbuf.shape[0] - q

    @pl.when(pl.program_id(1) == 0)
    def _():
        cbuf[0:tail, :] = jnp.zeros((tail, cbuf.shape[1]), BF16)
        state[...] = jnp.zeros(state.shape, F32)

    cbuf[tail:, :] = xbc_ref[0]
    cdim = cbuf.shape[1]
    nshift = CONV_WIDTH - 1
    srow = lax.broadcasted_iota(I32, (nshift * q, tail + q), 0)
    scol = lax.broadcasted_iota(I32, (nshift * q, tail + q), 1)
    shift = jnp.where(scol == srow - (srow // q) * (q - 1) + (tail - nshift), 1.0, 0.0).astype(BF16)
    cw_lanes = 2 * LANES
    for j in range(cdim // cw_lanes):
        sl = slice(j * cw_lanes, (j + 1) * cw_lanes)
        shifted = jnp.dot(shift, cbuf[:, sl], preferred_element_type=F32)
        acc = cb_ref[:, sl] + cbuf[tail:, sl].astype(F32) * cw_ref[nshift:nshift + 1, sl]
        for w in range(nshift):
            acc = acc + shifted[w * q:(w + 1) * q] * cw_ref[w:w + 1, sl]
        xact[:, sl] = acc * _sigmoid(acc)
    cbuf[0:tail, :] = cbuf[q:q + tail, :]

    dt = _softplus(dt_ref[0] + dtb_ref[...])
    a = dt * (-LOG2E * jnp.exp(alog_ref[...]))
    row = lax.broadcasted_iota(I32, (q, q), 0)
    col = lax.broadcasted_iota(I32, (q, q), 1)
    causal = row >= col
    tril = jnp.where(causal, 1.0, 0.0).astype(F32)
    acs = jnp.dot(tril, a, preferred_element_type=F32, precision=HIGHEST)
    acs_t[...] = acs.T
    a_last = acs[q - 1:q, :]
    dt3 = _split3(dt, n_heads)
    dec3 = _split3(jnp.exp2(a_last - acs), n_heads)
    e3 = _split3(jnp.exp2(acs), n_heads)

    for g in range(SSM_GROUPS):
        gl = slice(g * gw, (g + 1) * gw)
        ex = exp_ref[:, gl]
        xs = xact[:, gl]
        b_g = xact[:, d_inner + g * SSM_STATE:d_inner + (g + 1) * SSM_STATE].astype(BF16)
        c_g = xact[:, d_inner + (SSM_GROUPS + g) * SSM_STATE:d_inner + (SSM_GROUPS + g + 1) * SSM_STATE].astype(BF16)
        dt_e = jnp.dot(dt3, ex, preferred_element_type=F32)
        dec_e = jnp.dot(dec3, ex, preferred_element_type=F32)
        e_e = jnp.dot(e3, ex, preferred_element_type=F32)
        xc = xs * dt_e
        xc_b = xc.astype(BF16)
        xdec_b = (xc * dec_e).astype(BF16)
        scores = lax.dot_general(c_g, b_g, (((1,), (1,)), ((), ())), preferred_element_type=F32)
        st = state[g]
        y = jnp.dot(c_g, st.astype(BF16), preferred_element_type=F32) * e_e
        y = y + xs * dskip_ref[:, gl]
        lane = lax.broadcasted_iota(I32, (q, LANES), 1)
        ydiag = []
        for pr in range(gw // LANES):
            ms = []
            for hh in range(2):
                h = g * (gw // SSM_HEADDIM) + 2 * pr + hh
                seg = acs[:, h:h + 1] - acs_t[h:h + 1, :]
                ms.append(scores * jnp.exp2(jnp.where(causal, seg, -jnp.inf)))
            lhs = jnp.concatenate(ms, axis=1).astype(BF16)
            xp = xc_b[:, pr * LANES:(pr + 1) * LANES]
            zero = jnp.zeros_like(xp)
            rhs = jnp.concatenate([jnp.where(lane < SSM_HEADDIM, xp, zero),
                                   jnp.where(lane >= SSM_HEADDIM, xp, zero)], axis=0)
            ydiag.append(jnp.dot(lhs, rhs, preferred_element_type=F32))
        y_scr[:, gl] = y + jnp.concatenate(ydiag, axis=1)
        upd = lax.dot_general(b_g, xdec_b, (((0,), (0,)), ((), ())), preferred_element_type=F32)
        state[g] = st * e_e[q - 1:q, :] + upd

    zz = z_ref[0].astype(F32)
    y = y_scr[...] * (zz * _sigmoid(zz))
    o_ref[0] = (y * _rms_scale(y) * gnw_ref[...]).astype(o_ref.dtype)


def ssd(z, xbc, dt_raw, conv_w, conv_b, dt_bias3, a_log3, dskip_e, gn_w, expand):
    b, l, d_inner = z.shape
    cdim = xbc.shape[-1]
    gw = d_inner // SSM_GROUPS
    nc = l // CHUNK
    full = lambda shape: pl.BlockSpec(shape, lambda i, c: (0,) * len(shape))
    return pl.pallas_call(
        _ssd_kernel,
        grid=(b, nc),
        in_specs=[
            pl.BlockSpec((1, CHUNK, d_inner), lambda i, c: (i, c, 0)),
            pl.BlockSpec((1, CHUNK, cdim), lambda i, c: (i, c, 0)),
            pl.BlockSpec((1, CHUNK, LANES), lambda i, c: (i, c, 0)),
            full((CONV_WIDTH, cdim)), full((1, cdim)), full((1, LANES)), full((1, LANES)),
            full((1, d_inner)), full((1, d_inner)), full((LANES, d_inner)),
        ],
        out_specs=pl.BlockSpec((1, CHUNK, d_inner), lambda i, c: (i, c, 0)),
        out_shape=jax.ShapeDtypeStruct((b, l, d_inner), BF16),
        scratch_shapes=[
            pltpu.VMEM((CHUNK + 16, cdim), BF16),
            pltpu.VMEM((CHUNK, cdim), F32),
            pltpu.VMEM((SSM_GROUPS, SSM_STATE, gw), F32),
            pltpu.VMEM((LANES, CHUNK), F32),
            pltpu.VMEM((CHUNK, d_inner), F32),
        ],
        compiler_params=_params(("parallel", "arbitrary"), 40),
        name="ssd",
    )(z, xbc, dt_raw, conv_w, conv_b, dt_bias3, a_log3, dskip_e, gn_w, expand)


def _cum_kernel(f_ref, bf_ref, pq_ref, pk_ref, eq_ref, ek_ref, *, blk, n_heads):
    l = f_ref.shape[1]
    row = lax.broadcasted_iota(I32, (blk, blk), 0)
    col = lax.broadcasted_iota(I32, (blk, blk), 1)
    tril = jnp.where(row >= col, 1.0, 0.0).astype(F32)
    lane = lax.broadcasted_iota(I32, (blk, LANES), 1)
    carry = jnp.zeros((1, LANES), F32)
    for i in range(l // blk):
        rs = slice(i * blk, (i + 1) * blk)
        x = f_ref[0, rs, :] + bf_ref[...]
        lf = -_softplus(-x)
        cs = jnp.dot(tril, lf, preferred_element_type=F32, precision=HIGHEST) + carry
        carry = cs[blk - 1:blk, :]
        lhs = jnp.where(lane == 3 * n_heads, jnp.ones((), BF16), _split3(cs * LOG2E, n_heads))
        eq_ref[0, rs, :] = jnp.dot(lhs, pq_ref[...], preferred_element_type=F32).astype(BF16)
        ek_ref[0, rs, :] = jnp.dot(lhs, pk_ref[...], preferred_element_type=F32).astype(BF16)


def _ext_placement(n_heads):
    w = n_heads * ATT_HEADDIM
    pq = np.zeros((LANES, w), np.float32)
    pk = np.zeros((LANES, w), np.float32)
    one = 3 * n_heads
    for h in range(n_heads):
        base = (h // 2) * LANES + (ATT_HEADDIM if h % 2 == 0 else 0)
        for piece in range(3):
            pq[piece * n_heads + h, base + piece] = 1.0
            pq[one, base + 3 + piece] = 1.0
            pk[one, base + piece] = 1.0
            pk[piece * n_heads + h, base + 3 + piece] = -1.0
    return jnp.asarray(pq, BF16), jnp.asarray(pk, BF16)


def forget_ext(f, b_f3, n_heads):
    b, l, _ = f.shape
    w = n_heads * ATT_HEADDIM
    pq, pk = _ext_placement(n_heads)
    out = jax.ShapeDtypeStruct((b, l, w), BF16)
    return pl.pallas_call(
        functools.partial(_cum_kernel, blk=256, n_heads=n_heads),
        grid=(b,),
        in_specs=[
            pl.BlockSpec((1, l, LANES), lambda i: (i, 0, 0)),
            pl.BlockSpec((1, LANES), lambda i: (0, 0)),
            pl.BlockSpec((LANES, w), lambda i: (0, 0)),
            pl.BlockSpec((LANES, w), lambda i: (0, 0)),
        ],
        out_specs=[pl.BlockSpec((1, l, w), lambda i: (i, 0, 0)), pl.BlockSpec((1, l, w), lambda i: (i, 0, 0))],
        out_shape=[out, out],
        compiler_params=_params(("parallel",), 40),
        name="forget_ext",
    )(f, b_f3, pq, pk)


def _head_rms(x, lane_lo):
    sq = x * x
    s_lo = jnp.sum(jnp.where(lane_lo, sq, 0.0), axis=-1, keepdims=True)
    s_hi = jnp.sum(jnp.where(lane_lo, 0.0, sq), axis=-1, keepdims=True)
    inv = 1.0 / ATT_HEADDIM
    return jnp.where(lane_lo, lax.rsqrt(s_lo * inv + EPS), lax.rsqrt(s_hi * inv + EPS))


def _attn_kernel(q_ref, k_ref, v_ref, eq_ref, ek_ref, qnw_ref, knw_ref, o_ref, kaug, vaugt, s_scr, p_scr, *, tq):
    l = q_ref.shape[1]
    nblk = l // tq
    hd = ATT_HEADDIM
    lane_lo = lax.broadcasted_iota(I32, (tq, LANES), 1) < hd
    sub_lo = lax.broadcasted_iota(I32, (LANES, tq), 0) < hd
    row = lax.broadcasted_iota(I32, (tq, tq), 0)
    col = lax.broadcasted_iota(I32, (tq, tq), 1)
    visible = row <= col
    eye_q = jnp.where(row == col, 1.0, 0.0).astype(BF16)
    eye_l = eye_q[:LANES, :LANES]
    nt = (((1,), (1,)), ((), ()))
    qscale = qnw_ref[...] * (hd ** -0.5 * LOG2E)

    for i in range(nblk):
        rs = slice(i * tq, (i + 1) * tq)
        kk = k_ref[0, rs, :].astype(F32)
        kn = (kk * _head_rms(kk, lane_lo) * knw_ref[...]).astype(BF16)
        ek = ek_ref[0, rs, :]
        kaug[0, rs, :] = jnp.where(lane_lo, kn, ek)
        kaug[1, rs, :] = jnp.where(lane_lo, ek, kn)
        vt = lax.dot_general(eye_l, v_ref[0, rs, :], nt, preferred_element_type=F32).astype(BF16)
        one = jnp.ones_like(vt)
        vaugt[0, :, rs] = jnp.where(sub_lo, vt, one)
        vaugt[1, :, rs] = jnp.where(sub_lo, one, vt)

    def scores(qi):
        qs = slice(qi * tq, (qi + 1) * tq)
        nk = (qi + 1) * tq
        qq = q_ref[0, qs, :].astype(F32)
        qn = (qq * _head_rms(qq, lane_lo) * qscale).astype(BF16)
        eq = eq_ref[0, qs, :]
        for hh in range(2):
            qa = jnp.where(lane_lo, qn, eq) if hh == 0 else jnp.where(lane_lo, eq, qn)
            qt = lax.dot_general(eye_l, qa, nt, preferred_element_type=F32).astype(BF16)
            s = jnp.dot(kaug[hh, :nk, :], qt, preferred_element_type=F32)
            s_scr[qi % 2, hh, :nk, :] = s

    def softmax_pv(qi):
        qs = slice(qi * tq, (qi + 1) * tq)
        nk = (qi + 1) * tq
        par = qi % 2
        accs = []
        for hh in range(2):
            s_scr[par, hh, nk - tq:nk, :] = jnp.where(visible, s_scr[par, hh, nk - tq:nk, :], -jnp.inf)
            m = jnp.max(s_scr[par, hh, :nk, :], axis=0, keepdims=True)
            p_scr[par, hh, :nk, :] = jnp.exp2(s_scr[par, hh, :nk, :] - m).astype(BF16)
            accs.append(jnp.dot(vaugt[hh, :, :nk], p_scr[par, hh, :nk, :], preferred_element_type=F32))
        a0, a1 = accs
        o_t = jnp.concatenate([a0[:hd] / a0[hd:], a1[hd:] / a1[:hd]], axis=0).astype(BF16)
        o_ref[0, qs, :] = lax.dot_general(eye_q, o_t, nt, preferred_element_type=F32).astype(o_ref.dtype)

    scores(0)
    for qi in range(nblk):
        if qi + 1 < nblk:
            scores(qi + 1)
        softmax_pv(qi)


def fox_attention(qp, kv, ext_q, ext_k, qnw2, knw2, att_w, tq):
    b, l, _ = qp.shape
    npair = att_w // LANES
    nblk = l // tq
    blk = lambda off: pl.BlockSpec((1, l, LANES), lambda i, j: (i, 0, j + off))
    return pl.pallas_call(
        functools.partial(_attn_kernel, tq=tq),
        grid=(b, npair),
        in_specs=[
            blk(0), blk(0), blk(npair), blk(0), blk(0),
            pl.BlockSpec((1, LANES), lambda i, j: (0, 0)),
            pl.BlockSpec((1, LANES), lambda i, j: (0, 0)),
        ],
        out_specs=blk(0),
        out_shape=jax.ShapeDtypeStruct((b, l, att_w), BF16),
        scratch_shapes=[
            pltpu.VMEM((2, l, LANES), BF16),
            pltpu.VMEM((2, LANES, l), BF16),
            pltpu.VMEM((2, 2, l, tq), F32),
            pltpu.VMEM((2, 2, l, tq), BF16),
        ],
        compiler_params=_params(("parallel", "parallel"), 40),
        name="fox_attention",
    )(qp, kv, kv, ext_q, ext_k, qnw2, knw2)


def _router_kernel(h_ref, nw_ref, wr_ref, xpk_ref, topt_ref, sel_ref, top_ref):
    x = h_ref[...]
    u = x * _rms_scale(x) * nw_ref[...]
    xpk_ref[...] = _pack_bf16_pairs(u)
    logits = jnp.dot(u, wr_ref[...], preferred_element_type=F32, precision=HIGHEST)
    lt = logits.T[:N_EXPERTS, :]
    sub = lax.broadcasted_iota(I32, lt.shape, 0)
    neg = jnp.float32(-jnp.inf)
    m1 = jnp.max(lt, axis=0, keepdims=True)
    i1 = jnp.min(jnp.where(lt == m1, sub, N_EXPERTS), axis=0, keepdims=True)
    rest = jnp.where(sub == i1, neg, lt)
    m2 = jnp.max(rest, axis=0, keepdims=True)
    i2 = jnp.min(jnp.where(rest == m2, sub, N_EXPERTS), axis=0, keepdims=True)
    e2 = jnp.exp(m2 - m1)
    g1 = 1.0 / (1.0 + e2)
    g2 = e2 / (1.0 + e2)
    sel_ref[...] = jnp.where((sub == i1) | (sub == i2), 1.0, 0.0).astype(F32)
    topt = jnp.where(sub == 0, i1.astype(F32),
                     jnp.where(sub == 1, i2.astype(F32), jnp.where(sub == 2, g1, jnp.where(sub == 3, g2, 0.0))))
    topt_ref[...] = topt
    pad = jnp.zeros((LANES - N_EXPERTS, topt.shape[1]), F32)
    top_ref[...] = jnp.concatenate([topt, pad], axis=0).T


def router(h, nw, wr_pad, tm):
    t, d = h.shape
    return pl.pallas_call(
        _router_kernel,
        grid=(t // tm,),
        in_specs=[
            pl.BlockSpec((tm, d), lambda i: (i, 0)),
            pl.BlockSpec((1, d), lambda i: (0, 0)),
            pl.BlockSpec((d, LANES), lambda i: (0, 0)),
        ],
        out_specs=[
            pl.BlockSpec((tm, d // 2), lambda i: (i, 0)),
            pl.BlockSpec((N_EXPERTS, tm), lambda i: (0, i)),
            pl.BlockSpec((N_EXPERTS, tm), lambda i: (0, i)),
            pl.BlockSpec((tm, LANES), lambda i: (i, 0)),
        ],
        out_shape=[
            jax.ShapeDtypeStruct((t, d // 2), U32),
            jax.ShapeDtypeStruct((N_EXPERTS, t), F32),
            jax.ShapeDtypeStruct((N_EXPERTS, t), F32),
            jax.ShapeDtypeStruct((t, LANES), F32),
        ],
        compiler_params=_params(("parallel",), 40),
        name="router",
    )(h, nw.reshape(1, d), wr_pad)


def _rank_kernel(sel_ref, rank_ref, cnt_ref, carry):
    @pl.when(pl.program_id(0) == 0)
    def _():
        carry[...] = jnp.zeros(carry.shape, F32)

    s = sel_ref[...]
    tb = s.shape[1]
    row = lax.broadcasted_iota(I32, (tb, tb), 0)
    col = lax.broadcasted_iota(I32, (tb, tb), 1)
    before = jnp.where(row < col, 1.0, 0.0).astype(BF16)
    c = carry[...]
    rank_ref[...] = jnp.dot(s.astype(BF16), before, preferred_element_type=F32) + c[:, :1]
    c = c + jnp.sum(s, axis=1, keepdims=True)
    carry[...] = c
    cnt_ref[...] = c


def moe_rank(sel, tb):
    ne, t = sel.shape
    return pl.pallas_call(
        _rank_kernel,
        grid=(t // tb,),
        in_specs=[pl.BlockSpec((ne, tb), lambda i: (0, i))],
        out_specs=[pl.BlockSpec((ne, tb), lambda i: (0, i)), pl.BlockSpec((ne, LANES), lambda i: (0, 0))],
        out_shape=[jax.ShapeDtypeStruct((ne, t), F32), jax.ShapeDtypeStruct((ne, LANES), F32)],
        scratch_shapes=[pltpu.VMEM((ne, LANES), F32)],
        compiler_params=_params(("arbitrary",), 32),
        name="moe_rank",
    )(sel)


def _pos_kernel(topt_ref, rank_ref, off_ref, pos_ref):
    posv = rank_ref[...] + off_ref[...][:, :1]
    sub = lax.broadcasted_iota(I32, posv.shape, 0).astype(F32)
    for k in range(TOP_K):
        ek = topt_ref[k:k + 1, :]
        pos_ref[0, k:k + 1, :] = jnp.sum(jnp.where(sub == ek, posv, 0.0), axis=0, keepdims=True).astype(I32)


def moe_pos(topt, rank, off, tb):
    ne, t = rank.shape
    return pl.pallas_call(
        _pos_kernel,
        grid=(t // tb,),
        in_specs=[
            pl.BlockSpec((ne, tb), lambda i: (0, i)),
            pl.BlockSpec((ne, tb), lambda i: (0, i)),
            pl.BlockSpec((ne, LANES), lambda i: (0, 0)),
        ],
        out_specs=pl.BlockSpec((1, TOP_K, tb), lambda i: (i, 0, 0)),
        out_shape=jax.ShapeDtypeStruct((t // tb, TOP_K, tb), I32),
        compiler_params=_params(("parallel",), 32),
        name="moe_pos",
    )(topt, rank, off)


def _pos_copy(pos_hbm, pos_smem, psem, tile, slot, n):
    return pltpu.make_async_copy(pos_hbm.at[pl.ds(pl.multiple_of(tile * n, n), n)], pos_smem.at[slot], psem.at[slot])


def _dispatch_kernel(x_ref, pos_hbm, init_hbm, xs_hbm, pos_smem, psem, sem, *, tb):
    del init_hbm
    i = pl.program_id(0)
    nsteps = pl.num_programs(0)
    slot = i % 2
    pos_copy = functools.partial(_pos_copy, pos_hbm, pos_smem, psem, n=TOP_K * tb)

    @pl.when(i == 0)
    def _():
        pos_copy(0, 0).start()

    @pl.when(i + 1 < nsteps)
    def _():
        pos_copy(i + 1, 1 - slot).start()

    pos_copy(i, slot).wait()

    def row_copy(r, k):
        return pltpu.make_async_copy(x_ref.at[pl.ds(r, 1)], xs_hbm.at[pl.ds(pos_smem[slot, k * tb + r], 1)], sem)

    def issue(r, _):
        for k in range(TOP_K):
            row_copy(r, k).start(priority=k)
        return 0

    lax.fori_loop(0, tb, issue, 0, unroll=8)

    def drain(r, _):
        for k in range(TOP_K):
            row_copy(r, k).wait()
        return 0

    lax.fori_loop(0, tb, drain, 0, unroll=8)


def moe_dispatch(xpk, pos_flat, rows, tb):
    t, dw = xpk.shape
    init = jnp.zeros((rows, dw), U32)
    return pl.pallas_call(
        functools.partial(_dispatch_kernel, tb=tb),
        grid=(t // tb,),
        in_specs=[
            pl.BlockSpec((tb, dw), lambda i: (i, 0)),
            pl.BlockSpec(memory_space=pl.ANY),
            pl.BlockSpec(memory_space=pl.ANY),
        ],
        out_specs=pl.BlockSpec(memory_space=pl.ANY),
        out_shape=jax.ShapeDtypeStruct((rows, dw), U32),
        scratch_shapes=[pltpu.SMEM((2, TOP_K * tb), I32), pltpu.SemaphoreType.DMA((2,)), pltpu.SemaphoreType.DMA],
        input_output_aliases={2: 0},
        compiler_params=_params(("arbitrary",), 32),
        name="moe_dispatch",
    )(xpk, pos_flat, init)


def _expert_ffn_kernel(te_ref, nv_ref, x_ref, wg_ref, wu_ref, wd_ref, o_ref, xn_ref, acc_ref):
    del te_ref
    i = pl.program_id(0)
    j = pl.program_id(1)

    @pl.when(i < nv_ref[0])
    def _():
        @pl.when(j == 0)
        def _():
            xn_ref[...] = _unpack_bf16_pairs(x_ref[...]).astype(BF16)
            acc_ref[...] = jnp.zeros(acc_ref.shape, F32)

        xn = xn_ref[...]
        g = jnp.dot(xn, wg_ref[0], preferred_element_type=F32)
        u = jnp.dot(xn, wu_ref[0], preferred_element_type=F32)
        a = (g * _sigmoid(g) * u).astype(BF16)
        acc_ref[...] += jnp.dot(a, wd_ref[0], preferred_element_type=F32)

        @pl.when(j == pl.num_programs(1) - 1)
        def _():
            o_ref[...] = _pack_bf16_pairs(acc_ref[...])

    @pl.when((i >= nv_ref[0]) & (j == pl.num_programs(1) - 1))
    def _():
        o_ref[...] = jnp.zeros(o_ref.shape, o_ref.dtype)


def expert_ffn(xs, tile_expert, n_valid, wg, wu, wd, tm, tf):
    rows, dw = xs.shape
    ne, d, f = wg.shape
    nf = f // tf
    row_blk = lambda i, j, te, nv: (jnp.maximum(jnp.minimum(i, nv[0] - 1), 0), 0)
    out_blk = lambda i, j, te, nv: (i, 0)
    col_j = lambda i, j, nv: jnp.where(i < nv[0], j, nf - 1)
    return pl.pallas_call(
        _expert_ffn_kernel,
        grid_spec=pltpu.PrefetchScalarGridSpec(
            num_scalar_prefetch=2,
            grid=(rows // tm, nf),
            in_specs=[
                pl.BlockSpec((tm, dw), row_blk),
                pl.BlockSpec((1, d, tf), lambda i, j, te, nv: (te[i], 0, col_j(i, j, nv))),
                pl.BlockSpec((1, d, tf), lambda i, j, te, nv: (te[i], 0, col_j(i, j, nv))),
                pl.BlockSpec((1, tf, d), lambda i, j, te, nv: (te[i], col_j(i, j, nv), 0)),
            ],
            out_specs=pl.BlockSpec((tm, dw), out_blk),
            scratch_shapes=[pltpu.VMEM((tm, d), BF16), pltpu.VMEM((tm, d), F32)],
        ),
        out_shape=jax.ShapeDtypeStruct((rows, dw), U32),
        compiler_params=_params(("arbitrary", "arbitrary"), 48),
        name="expert_ffn",
    )(tile_expert, n_valid, xs, wg, wu, wd)


def _combine_ple_kernel(h_ref, top_ref, p_ref, nw_ref, wg_ref, wp_ref, pos_hbm, ys_hbm, o_ref,
                        ybuf, pos_smem, psem, sem, *, tb):
    i = pl.program_id(0)
    nsteps = pl.num_programs(0)
    slot = i % 2
    pos_copy = functools.partial(_pos_copy, pos_hbm, pos_smem, psem, n=TOP_K * tb)

    def row_copy(s, r, k):
        return pltpu.make_async_copy(ys_hbm.at[pl.ds(pos_smem[s, k * tb + r], 1)], ybuf.at[s, k, pl.ds(r, 1)],
                                     sem.at[s])

    def gather_start(s):
        def issue(r, _):
            for k in range(TOP_K):
                row_copy(s, r, k).start(priority=k)
            return 0
        lax.fori_loop(0, tb, issue, 0, unroll=8)

    def gather_wait(s):
        def drain(r, _):
            for k in range(TOP_K):
                row_copy(s, r, k).wait()
            return 0
        lax.fori_loop(0, tb, drain, 0, unroll=8)

    @pl.when(i == 0)
    def _():
        pos_copy(0, 0).start()
        pos_copy(0, 0).wait()
        gather_start(0)

        @pl.when(nsteps > 1)
        def _():
            pos_copy(1, 1).start()

    @pl.when(i + 1 < nsteps)
    def _():
        pos_copy(i + 1, 1 - slot).wait()
        gather_start(1 - slot)

    gather_wait(slot)

    @pl.when(i + 2 < nsteps)
    def _():
        pos_copy(i + 2, slot).start()

    top = top_ref[...]
    x = h_ref[...]
    for k in range(TOP_K):
        x = x + top[:, TOP_K + k:TOP_K + k + 1] * _unpack_bf16_pairs(ybuf[slot, k])
    xn = (x * _rms_scale(x) * nw_ref[...]).astype(BF16)
    gate = _sigmoid(jnp.dot(xn, wg_ref[...], preferred_element_type=F32))
    proj = jnp.dot(p_ref[...].astype(BF16), wp_ref[...], preferred_element_type=F32)
    o_ref[...] = x + proj * gate


def moe_combine_ple(h, top, pos_flat, ys, p, nw, wg, wp, tb):
    t, d = h.shape
    dw = ys.shape[1]
    pd = p.shape[1]
    resident = lambda shape: pl.BlockSpec(shape, lambda i: (0, 0), pipeline_mode=pl.Buffered(1))
    return pl.pallas_call(
        functools.partial(_combine_ple_kernel, tb=tb),
        grid=(t // tb,),
        in_specs=[
            pl.BlockSpec((tb, d), lambda i: (i, 0)),
            pl.BlockSpec((tb, LANES), lambda i: (i, 0)),
            pl.BlockSpec((tb, pd), lambda i: (i, 0)),
            resident((1, d)), resident((d, d)), resident((pd, d)),
            pl.BlockSpec(memory_space=pl.ANY),
            pl.BlockSpec(memory_space=pl.ANY),
        ],
        out_specs=pl.BlockSpec((tb, d), lambda i: (i, 0)),
        out_shape=jax.ShapeDtypeStruct((t, d), F32),
        scratch_shapes=[pltpu.VMEM((2, TOP_K, tb, dw), U32), pltpu.SMEM((2, TOP_K * tb), I32),
                        pltpu.SemaphoreType.DMA((2,)), pltpu.SemaphoreType.DMA((2,))],
        compiler_params=_params(("arbitrary",), 40),
        name="moe_combine_ple",
    )(h, top, p, nw.reshape(1, d), wg, wp, pos_flat, ys)


def moe_ple(h, nw, w_router, wg, wu, wd, p, ple_nw, ple_wg, ple_wp, tm_route, tm_expert, tb, tf):
    t, d = h.shape
    xpk, topt, sel, top = router(h, nw, _pad_cols(w_router, LANES), tm_route)
    rank, cnt = moe_rank(sel, tb)
    counts = cnt[:, 0].astype(I32)
    padded = ((counts + tm_expert - 1) // tm_expert) * tm_expert
    ends = jnp.cumsum(padded)
    n_tiles = (TOP_K * t) // tm_expert + N_EXPERTS
    n_valid = (ends[-1] // tm_expert).astype(I32)
    tile_start = jnp.arange(n_tiles, dtype=I32) * tm_expert
    tile_expert = jnp.minimum(jnp.searchsorted(ends, jnp.minimum(tile_start, ends[-1] - 1), side="right"),
                              N_EXPERTS - 1).astype(I32)
    off = jnp.broadcast_to((ends - padded).astype(F32)[:, None], (N_EXPERTS, LANES))
    pos_flat = moe_pos(topt, rank, off, tb).reshape(-1)
    xs = moe_dispatch(xpk, pos_flat, n_tiles * tm_expert, tb)
    ys = expert_ffn(xs, tile_expert, n_valid.reshape(1), wg, wu, wd, tm_expert, tf)
    return moe_combine_ple(h, top, pos_flat, ys, p, ple_nw, ple_wg, ple_wp, tb)


def _pad_cols(w, n):
    return jnp.pad(w, ((0, 0), (0, n - w.shape[1])))


def _tile3(v, n):
    return jnp.pad(jnp.tile(v, 3), (0, LANES - 3 * n)).reshape(1, LANES)


def kernel(x, p, ssm_norm_w, ssm_w_in, ssm_conv_w, ssm_conv_b, ssm_dt_bias, ssm_a_log, ssm_d, ssm_gn_w, ssm_w_out, kv_norm_w, w_kv, b_f, k_norm_w, att_norm_w, att_w_q, q_norm_w, att_w_o, ffn_norm_w, ffn_w_gate, ffn_w_up, ffn_w_down, moe_norm_w, moe_w_router, moe_w_gate, moe_w_up, moe_w_down, ple_norm_w, ple_w_gate, ple_w_proj):
    b, l, d = x.shape
    t = b * l
    n_a = ssm_norm_w.shape[0]
    depth = ple_norm_w.shape[0]
    d_inner = ssm_gn_w.shape[1]
    n_heads = ssm_dt_bias.shape[1]
    cdim = ssm_conv_w.shape[2]
    att_w = att_w_q.shape[2]
    n_att_heads = att_w // ATT_HEADDIM
    tm = min(1024, t)
    tq = min(256, l)

    h = x.reshape(t, d)
    r = jnp.arange(LANES)[:, None]
    c = jnp.arange(d_inner)[None, :]
    expand = ((r < 3 * n_heads) & ((r % n_heads) == (c // SSM_HEADDIM))).astype(BF16)

    kv = ext_q = ext_k = None
    for i in range(depth):
        if i < n_a:
            w_in = ssm_w_in[i].astype(BF16)
            w_dt = w_in[:, d_inner + cdim:]
            w_dt3 = _pad_cols(jnp.concatenate([w_dt, w_dt, w_dt], axis=1), LANES)
            z, xbc, dt_raw = norm_matmul(h, ssm_norm_w[i], [w_in[:, :d_inner], w_in[:, d_inner:d_inner + cdim], w_dt3],
                                         [BF16, BF16, F32], tm // 2)
            y = ssd(z.reshape(b, l, d_inner), xbc.reshape(b, l, cdim), dt_raw.reshape(b, l, LANES),
                    ssm_conv_w[i], ssm_conv_b[i].reshape(1, cdim), _tile3(ssm_dt_bias[i], n_heads),
                    _tile3(ssm_a_log[i], n_heads), jnp.repeat(ssm_d[i], SSM_HEADDIM).reshape(1, d_inner),
                    ssm_gn_w[i].reshape(1, d_inner), expand)
            h = matmul_residual(h, y.reshape(t, d_inner), ssm_w_out[i].astype(BF16), tm)
        else:
            j = i - n_a
            if i == n_a:
                wkv = w_kv.astype(BF16)
                w_f = wkv[:, 2 * att_w:]
                w_f3 = _pad_cols(jnp.concatenate([w_f, w_f, w_f], axis=1), LANES)
                kv, f = norm_matmul(h, kv_norm_w, [wkv[:, :2 * att_w], w_f3], [BF16, F32], tm)
                kv = kv.reshape(b, l, 2 * att_w)
                ext_q, ext_k = forget_ext(f.reshape(b, l, LANES), _tile3(b_f, n_att_heads), n_att_heads)
            qp, = norm_matmul(h, att_norm_w[j], [att_w_q[j].astype(BF16)], [BF16], tm)
            qp = qp.reshape(b, l, att_w)
            qnw2 = jnp.tile(q_norm_w[j], 2).reshape(1, LANES)
            knw2 = jnp.tile(k_norm_w, 2).reshape(1, LANES)
            o = fox_attention(qp, kv, ext_q, ext_k, qnw2, knw2, att_w, tq)
            h = matmul_residual(h, o.reshape(t, att_w), att_w_o[j].astype(BF16), tm)
        p_i = p[i].reshape(t, p.shape[-1])
        ple_wg, ple_wp = ple_w_gate[i].astype(BF16), ple_w_proj[i].astype(BF16)
        if i % 2 == 0:
            dd = i // 2
            h = ffn(h, ffn_norm_w[dd], ffn_w_gate[dd].astype(BF16), ffn_w_up[dd].astype(BF16),
                    ffn_w_down[dd].astype(BF16), tm, 512)
            h = ple(h, p_i, ple_norm_w[i], ple_wg, ple_wp, tm)
        else:
            m = i // 2
            h = moe_ple(h, moe_norm_w[m], moe_w_router[m], moe_w_gate[m].astype(BF16), moe_w_up[m].astype(BF16),
                        moe_w_down[m].astype(BF16), p_i, ple_norm_w[i], ple_wg, ple_wp, tm, tm, min(512, t), 512)
    return h.reshape(b, l, d)
```

```python
import functools
import math

import numpy as np
import jax
import jax.numpy as jnp
from jax import lax
from jax.experimental import pallas as pl
from jax.experimental.pallas import tpu as pltpu

F32 = jnp.float32
BF16 = jnp.bfloat16
I32 = jnp.int32
U32 = jnp.uint32
EPS = 1e-6
HIGHEST = lax.Precision.HIGHEST
LOG2E = math.log2(math.e)

LANES = 128
SUBLANES = 8
MIB = 1024 * 1024

SSM_HEADDIM = 64
SSM_GROUPS = 4
SSM_STATE = 128
CONV_WIDTH = 4
CHUNK = 128
ATT_HEADDIM = 64
N_EXPERTS = 8
TOP_K = 2


def _params(sem, vmem_mib):
    return pltpu.CompilerParams(dimension_semantics=sem, vmem_limit_bytes=vmem_mib * MIB)


def _rms_scale(x):
    return lax.rsqrt(jnp.mean(x * x, axis=-1, keepdims=True) + EPS)


def _sigmoid(x):
    return 1.0 / (1.0 + jnp.exp2(x * (-LOG2E)))


def _softplus(x):
    return jnp.maximum(x, 0.0) + jnp.log(1.0 + jnp.exp(-jnp.abs(x)))


def _pack_bf16_pairs(x):
    n = x.shape[1] // 2
    bits = lax.bitcast_convert_type(x.astype(BF16).astype(F32), U32)
    return lax.shift_right_logical(bits[:, :n], jnp.uint32(16)) | (bits[:, n:] & jnp.uint32(0xFFFF0000))


def _unpack_bf16_pairs(w):
    lo = lax.bitcast_convert_type(lax.shift_left(w, jnp.uint32(16)), F32)
    hi = lax.bitcast_convert_type(w & jnp.uint32(0xFFFF0000), F32)
    return jnp.concatenate([lo, hi], axis=1)


def _store_row_tiles(ref, w):
    for s in range(ref.shape[-2]):
        ref[:, s, :] = w[:, s * LANES:(s + 1) * LANES]


def _load_row_tiles(ref):
    return jnp.concatenate([ref[:, s, :] for s in range(ref.shape[-2])], axis=1)


def _norm_matmul_kernel(h_ref, nw_ref, *refs):
    n = len(refs) // 2
    x = h_ref[...]
    xn = (x * _rms_scale(x) * nw_ref[...]).astype(BF16)
    for w_ref, o_ref in zip(refs[:n], refs[n:]):
        o_ref[...] = jnp.dot(xn, w_ref[...], preferred_element_type=F32).astype(o_ref.dtype)


def norm_matmul(h, nw, ws, out_dtypes, tm):
    t, d = h.shape
    resident = lambda shape: pl.BlockSpec(shape, lambda i: (0, 0), pipeline_mode=pl.Buffered(1))
    return pl.pallas_call(
        _norm_matmul_kernel,
        grid=(t // tm,),
        in_specs=[pl.BlockSpec((tm, d), lambda i: (i, 0)), resident((1, d))] + [resident(w.shape) for w in ws],
        out_specs=[pl.BlockSpec((tm, w.shape[1]), lambda i: (i, 0)) for w in ws],
        out_shape=[jax.ShapeDtypeStruct((t, w.shape[1]), dt) for w, dt in zip(ws, out_dtypes)],
        compiler_params=_params(("parallel",), 48),
        name="norm_matmul",
    )(h, nw.reshape(1, d), *ws)


def _matmul_residual_kernel(h_ref, y_ref, w_ref, o_ref):
    o_ref[...] = h_ref[...] + jnp.dot(y_ref[...], w_ref[...], preferred_element_type=F32)


def matmul_residual(h, y, w, tm):
    t, n = h.shape
    k = y.shape[1]
    return pl.pallas_call(
        _matmul_residual_kernel,
        grid=(t // tm,),
        in_specs=[
            pl.BlockSpec((tm, n), lambda i: (i, 0)),
            pl.BlockSpec((tm, k), lambda i: (i, 0)),
            pl.BlockSpec((k, n), lambda i: (0, 0)),
        ],
        out_specs=pl.BlockSpec((tm, n), lambda i: (i, 0)),
        out_shape=jax.ShapeDtypeStruct((t, n), F32),
        compiler_params=_params(("parallel",), 48),
        name="matmul_residual",
    )(h, y, w)


def _ple_update(x, p, nw, wg, wp):
    xn = (x * _rms_scale(x) * nw).astype(BF16)
    gate = _sigmoid(jnp.dot(xn, wg, preferred_element_type=F32))
    proj = jnp.dot(p.astype(BF16), wp, preferred_element_type=F32)
    return x + proj * gate


def _ffn_ple_kernel(h_ref, nw_ref, wg_ref, wu_ref, wd_ref, p_ref, pnw_ref, pwg_ref, pwp_ref, o_ref, xn_ref, acc_ref):
    j = pl.program_id(1)

    @pl.when(j == 0)
    def _():
        x = h_ref[...]
        xn_ref[...] = (x * _rms_scale(x) * nw_ref[...]).astype(xn_ref.dtype)
        acc_ref[...] = x

    xn = xn_ref[...]
    g = jnp.dot(xn, wg_ref[...], preferred_element_type=F32)
    u = jnp.dot(xn, wu_ref[...], preferred_element_type=F32)
    a = (g * _sigmoid(g) * u).astype(BF16)
    acc_ref[...] += jnp.dot(a, wd_ref[...], preferred_element_type=F32)

    @pl.when(j == pl.num_programs(1) - 1)
    def _():
        o_ref[...] = _ple_update(acc_ref[...], p_ref[...], pnw_ref[...], pwg_ref[...], pwp_ref[...])


def ffn_ple(h, nw, wg, wu, wd, p, ple_nw, ple_wg, ple_wp, tm, tf):
    t, d = h.shape
    f = wg.shape[1]
    pd = p.shape[1]
    resident = lambda shape: pl.BlockSpec(shape, lambda i, j: (0, 0), pipeline_mode=pl.Buffered(1))
    return pl.pallas_call(
        _ffn_ple_kernel,
        grid=(t // tm, f // tf),
        in_specs=[
            pl.BlockSpec((tm, d), lambda i, j: (i, 0)),
            resident((1, d)),
            pl.BlockSpec((d, tf), lambda i, j: (0, j)),
            pl.BlockSpec((d, tf), lambda i, j: (0, j)),
            pl.BlockSpec((tf, d), lambda i, j: (j, 0)),
            pl.BlockSpec((tm, pd), lambda i, j: (i, 0)),
            resident((1, d)), resident((d, d)), resident((pd, d)),
        ],
        out_specs=pl.BlockSpec((tm, d), lambda i, j: (i, 0)),
        out_shape=jax.ShapeDtypeStruct((t, d), F32),
        scratch_shapes=[pltpu.VMEM((tm, d), BF16), pltpu.VMEM((tm, d), F32)],
        compiler_params=_params(("parallel", "arbitrary"), 52),
        name="ffn_ple",
    )(h, nw.reshape(1, d), wg, wu, wd, p, ple_nw.reshape(1, d), ple_wg, ple_wp)


def _split3(v, n):
    v1 = v.astype(BF16)
    r1 = v - v1.astype(F32)
    v2 = r1.astype(BF16)
    v3 = (r1 - v2.astype(F32)).astype(BF16)
    lane = lax.broadcasted_iota(I32, v.shape, 1)
    return jnp.where(lane < n, v1, jnp.where(lane < 2 * n, v2, v3))


def _ssd_kernel(z_ref, xbc_ref, dt_ref, cw_ref, cb_ref, dtb_ref, alog_ref, dskip_ref, gnw_ref, exp_ref,
                o_ref, cbuf, xact, state, acs_t, y_scr):
    d_inner = z_ref.shape[-1]
    gw = d_inner // SSM_GROUPS
    n_heads = d_inner // SSM_HEADDIM
    q = CHUNK

    tail = cbuf.shape[0] - q

    @pl.when(pl.program_id(1) == 0)
    def _():
        cbuf[0:tail, :] = jnp.zeros((tail, cbuf.shape[1]), BF16)
        state[...] = jnp.zeros(state.shape, F32)

    cbuf[tail:, :] = xbc_ref[0]
    cdim = cbuf.shape[1]
    nshift = CONV_WIDTH - 1
    srow = lax.broadcasted_iota(I32, (nshift * q, tail + q), 0)
    scol = lax.broadcasted_iota(I32, (nshift * q, tail + q), 1)
    shift = jnp.where(scol == srow - (srow // q) * (q - 1) + (tail - nshift), 1.0, 0.0).astype(BF16)
    cw_lanes = 2 * LANES
    for j in range(cdim // cw_lanes):
        sl = slice(j * cw_lanes, (j + 1) * cw_lanes)
        shifted = jnp.dot(shift, cbuf[:, sl], preferred_element_type=F32)
        acc = cb_ref[:, sl] + cbuf[tail:, sl].astype(F32) * cw_ref[nshift:nshift + 1, sl]
        for w in range(nshift):
            acc = acc + shifted[w * q:(w + 1) * q] * cw_ref[w:w + 1, sl]
        xact[:, sl] = acc * _sigmoid(acc)
    cbuf[0:tail, :] = cbuf[q:q + tail, :]

    dt = _softplus(dt_ref[0] + dtb_ref[...])
    a = dt * (-LOG2E * jnp.exp(alog_ref[...]))
    row = lax.broadcasted_iota(I32, (q, q), 0)
    col = lax.broadcasted_iota(I32, (q, q), 1)
    causal = row >= col
    tril = jnp.where(causal, 1.0, 0.0).astype(F32)
    acs = jnp.dot(tril, a, preferred_element_type=F32, precision=HIGHEST)
    acs_t[...] = acs.T
    a_last = acs[q - 1:q, :]
    dt3 = _split3(dt, n_heads)
    dec3 = _split3(jnp.exp2(a_last - acs), n_heads)
    e3 = _split3(jnp.exp2(acs), n_heads)

    for g in range(SSM_GROUPS):
        gl = slice(g * gw, (g + 1) * gw)
        ex = exp_ref[:, gl]
        xs = xact[:, gl]
        b_g = xact[:, d_inner + g * SSM_STATE:d_inner + (g + 1) * SSM_STATE].astype(BF16)
        c_g = xact[:, d_inner + (SSM_GROUPS + g) * SSM_STATE:d_inner + (SSM_GROUPS + g + 1) * SSM_STATE].astype(BF16)
        dt_e = jnp.dot(dt3, ex, preferred_element_type=F32)
        dec_e = jnp.dot(dec3, ex, preferred_element_type=F32)
        e_e = jnp.dot(e3, ex, preferred_element_type=F32)
        xc = xs * dt_e
        xc_b = xc.astype(BF16)
        xdec_b = (xc * dec_e).astype(BF16)
        scores = lax.dot_general(c_g, b_g, (((1,), (1,)), ((), ())), preferred_element_type=F32)
        st = state[g]
        y = jnp.dot(c_g, st.astype(BF16), preferred_element_type=F32) * e_e
        y = y + xs * dskip_ref[:, gl]
        lane = lax.broadcasted_iota(I32, (q, LANES), 1)
        ydiag = []
        for pr in range(gw // LANES):
            ms = []
            for hh in range(2):
                h = g * (gw // SSM_HEADDIM) + 2 * pr + hh
                seg = acs[:, h:h + 1] - acs_t[h:h + 1, :]
                ms.append(scores * jnp.exp2(jnp.where(causal, seg, -jnp.inf)))
            lhs = jnp.concatenate(ms, axis=1).astype(BF16)
            xp = xc_b[:, pr * LANES:(pr + 1) * LANES]
            zero = jnp.zeros_like(xp)
            rhs = jnp.concatenate([jnp.where(lane < SSM_HEADDIM, xp, zero),
                                   jnp.where(lane >= SSM_HEADDIM, xp, zero)], axis=0)
            ydiag.append(jnp.dot(lhs, rhs, preferred_element_type=F32))
        y_scr[:, gl] = y + jnp.concatenate(ydiag, axis=1)
        upd = lax.dot_general(b_g, xdec_b, (((0,), (0,)), ((), ())), preferred_element_type=F32)
        state[g] = st * e_e[q - 1:q, :] + upd

    zz = z_ref[0].astype(F32)
    y = y_scr[...] * (zz * _sigmoid(zz))
    o_ref[0] = (y * _rms_scale(y) * gnw_ref[...]).astype(o_ref.dtype)


def ssd(z, xbc, dt_raw, conv_w, conv_b, dt_bias3, a_log3, dskip_e, gn_w, expand):
    b, l, d_inner = z.shape
    cdim = xbc.shape[-1]
    gw = d_inner // SSM_GROUPS
    nc = l // CHUNK
    full = lambda shape: pl.BlockSpec(shape, lambda i, c: (0,) * len(shape))
    return pl.pallas_call(
        _ssd_kernel,
        grid=(b, nc),
        in_specs=[
            pl.BlockSpec((1, CHUNK, d_inner), lambda i, c: (i, c, 0)),
            pl.BlockSpec((1, CHUNK, cdim), lambda i, c: (i, c, 0)),
            pl.BlockSpec((1, CHUNK, LANES), lambda i, c: (i, c, 0)),
            full((CONV_WIDTH, cdim)), full((1, cdim)), full((1, LANES)), full((1, LANES)),
            full((1, d_inner)), full((1, d_inner)), full((LANES, d_inner)),
        ],
        out_specs=pl.BlockSpec((1, CHUNK, d_inner), lambda i, c: (i, c, 0)),
        out_shape=jax.ShapeDtypeStruct((b, l, d_inner), BF16),
        scratch_shapes=[
            pltpu.VMEM((CHUNK + 16, cdim), BF16),
            pltpu.VMEM((CHUNK, cdim), F32),
            pltpu.VMEM((SSM_GROUPS, SSM_STATE, gw), F32),
            pltpu.VMEM((LANES, CHUNK), F32),
            pltpu.VMEM((CHUNK, d_inner), F32),
        ],
        compiler_params=_params(("parallel", "arbitrary"), 40),
        name="ssd",
    )(z, xbc, dt_raw, conv_w, conv_b, dt_bias3, a_log3, dskip_e, gn_w, expand)


def _cum_kernel(f_ref, bf_ref, pq_ref, pk_ref, eq_ref, ek_ref, *, blk, n_heads):
    l = f_ref.shape[1]
    row = lax.broadcasted_iota(I32, (blk, blk), 0)
    col = lax.broadcasted_iota(I32, (blk, blk), 1)
    tril = jnp.where(row >= col, 1.0, 0.0).astype(F32)
    lane = lax.broadcasted_iota(I32, (blk, LANES), 1)
    carry = jnp.zeros((1, LANES), F32)
    for i in range(l // blk):
        rs = slice(i * blk, (i + 1) * blk)
        x = f_ref[0, rs, :] + bf_ref[...]
        lf = -_softplus(-x)
        cs = jnp.dot(tril, lf, preferred_element_type=F32, precision=HIGHEST) + carry
        carry = cs[blk - 1:blk, :]
        lhs = jnp.where(lane == 3 * n_heads, jnp.ones((), BF16), _split3(cs * LOG2E, n_heads))
        eq_ref[0, rs, :] = jnp.dot(lhs, pq_ref[...], preferred_element_type=F32).astype(BF16)
        ek_ref[0, rs, :] = jnp.dot(lhs, pk_ref[...], preferred_element_type=F32).astype(BF16)


def _ext_placement(n_heads):
    w = n_heads * ATT_HEADDIM
    pq = np.zeros((LANES, w), np.float32)
    pk = np.zeros((LANES, w), np.float32)
    one = 3 * n_heads
    for h in range(n_heads):
        base = (h // 2) * LANES + (ATT_HEADDIM if h % 2 == 0 else 0)
        for piece in range(3):
            pq[piece * n_heads + h, base + piece] = 1.0
            pq[one, base + 3 + piece] = 1.0
            pk[one, base + piece] = 1.0
            pk[piece * n_heads + h, base + 3 + piece] = -1.0
    return jnp.asarray(pq, BF16), jnp.asarray(pk, BF16)


def forget_ext(f, b_f3, n_heads):
    b, l, _ = f.shape
    w = n_heads * ATT_HEADDIM
    pq, pk = _ext_placement(n_heads)
    out = jax.ShapeDtypeStruct((b, l, w), BF16)
    return pl.pallas_call(
        functools.partial(_cum_kernel, blk=256, n_heads=n_heads),
        grid=(b,),
        in_specs=[
            pl.BlockSpec((1, l, LANES), lambda i: (i, 0, 0)),
            pl.BlockSpec((1, LANES), lambda i: (0, 0)),
            pl.BlockSpec((LANES, w), lambda i: (0, 0)),
            pl.BlockSpec((LANES, w), lambda i: (0, 0)),
        ],
        out_specs=[pl.BlockSpec((1, l, w), lambda i: (i, 0, 0)), pl.BlockSpec((1, l, w), lambda i: (i, 0, 0))],
        out_shape=[out, out],
        compiler_params=_params(("parallel",), 40),
        name="forget_ext",
    )(f, b_f3, pq, pk)


def _head_rms(x, lane_lo):
    sq = x * x
    s_lo = jnp.sum(jnp.where(lane_lo, sq, 0.0), axis=-1, keepdims=True)
    s_hi = jnp.sum(jnp.where(lane_lo, 0.0, sq), axis=-1, keepdims=True)
    inv = 1.0 / ATT_HEADDIM
    return jnp.where(lane_lo, lax.rsqrt(s_lo * inv + EPS), lax.rsqrt(s_hi * inv + EPS))


def _attn_kernel(q_ref, k_ref, v_ref, eq_ref, ek_ref, qnw_ref, knw_ref, o_ref, kaug, vaugt, s_scr, p_scr, *, tq):
    l = q_ref.shape[1]
    nblk = l // tq
    hd = ATT_HEADDIM
    lane_lo = lax.broadcasted_iota(I32, (tq, LANES), 1) < hd
    sub_lo = lax.broadcasted_iota(I32, (LANES, tq), 0) < hd
    row = lax.broadcasted_iota(I32, (tq, tq), 0)
    col = lax.broadcasted_iota(I32, (tq, tq), 1)
    visible = row <= col
    eye_q = jnp.where(row == col, 1.0, 0.0).astype(BF16)
    eye_l = eye_q[:LANES, :LANES]
    nt = (((1,), (1,)), ((), ()))
    qscale = qnw_ref[...] * (hd ** -0.5 * LOG2E)

    for i in range(nblk):
        rs = slice(i * tq, (i + 1) * tq)
        kk = k_ref[0, rs, :].astype(F32)
        kn = (kk * _head_rms(kk, lane_lo) * knw_ref[...]).astype(BF16)
        ek = ek_ref[0, rs, :]
        kaug[0, rs, :] = jnp.where(lane_lo, kn, ek)
        kaug[1, rs, :] = jnp.where(lane_lo, ek, kn)
        vt = lax.dot_general(eye_l, v_ref[0, rs, :], nt, preferred_element_type=F32).astype(BF16)
        one = jnp.ones_like(vt)
        vaugt[0, :, rs] = jnp.where(sub_lo, vt, one)
        vaugt[1, :, rs] = jnp.where(sub_lo, one, vt)

    def scores(qi):
        qs = slice(qi * tq, (qi + 1) * tq)
        nk = (qi + 1) * tq
        qq = q_ref[0, qs, :].astype(F32)
        qn = (qq * _head_rms(qq, lane_lo) * qscale).astype(BF16)
        eq = eq_ref[0, qs, :]
        for hh in range(2):
            qa = jnp.where(lane_lo, qn, eq) if hh == 0 else jnp.where(lane_lo, eq, qn)
            qt = lax.dot_general(eye_l, qa, nt, preferred_element_type=F32).astype(BF16)
            s = jnp.dot(kaug[hh, :nk, :], qt, preferred_element_type=F32)
            s_scr[qi % 2, hh, :nk, :] = s

    def softmax_pv(qi):
        qs = slice(qi * tq, (qi + 1) * tq)
        nk = (qi + 1) * tq
        par = qi % 2
        accs = []
        for hh in range(2):
            s_scr[par, hh, nk - tq:nk, :] = jnp.where(visible, s_scr[par, hh, nk - tq:nk, :], -jnp.inf)
            m = jnp.max(s_scr[par, hh, :nk, :], axis=0, keepdims=True)
            p_scr[par, hh, :nk, :] = jnp.exp2(s_scr[par, hh, :nk, :] - m).astype(BF16)
            accs.append(jnp.dot(vaugt[hh, :, :nk], p_scr[par, hh, :nk, :], preferred_element_type=F32))
        a0, a1 = accs
        o_t = jnp.concatenate([a0[:hd] / a0[hd:], a1[hd:] / a1[:hd]], axis=0).astype(BF16)
        o_ref[0, qs, :] = lax.dot_general(eye_q, o_t, nt, preferred_element_type=F32).astype(o_ref.dtype)

    scores(0)
    for qi in range(nblk):
        if qi + 1 < nblk:
            scores(qi + 1)
        softmax_pv(qi)


def fox_attention(qp, kv, ext_q, ext_k, qnw2, knw2, att_w, tq):
    b, l, _ = qp.shape
    npair = att_w // LANES
    nblk = l // tq
    blk = lambda off: pl.BlockSpec((1, l, LANES), lambda i, j: (i, 0, j + off))
    return pl.pallas_call(
        functools.partial(_attn_kernel, tq=tq),
        grid=(b, npair),
        in_specs=[
            blk(0), blk(0), blk(npair), blk(0), blk(0),
            pl.BlockSpec((1, LANES), lambda i, j: (0, 0)),
            pl.BlockSpec((1, LANES), lambda i, j: (0, 0)),
        ],
        out_specs=blk(0),
        out_shape=jax.ShapeDtypeStruct((b, l, att_w), BF16),
        scratch_shapes=[
            pltpu.VMEM((2, l, LANES), BF16),
            pltpu.VMEM((2, LANES, l), BF16),
            pltpu.VMEM((2, 2, l, tq), F32),
            pltpu.VMEM((2, 2, l, tq), BF16),
        ],
        compiler_params=_params(("parallel", "parallel"), 40),
        name="fox_attention",
    )(qp, kv, kv, ext_q, ext_k, qnw2, knw2)


def _router_kernel(h_ref, nw_ref, wr_ref, xpk_ref, topt_ref, sel_ref, top_ref):
    x = h_ref[...]
    u = x * _rms_scale(x) * nw_ref[...]
    _store_row_tiles(xpk_ref, _pack_bf16_pairs(u))
    logits = jnp.dot(u, wr_ref[...], preferred_element_type=F32, precision=HIGHEST)
    lt = logits.T[:N_EXPERTS, :]
    sub = lax.broadcasted_iota(I32, lt.shape, 0)
    neg = jnp.float32(-jnp.inf)
    m1 = jnp.max(lt, axis=0, keepdims=True)
    i1 = jnp.min(jnp.where(lt == m1, sub, N_EXPERTS), axis=0, keepdims=True)
    rest = jnp.where(sub == i1, neg, lt)
    m2 = jnp.max(rest, axis=0, keepdims=True)
    i2 = jnp.min(jnp.where(rest == m2, sub, N_EXPERTS), axis=0, keepdims=True)
    e2 = jnp.exp(m2 - m1)
    g1 = 1.0 / (1.0 + e2)
    g2 = e2 / (1.0 + e2)
    sel_ref[...] = jnp.where((sub == i1) | (sub == i2), 1.0, 0.0).astype(F32)
    topt = jnp.where(sub == 0, i1.astype(F32),
                     jnp.where(sub == 1, i2.astype(F32), jnp.where(sub == 2, g1, jnp.where(sub == 3, g2, 0.0))))
    topt_ref[...] = topt
    pad = jnp.zeros((LANES - N_EXPERTS, topt.shape[1]), F32)
    top_ref[...] = jnp.concatenate([topt, pad], axis=0).T


def router(h, nw, wr_pad, tm):
    t, d = h.shape
    return pl.pallas_call(
        _router_kernel,
        grid=(t // tm,),
        in_specs=[
            pl.BlockSpec((tm, d), lambda i: (i, 0)),
            pl.BlockSpec((1, d), lambda i: (0, 0)),
            pl.BlockSpec((d, LANES), lambda i: (0, 0)),
        ],
        out_specs=[
            pl.BlockSpec((tm, d // 2 // LANES, LANES), lambda i: (i, 0, 0)),
            pl.BlockSpec((N_EXPERTS, tm), lambda i: (0, i)),
            pl.BlockSpec((N_EXPERTS, tm), lambda i: (0, i)),
            pl.BlockSpec((tm, LANES), lambda i: (i, 0)),
        ],
        out_shape=[
            jax.ShapeDtypeStruct((t, d // 2 // LANES, LANES), U32),
            jax.ShapeDtypeStruct((N_EXPERTS, t), F32),
            jax.ShapeDtypeStruct((N_EXPERTS, t), F32),
            jax.ShapeDtypeStruct((t, LANES), F32),
        ],
        compiler_params=_params(("parallel",), 40),
        name="router",
    )(h, nw.reshape(1, d), wr_pad)


def _rank_kernel(sel_ref, rank_ref, cnt_ref, carry):
    @pl.when(pl.program_id(0) == 0)
    def _():
        carry[...] = jnp.zeros(carry.shape, F32)

    s = sel_ref[...]
    tb = s.shape[1]
    row = lax.broadcasted_iota(I32, (tb, tb), 0)
    col = lax.broadcasted_iota(I32, (tb, tb), 1)
    before = jnp.where(row < col, 1.0, 0.0).astype(BF16)
    c = carry[...]
    rank_ref[...] = jnp.dot(s.astype(BF16), before, preferred_element_type=F32) + c[:, :1]
    c = c + jnp.sum(s, axis=1, keepdims=True)
    carry[...] = c
    cnt_ref[...] = c


def moe_rank(sel, tb):
    ne, t = sel.shape
    return pl.pallas_call(
        _rank_kernel,
        grid=(t // tb,),
        in_specs=[pl.BlockSpec((ne, tb), lambda i: (0, i))],
        out_specs=[pl.BlockSpec((ne, tb), lambda i: (0, i)), pl.BlockSpec((ne, LANES), lambda i: (0, 0))],
        out_shape=[jax.ShapeDtypeStruct((ne, t), F32), jax.ShapeDtypeStruct((ne, LANES), F32)],
        scratch_shapes=[pltpu.VMEM((ne, LANES), F32)],
        compiler_params=_params(("arbitrary",), 32),
        name="moe_rank",
    )(sel)


def _pos_kernel(topt_ref, rank_ref, off_ref, pos_ref):
    posv = rank_ref[...] + off_ref[...][:, :1]
    sub = lax.broadcasted_iota(I32, posv.shape, 0).astype(F32)
    for k in range(TOP_K):
        ek = topt_ref[k:k + 1, :]
        pos_ref[0, k:k + 1, :] = jnp.sum(jnp.where(sub == ek, posv, 0.0), axis=0, keepdims=True).astype(I32)


def moe_pos(topt, rank, off, tb):
    ne, t = rank.shape
    return pl.pallas_call(
        _pos_kernel,
        grid=(t // tb,),
        in_specs=[
            pl.BlockSpec((ne, tb), lambda i: (0, i)),
            pl.BlockSpec((ne, tb), lambda i: (0, i)),
            pl.BlockSpec((ne, LANES), lambda i: (0, 0)),
        ],
        out_specs=pl.BlockSpec((1, TOP_K, tb), lambda i: (i, 0, 0)),
        out_shape=jax.ShapeDtypeStruct((t // tb, TOP_K, tb), I32),
        compiler_params=_params(("parallel",), 32),
        name="moe_pos",
    )(topt, rank, off)


def _pos_copy(pos_hbm, pos_smem, psem, tile, slot, n):
    return pltpu.make_async_copy(pos_hbm.at[pl.ds(pl.multiple_of(tile * n, n), n)],
                                 pos_smem.at[pl.ds(pl.multiple_of(slot * n, n), n)], psem.at[slot])


def _dispatch_kernel(x_ref, pos_hbm, init_hbm, xs_hbm, pos_smem, psem, sem, *, tb):
    del init_hbm
    i = pl.program_id(0)
    nsteps = pl.num_programs(0)
    slot = i % 2
    pos_copy = functools.partial(_pos_copy, pos_hbm, pos_smem, psem, n=TOP_K * tb)

    @pl.when(i == 0)
    def _():
        pos_copy(0, 0).start()

    @pl.when(i + 1 < nsteps)
    def _():
        pos_copy(i + 1, 1 - slot).start()

    pos_copy(i, slot).wait()

    def row_copy(r, k):
        return pltpu.make_async_copy(x_ref.at[r], xs_hbm.at[pos_smem[slot * (TOP_K * tb) + k * tb + r]], sem)

    def issue(r, _):
        for k in range(TOP_K):
            row_copy(r, k).start()
        return 0

    lax.fori_loop(0, tb, issue, 0, unroll=8)

    def drain(r, _):
        for k in range(TOP_K):
            row_copy(r, k).wait()
        return 0

    lax.fori_loop(0, tb, drain, 0, unroll=8)


def moe_dispatch(xpk, pos_flat, rows, tb):
    t = xpk.shape[0]
    slab = xpk.shape[1:]
    init = jnp.zeros((rows,) + slab, U32)
    return pl.pallas_call(
        functools.partial(_dispatch_kernel, tb=tb),
        grid=(t // tb,),
        in_specs=[
            pl.BlockSpec((tb,) + slab, lambda i: (i, 0, 0)),
            pl.BlockSpec(memory_space=pl.ANY),
            pl.BlockSpec(memory_space=pl.ANY),
        ],
        out_specs=pl.BlockSpec(memory_space=pl.ANY),
        out_shape=jax.ShapeDtypeStruct((rows,) + slab, U32),
        scratch_shapes=[pltpu.SMEM((2 * TOP_K * tb,), I32), pltpu.SemaphoreType.DMA((2,)), pltpu.SemaphoreType.DMA],
        input_output_aliases={2: 0},
        compiler_params=_params(("arbitrary",), 32),
        name="moe_dispatch",
    )(xpk, pos_flat, init)


def _expert_ffn_kernel(te_ref, nv_ref, x_ref, wg_ref, wu_ref, wd_ref, o_ref, xn_ref, acc_ref):
    del te_ref
    i = pl.program_id(0)
    j = pl.program_id(1)

    @pl.when(i < nv_ref[0])
    def _():
        @pl.when(j == 0)
        def _():
            xn_ref[...] = _unpack_bf16_pairs(_load_row_tiles(x_ref)).astype(BF16)
            acc_ref[...] = jnp.zeros(acc_ref.shape, F32)

        xn = xn_ref[...]
        g = jnp.dot(xn, wg_ref[0], preferred_element_type=F32)
        u = jnp.dot(xn, wu_ref[0], preferred_element_type=F32)
        a = (g * _sigmoid(g) * u).astype(BF16)
        acc_ref[...] += jnp.dot(a, wd_ref[0], preferred_element_type=F32)

        @pl.when(j == pl.num_programs(1) - 1)
        def _():
            _store_row_tiles(o_ref, _pack_bf16_pairs(acc_ref[...]))

    @pl.when((i >= nv_ref[0]) & (j == pl.num_programs(1) - 1))
    def _():
        o_ref[...] = jnp.zeros(o_ref.shape, o_ref.dtype)


def expert_ffn(xs, tile_expert, n_valid, wg, wu, wd, tm, tf):
    rows = xs.shape[0]
    slab = xs.shape[1:]
    ne, d, f = wg.shape
    nf = f // tf
    row_blk = lambda i, j, te, nv: (jnp.maximum(jnp.minimum(i, nv[0] - 1), 0), 0, 0)
    out_blk = lambda i, j, te, nv: (i, 0, 0)
    col_j = lambda i, j, nv: jnp.where(i < nv[0], j, nf - 1)
    return pl.pallas_call(
        _expert_ffn_kernel,
        grid_spec=pltpu.PrefetchScalarGridSpec(
            num_scalar_prefetch=2,
            grid=(rows // tm, nf),
            in_specs=[
                pl.BlockSpec((tm,) + slab, row_blk),
                pl.BlockSpec((1, d, tf), lambda i, j, te, nv: (te[i], 0, col_j(i, j, nv))),
                pl.BlockSpec((1, d, tf), lambda i, j, te, nv: (te[i], 0, col_j(i, j, nv))),
                pl.BlockSpec((1, tf, d), lambda i, j, te, nv: (te[i], col_j(i, j, nv), 0)),
            ],
            out_specs=pl.BlockSpec((tm,) + slab, out_blk),
            scratch_shapes=[pltpu.VMEM((tm, d), BF16), pltpu.VMEM((tm, d), F32)],
        ),
        out_shape=jax.ShapeDtypeStruct((rows,) + slab, U32),
        compiler_params=_params(("arbitrary", "arbitrary"), 48),
        name="expert_ffn",
    )(tile_expert, n_valid, xs, wg, wu, wd)


def _combine_ple_kernel(h_ref, top_ref, p_ref, nw_ref, wg_ref, wp_ref, pos_hbm, ys_hbm, o_ref,
                        ybuf, pos_smem, psem, sem, *, tb):
    i = pl.program_id(0)
    nsteps = pl.num_programs(0)
    slot = i % 2
    pos_copy = functools.partial(_pos_copy, pos_hbm, pos_smem, psem, n=TOP_K * tb)

    def row_copy(s, r, k):
        return pltpu.make_async_copy(ys_hbm.at[pos_smem[s * (TOP_K * tb) + k * tb + r]], ybuf.at[s, k, r], sem.at[s])

    def gather_start(s):
        def issue(r, _):
            for k in range(TOP_K):
                row_copy(s, r, k).start()
            return 0
        lax.fori_loop(0, tb, issue, 0, unroll=8)

    def gather_wait(s):
        def drain(r, _):
            for k in range(TOP_K):
                row_copy(s, r, k).wait()
            return 0
        lax.fori_loop(0, tb, drain, 0, unroll=8)

    @pl.when(i == 0)
    def _():
        pos_copy(0, 0).start()
        pos_copy(0, 0).wait()
        gather_start(0)

        @pl.when(nsteps > 1)
        def _():
            pos_copy(1, 1).start()

    @pl.when(i + 1 < nsteps)
    def _():
        pos_copy(i + 1, 1 - slot).wait()
        gather_start(1 - slot)

    gather_wait(slot)

    @pl.when(i + 2 < nsteps)
    def _():
        pos_copy(i + 2, slot).start()

    top = top_ref[...]
    x = h_ref[...]
    for k in range(TOP_K):
        x = x + top[:, TOP_K + k:TOP_K + k + 1] * _unpack_bf16_pairs(_load_row_tiles(ybuf.at[slot, k]))
    o_ref[...] = _ple_update(x, p_ref[...], nw_ref[...], wg_ref[...], wp_ref[...])


def moe_combine_ple(h, top, pos_flat, ys, p, nw, wg, wp, tb):
    t, d = h.shape
    slab = ys.shape[1:]
    pd = p.shape[1]
    resident = lambda shape: pl.BlockSpec(shape, lambda i: (0, 0), pipeline_mode=pl.Buffered(1))
    return pl.pallas_call(
        functools.partial(_combine_ple_kernel, tb=tb),
        grid=(t // tb,),
        in_specs=[
            pl.BlockSpec((tb, d), lambda i: (i, 0)),
            pl.BlockSpec((tb, LANES), lambda i: (i, 0)),
            pl.BlockSpec((tb, pd), lambda i: (i, 0)),
            resident((1, d)), resident((d, d)), resident((pd, d)),
            pl.BlockSpec(memory_space=pl.ANY),
            pl.BlockSpec(memory_space=pl.ANY),
        ],
        out_specs=pl.BlockSpec((tb, d), lambda i: (i, 0)),
        out_shape=jax.ShapeDtypeStruct((t, d), F32),
        scratch_shapes=[pltpu.VMEM((2, TOP_K, tb) + slab, U32), pltpu.SMEM((2 * TOP_K * tb,), I32),
                        pltpu.SemaphoreType.DMA((2,)), pltpu.SemaphoreType.DMA((2,))],
        compiler_params=_params(("arbitrary",), 40),
        name="moe_combine_ple",
    )(h, top, p, nw.reshape(1, d), wg, wp, pos_flat, ys)


def moe_ple(h, nw, w_router, wg, wu, wd, p, ple_nw, ple_wg, ple_wp, tm_route, tm_expert, tb, tf):
    t, d = h.shape
    xpk, topt, sel, top = router(h, nw, _pad_cols(w_router, LANES), tm_route)
    rank, cnt = moe_rank(sel, tb)
    counts = cnt[:, 0].astype(I32)
    padded = ((counts + tm_expert - 1) // tm_expert) * tm_expert
    ends = jnp.cumsum(padded)
    n_tiles = (TOP_K * t) // tm_expert + N_EXPERTS
    n_valid = (ends[-1] // tm_expert).astype(I32)
    tile_start = jnp.arange(n_tiles, dtype=I32) * tm_expert
    first_row = jnp.minimum(tile_start, ends[-1] - 1)
    tile_expert = jnp.minimum(jnp.sum(ends[None, :] <= first_row[:, None], axis=1), N_EXPERTS - 1).astype(I32)
    off = jnp.broadcast_to((ends - padded).astype(F32)[:, None], (N_EXPERTS, LANES))
    pos_flat = moe_pos(topt, rank, off, tb).reshape(-1)
    xs = moe_dispatch(xpk, pos_flat, n_tiles * tm_expert, tb)
    ys = expert_ffn(xs, tile_expert, n_valid.reshape(1), wg, wu, wd, tm_expert, tf)
    return moe_combine_ple(h, top, pos_flat, ys, p, ple_nw, ple_wg, ple_wp, tb)


def _pad_cols(w, n):
    return jnp.pad(w, ((0, 0), (0, n - w.shape[1])))


def _tile3(v, n):
    return jnp.pad(jnp.tile(v, 3), (0, LANES - 3 * n)).reshape(1, LANES)


def kernel(x, p, ssm_norm_w, ssm_w_in, ssm_conv_w, ssm_conv_b, ssm_dt_bias, ssm_a_log, ssm_d, ssm_gn_w, ssm_w_out, kv_norm_w, w_kv, b_f, k_norm_w, att_norm_w, att_w_q, q_norm_w, att_w_o, ffn_norm_w, ffn_w_gate, ffn_w_up, ffn_w_down, moe_norm_w, moe_w_router, moe_w_gate, moe_w_up, moe_w_down, ple_norm_w, ple_w_gate, ple_w_proj):
    b, l, d = x.shape
    t = b * l
    n_a = ssm_norm_w.shape[0]
    depth = ple_norm_w.shape[0]
    d_inner = ssm_gn_w.shape[1]
    n_heads = ssm_dt_bias.shape[1]
    cdim = ssm_conv_w.shape[2]
    att_w = att_w_q.shape[2]
    n_att_heads = att_w // ATT_HEADDIM
    tm = min(1024, t)
    tq = min(256, l)

    h = x.reshape(t, d)
    r = jnp.arange(LANES)[:, None]
    c = jnp.arange(d_inner)[None, :]
    expand = ((r < 3 * n_heads) & ((r % n_heads) == (c // SSM_HEADDIM))).astype(BF16)

    kv = ext_q = ext_k = None
    for i in range(depth):
        if i < n_a:
            w_in = ssm_w_in[i].astype(BF16)
            w_dt = w_in[:, d_inner + cdim:]
            w_dt3 = _pad_cols(jnp.concatenate([w_dt, w_dt, w_dt], axis=1), LANES)
            z, xbc, dt_raw = norm_matmul(h, ssm_norm_w[i], [w_in[:, :d_inner], w_in[:, d_inner:d_inner + cdim], w_dt3],
                                         [BF16, BF16, F32], tm // 2)
            y = ssd(z.reshape(b, l, d_inner), xbc.reshape(b, l, cdim), dt_raw.reshape(b, l, LANES),
                    ssm_conv_w[i], ssm_conv_b[i].reshape(1, cdim), _tile3(ssm_dt_bias[i], n_heads),
                    _tile3(ssm_a_log[i], n_heads), jnp.repeat(ssm_d[i], SSM_HEADDIM).reshape(1, d_inner),
                    ssm_gn_w[i].reshape(1, d_inner), expand)
            h = matmul_residual(h, y.reshape(t, d_inner), ssm_w_out[i].astype(BF16), tm)
        else:
            j = i - n_a
            if i == n_a:
                wkv = w_kv.astype(BF16)
                w_f = wkv[:, 2 * att_w:]
                w_f3 = _pad_cols(jnp.concatenate([w_f, w_f, w_f], axis=1), LANES)
                kv, f = norm_matmul(h, kv_norm_w, [wkv[:, :2 * att_w], w_f3], [BF16, F32], tm)
                kv = kv.reshape(b, l, 2 * att_w)
                ext_q, ext_k = forget_ext(f.reshape(b, l, LANES), _tile3(b_f, n_att_heads), n_att_heads)
            qp, = norm_matmul(h, att_norm_w[j], [att_w_q[j].astype(BF16)], [BF16], tm)
            qp = qp.reshape(b, l, att_w)
            qnw2 = jnp.tile(q_norm_w[j], 2).reshape(1, LANES)
            knw2 = jnp.tile(k_norm_w, 2).reshape(1, LANES)
            o = fox_attention(qp, kv, ext_q, ext_k, qnw2, knw2, att_w, tq)
            h = matmul_residual(h, o.reshape(t, att_w), att_w_o[j].astype(BF16), tm)
        p_i = p[i].reshape(t, p.shape[-1])
        ple_wg, ple_wp = ple_w_gate[i].astype(BF16), ple_w_proj[i].astype(BF16)
        if i % 2 == 0:
            dd = i // 2
            h = ffn_ple(h, ffn_norm_w[dd], ffn_w_gate[dd].astype(BF16), ffn_w_up[dd].astype(BF16),
                        ffn_w_down[dd].astype(BF16), p_i, ple_norm_w[i], ple_wg, ple_wp, tm, 512)
        else:
            m = i // 2
            h = moe_ple(h, moe_norm_w[m], moe_w_router[m], moe_w_gate[m].astype(BF16), moe_w_up[m].astype(BF16),
                        moe_w_down[m].astype(BF16), p_i, ple_norm_w[i], ple_wg, ple_wp, tm, tm, min(512, t), 512)
    return h.reshape(b, l, d)
```

```python
import functools
import math

import numpy as np
import jax
import jax.numpy as jnp
from jax import lax
from jax.experimental import pallas as pl
from jax.experimental.pallas import tpu as pltpu

F32 = jnp.float32
BF16 = jnp.bfloat16
I32 = jnp.int32
U32 = jnp.uint32
EPS = 1e-6
HIGHEST = lax.Precision.HIGHEST
LOG2E = math.log2(math.e)

LANES = 128
SUBLANES = 8
MIB = 1024 * 1024

SSM_HEADDIM = 64
SSM_GROUPS = 4
SSM_STATE = 128
CONV_WIDTH = 4
CHUNK = 128
ATT_HEADDIM = 64
N_EXPERTS = 8
TOP_K = 2


def _params(sem, vmem_mib):
    return pltpu.CompilerParams(dimension_semantics=sem, vmem_limit_bytes=vmem_mib * MIB)


def _rms_scale(x):
    return lax.rsqrt(jnp.mean(x * x, axis=-1, keepdims=True) + EPS)


def _sigmoid(x):
    return 1.0 / (1.0 + jnp.exp2(x * (-LOG2E)))


def _softplus(x):
    return jnp.maximum(x, 0.0) + jnp.log(1.0 + jnp.exp(-jnp.abs(x)))


def _pack_bf16_pairs(x):
    n = x.shape[1] // 2
    bits = lax.bitcast_convert_type(x.astype(BF16).astype(F32), U32)
    return lax.shift_right_logical(bits[:, :n], jnp.uint32(16)) | (bits[:, n:] & jnp.uint32(0xFFFF0000))


def _unpack_bf16_pairs(w):
    lo = lax.bitcast_convert_type(lax.shift_left(w, jnp.uint32(16)), F32)
    hi = lax.bitcast_convert_type(w & jnp.uint32(0xFFFF0000), F32)
    return jnp.concatenate([lo, hi], axis=1)


def _norm_matmul_kernel(h_ref, nw_ref, *refs):
    n = len(refs) // 2
    x = h_ref[...]
    xn = (x * _rms_scale(x) * nw_ref[...]).astype(BF16)
    for w_ref, o_ref in zip(refs[:n], refs[n:]):
        o_ref[...] = jnp.dot(xn, w_ref[...], preferred_element_type=F32).astype(o_ref.dtype)


def norm_matmul(h, nw, ws, out_dtypes, tm):
    t, d = h.shape
    resident = lambda shape: pl.BlockSpec(shape, lambda i: (0, 0), pipeline_mode=pl.Buffered(1))
    return pl.pallas_call(
        _norm_matmul_kernel,
        grid=(t // tm,),
        in_specs=[pl.BlockSpec((tm, d), lambda i: (i, 0)), resident((1, d))] + [resident(w.shape) for w in ws],
        out_specs=[pl.BlockSpec((tm, w.shape[1]), lambda i: (i, 0)) for w in ws],
        out_shape=[jax.ShapeDtypeStruct((t, w.shape[1]), dt) for w, dt in zip(ws, out_dtypes)],
        compiler_params=_params(("parallel",), 48),
        name="norm_matmul",
    )(h, nw.reshape(1, d), *ws)


def _matmul_residual_kernel(h_ref, y_ref, w_ref, o_ref):
    o_ref[...] = h_ref[...] + jnp.dot(y_ref[...], w_ref[...], preferred_element_type=F32)


def matmul_residual(h, y, w, tm):
    t, n = h.shape
    k = y.shape[1]
    return pl.pallas_call(
        _matmul_residual_kernel,
        grid=(t // tm,),
        in_specs=[
            pl.BlockSpec((tm, n), lambda i: (i, 0)),
            pl.BlockSpec((tm, k), lambda i: (i, 0)),
            pl.BlockSpec((k, n), lambda i: (0, 0)),
        ],
        out_specs=pl.BlockSpec((tm, n), lambda i: (i, 0)),
        out_shape=jax.ShapeDtypeStruct((t, n), F32),
        compiler_params=_params(("parallel",), 48),
        name="matmul_residual",
    )(h, y, w)


def _ple_update(x, p, nw, wg, wp):
    xn = (x * _rms_scale(x) * nw).astype(BF16)
    gate = _sigmoid(jnp.dot(xn, wg, preferred_element_type=F32))
    proj = jnp.dot(p.astype(BF16), wp, preferred_element_type=F32)
    return x + proj * gate


def _ffn_ple_kernel(h_ref, nw_ref, wg_ref, wu_ref, wd_ref, p_ref, pnw_ref, pwg_ref, pwp_ref, o_ref, xn_ref, acc_ref):
    j = pl.program_id(1)

    @pl.when(j == 0)
    def _():
        x = h_ref[...]
        xn_ref[...] = (x * _rms_scale(x) * nw_ref[...]).astype(xn_ref.dtype)
        acc_ref[...] = x

    xn = xn_ref[...]
    g = jnp.dot(xn, wg_ref[...], preferred_element_type=F32)
    u = jnp.dot(xn, wu_ref[...], preferred_element_type=F32)
    a = (g * _sigmoid(g) * u).astype(BF16)
    acc_ref[...] += jnp.dot(a, wd_ref[...], preferred_element_type=F32)

    @pl.when(j == pl.num_programs(1) - 1)
    def _():
        o_ref[...] = _ple_update(acc_ref[...], p_ref[...], pnw_ref[...], pwg_ref[...], pwp_ref[...])


def ffn_ple(h, nw, wg, wu, wd, p, ple_nw, ple_wg, ple_wp, tm, tf):
    t, d = h.shape
    f = wg.shape[1]
    pd = p.shape[1]
    resident = lambda shape: pl.BlockSpec(shape, lambda i, j: (0, 0), pipeline_mode=pl.Buffered(1))
    return pl.pallas_call(
        _ffn_ple_kernel,
        grid=(t // tm, f // tf),
        in_specs=[
            pl.BlockSpec((tm, d), lambda i, j: (i, 0)),
            resident((1, d)),
            pl.BlockSpec((d, tf), lambda i, j: (0, j)),
            pl.BlockSpec((d, tf), lambda i, j: (0, j)),
            pl.BlockSpec((tf, d), lambda i, j: (j, 0)),
            pl.BlockSpec((tm, pd), lambda i, j: (i, 0)),
            resident((1, d)), resident((d, d)), resident((pd, d)),
        ],
        out_specs=pl.BlockSpec((tm, d), lambda i, j: (i, 0)),
        out_shape=jax.ShapeDtypeStruct((t, d), F32),
        scratch_shapes=[pltpu.VMEM((tm, d), BF16), pltpu.VMEM((tm, d), F32)],
        compiler_params=_params(("parallel", "arbitrary"), 52),
        name="ffn_ple",
    )(h, nw.reshape(1, d), wg, wu, wd, p, ple_nw.reshape(1, d), ple_wg, ple_wp)


def _split3(v, n):
    v1 = v.astype(BF16)
    r1 = v - v1.astype(F32)
    v2 = r1.astype(BF16)
    v3 = (r1 - v2.astype(F32)).astype(BF16)
    lane = lax.broadcasted_iota(I32, v.shape, 1)
    return jnp.where(lane < n, v1, jnp.where(lane < 2 * n, v2, v3))


def _ssd_kernel(z_ref, xbc_ref, dt_ref, cw_ref, cb_ref, dtb_ref, alog_ref, dskip_ref, gnw_ref, exp_ref,
                o_ref, cbuf, xact, state, acs_t, y_scr):
    d_inner = z_ref.shape[-1]
    gw = d_inner // SSM_GROUPS
    n_heads = d_inner // SSM_HEADDIM
    q = CHUNK

    tail = cbuf.shape[0] - q

    @pl.when(pl.program_id(1) == 0)
    def _():
        cbuf[0:tail, :] = jnp.zeros((tail, cbuf.shape[1]), BF16)
        state[...] = jnp.zeros(state.shape, F32)

    cbuf[tail:, :] = xbc_ref[0]
    cdim = cbuf.shape[1]
    nshift = CONV_WIDTH - 1
    srow = lax.broadcasted_iota(I32, (nshift * q, tail + q), 0)
    scol = lax.broadcasted_iota(I32, (nshift * q, tail + q), 1)
    shift = jnp.where(scol == srow - (srow // q) * (q - 1) + (tail - nshift), 1.0, 0.0).astype(BF16)
    cw_lanes = 2 * LANES
    for j in range(cdim // cw_lanes):
        sl = slice(j * cw_lanes, (j + 1) * cw_lanes)
        shifted = jnp.dot(shift, cbuf[:, sl], preferred_element_type=F32)
        acc = cb_ref[:, sl] + cbuf[tail:, sl].astype(F32) * cw_ref[nshift:nshift + 1, sl]
        for w in range(nshift):
            acc = acc + shifted[w * q:(w + 1) * q] * cw_ref[w:w + 1, sl]
        xact[:, sl] = acc * _sigmoid(acc)
    cbuf[0:tail, :] = cbuf[q:q + tail, :]

    dt = _softplus(dt_ref[0] + dtb_ref[...])
    a = dt * (-LOG2E * jnp.exp(alog_ref[...]))
    row = lax.broadcasted_iota(I32, (q, q), 0)
    col = lax.broadcasted_iota(I32, (q, q), 1)
    causal = row >= col
    tril = jnp.where(causal, 1.0, 0.0).astype(F32)
    acs = jnp.dot(tril, a, preferred_element_type=F32, precision=HIGHEST)
    acs_t[...] = acs.T
    a_last = acs[q - 1:q, :]
    dt3 = _split3(dt, n_heads)
    dec3 = _split3(jnp.exp2(a_last - acs), n_heads)
    e3 = _split3(jnp.exp2(acs), n_heads)

    for g in range(SSM_GROUPS):
        gl = slice(g * gw, (g + 1) * gw)
        ex = exp_ref[:, gl]
        xs = xact[:, gl]
        b_g = xact[:, d_inner + g * SSM_STATE:d_inner + (g + 1) * SSM_STATE].astype(BF16)
        c_g = xact[:, d_inner + (SSM_GROUPS + g) * SSM_STATE:d_inner + (SSM_GROUPS + g + 1) * SSM_STATE].astype(BF16)
        dt_e = jnp.dot(dt3, ex, preferred_element_type=F32)
        dec_e = jnp.dot(dec3, ex, preferred_element_type=F32)
        e_e = jnp.dot(e3, ex, preferred_element_type=F32)
        xc = xs * dt_e
        xc_b = xc.astype(BF16)
        xdec_b = (xc * dec_e).astype(BF16)
        scores = lax.dot_general(c_g, b_g, (((1,), (1,)), ((), ())), preferred_element_type=F32)
        st = state[g]
        y = jnp.dot(c_g, st.astype(BF16), preferred_element_type=F32) * e_e
        y = y + xs * dskip_ref[:, gl]
        lane = lax.broadcasted_iota(I32, (q, LANES), 1)
        ydiag = []
        for pr in range(gw // LANES):
            ms = []
            for hh in range(2):
                h = g * (gw // SSM_HEADDIM) + 2 * pr + hh
                seg = acs[:, h:h + 1] - acs_t[h:h + 1, :]
                ms.append(scores * jnp.exp2(jnp.where(causal, seg, -jnp.inf)))
            lhs = jnp.concatenate(ms, axis=1).astype(BF16)
            xp = xc_b[:, pr * LANES:(pr + 1) * LANES]
            zero = jnp.zeros_like(xp)
            rhs = jnp.concatenate([jnp.where(lane < SSM_HEADDIM, xp, zero),
                                   jnp.where(lane >= SSM_HEADDIM, xp, zero)], axis=0)
            ydiag.append(jnp.dot(lhs, rhs, preferred_element_type=F32))
        y_scr[:, gl] = y + jnp.concatenate(ydiag, axis=1)
        upd = lax.dot_general(b_g, xdec_b, (((0,), (0,)), ((), ())), preferred_element_type=F32)
        state[g] = st * e_e[q - 1:q, :] + upd

    zz = z_ref[0].astype(F32)
    y = y_scr[...] * (zz * _sigmoid(zz))
    o_ref[0] = (y * _rms_scale(y) * gnw_ref[...]).astype(o_ref.dtype)


def ssd(z, xbc, dt_raw, conv_w, conv_b, dt_bias3, a_log3, dskip_e, gn_w, expand):
    b, l, d_inner = z.shape
    cdim = xbc.shape[-1]
    gw = d_inner // SSM_GROUPS
    nc = l // CHUNK
    full = lambda shape: pl.BlockSpec(shape, lambda i, c: (0,) * len(shape))
    return pl.pallas_call(
        _ssd_kernel,
        grid=(b, nc),
        in_specs=[
            pl.BlockSpec((1, CHUNK, d_inner), lambda i, c: (i, c, 0)),
            pl.BlockSpec((1, CHUNK, cdim), lambda i, c: (i, c, 0)),
            pl.BlockSpec((1, CHUNK, LANES), lambda i, c: (i, c, 0)),
            full((CONV_WIDTH, cdim)), full((1, cdim)), full((1, LANES)), full((1, LANES)),
            full((1, d_inner)), full((1, d_inner)), full((LANES, d_inner)),
        ],
        out_specs=pl.BlockSpec((1, CHUNK, d_inner), lambda i, c: (i, c, 0)),
        out_shape=jax.ShapeDtypeStruct((b, l, d_inner), BF16),
        scratch_shapes=[
            pltpu.VMEM((CHUNK + 16, cdim), BF16),
            pltpu.VMEM((CHUNK, cdim), F32),
            pltpu.VMEM((SSM_GROUPS, SSM_STATE, gw), F32),
            pltpu.VMEM((LANES, CHUNK), F32),
            pltpu.VMEM((CHUNK, d_inner), F32),
        ],
        compiler_params=_params(("parallel", "arbitrary"), 40),
        name="ssd",
    )(z, xbc, dt_raw, conv_w, conv_b, dt_bias3, a_log3, dskip_e, gn_w, expand)


def _cum_kernel(f_ref, bf_ref, pq_ref, pk_ref, eq_ref, ek_ref, *, blk, n_heads):
    l = f_ref.shape[1]
    row = lax.broadcasted_iota(I32, (blk, blk), 0)
    col = lax.broadcasted_iota(I32, (blk, blk), 1)
    tril = jnp.where(row >= col, 1.0, 0.0).astype(F32)
    lane = lax.broadcasted_iota(I32, (blk, LANES), 1)
    carry = jnp.zeros((1, LANES), F32)
    for i in range(l // blk):
        rs = slice(i * blk, (i + 1) * blk)
        x = f_ref[0, rs, :] + bf_ref[...]
        lf = -_softplus(-x)
        cs = jnp.dot(tril, lf, preferred_element_type=F32, precision=HIGHEST) + carry
        carry = cs[blk - 1:blk, :]
        lhs = jnp.where(lane == 3 * n_heads, jnp.ones((), BF16), _split3(cs * LOG2E, n_heads))
        eq_ref[0, rs, :] = jnp.dot(lhs, pq_ref[...], preferred_element_type=F32).astype(BF16)
        ek_ref[0, rs, :] = jnp.dot(lhs, pk_ref[...], preferred_element_type=F32).astype(BF16)


def _ext_placement(n_heads):
    w = n_heads * ATT_HEADDIM
    pq = np.zeros((LANES, w), np.float32)
    pk = np.zeros((LANES, w), np.float32)
    one = 3 * n_heads
    for h in range(n_heads):
        base = (h // 2) * LANES + (ATT_HEADDIM if h % 2 == 0 else 0)
        for piece in range(3):
            pq[piece * n_heads + h, base + piece] = 1.0
            pq[one, base + 3 + piece] = 1.0
            pk[one, base + piece] = 1.0
            pk[piece * n_heads + h, base + 3 + piece] = -1.0
    return jnp.asarray(pq, BF16), jnp.asarray(pk, BF16)


def forget_ext(f, b_f3, n_heads):
    b, l, _ = f.shape
    w = n_heads * ATT_HEADDIM
    pq, pk = _ext_placement(n_heads)
    out = jax.ShapeDtypeStruct((b, l, w), BF16)
    return pl.pallas_call(
        functools.partial(_cum_kernel, blk=256, n_heads=n_heads),
        grid=(b,),
        in_specs=[
            pl.BlockSpec((1, l, LANES), lambda i: (i, 0, 0)),
            pl.BlockSpec((1, LANES), lambda i: (0, 0)),
            pl.BlockSpec((LANES, w), lambda i: (0, 0)),
            pl.BlockSpec((LANES, w), lambda i: (0, 0)),
        ],
        out_specs=[pl.BlockSpec((1, l, w), lambda i: (i, 0, 0)), pl.BlockSpec((1, l, w), lambda i: (i, 0, 0))],
        out_shape=[out, out],
        compiler_params=_params(("parallel",), 40),
        name="forget_ext",
    )(f, b_f3, pq, pk)


def _head_rms(x, lane_lo):
    sq = x * x
    s_lo = jnp.sum(jnp.where(lane_lo, sq, 0.0), axis=-1, keepdims=True)
    s_hi = jnp.sum(jnp.where(lane_lo, 0.0, sq), axis=-1, keepdims=True)
    inv = 1.0 / ATT_HEADDIM
    return jnp.where(lane_lo, lax.rsqrt(s_lo * inv + EPS), lax.rsqrt(s_hi * inv + EPS))


def _attn_kernel(q_ref, k_ref, v_ref, eq_ref, ek_ref, qnw_ref, knw_ref, o_ref, kaug, vaugt, s_scr, p_scr, *, tq):
    l = q_ref.shape[1]
    nblk = l // tq
    hd = ATT_HEADDIM
    lane_lo = lax.broadcasted_iota(I32, (tq, LANES), 1) < hd
    sub_lo = lax.broadcasted_iota(I32, (LANES, tq), 0) < hd
    row = lax.broadcasted_iota(I32, (tq, tq), 0)
    col = lax.broadcasted_iota(I32, (tq, tq), 1)
    visible = row <= col
    eye_q = jnp.where(row == col, 1.0, 0.0).astype(BF16)
    eye_l = eye_q[:LANES, :LANES]
    nt = (((1,), (1,)), ((), ()))
    qscale = qnw_ref[...] * (hd ** -0.5 * LOG2E)

    for i in range(nblk):
        rs = slice(i * tq, (i + 1) * tq)
        kk = k_ref[0, rs, :].astype(F32)
        kn = (kk * _head_rms(kk, lane_lo) * knw_ref[...]).astype(BF16)
        ek = ek_ref[0, rs, :]
        kaug[0, rs, :] = jnp.where(lane_lo, kn, ek)
        kaug[1, rs, :] = jnp.where(lane_lo, ek, kn)
        vt = lax.dot_general(eye_l, v_ref[0, rs, :], nt, preferred_element_type=F32).astype(BF16)
        one = jnp.ones_like(vt)
        vaugt[0, :, rs] = jnp.where(sub_lo, vt, one)
        vaugt[1, :, rs] = jnp.where(sub_lo, one, vt)

    def scores(qi):
        qs = slice(qi * tq, (qi + 1) * tq)
        nk = (qi + 1) * tq
        qq = q_ref[0, qs, :].astype(F32)
        qn = (qq * _head_rms(qq, lane_lo) * qscale).astype(BF16)
        eq = eq_ref[0, qs, :]
        for hh in range(2):
            qa = jnp.where(lane_lo, qn, eq) if hh == 0 else jnp.where(lane_lo, eq, qn)
            qt = lax.dot_general(eye_l, qa, nt, preferred_element_type=F32).astype(BF16)
            s = jnp.dot(kaug[hh, :nk, :], qt, preferred_element_type=F32)
            s_scr[qi % 2, hh, :nk, :] = s

    def softmax_pv(qi):
        qs = slice(qi * tq, (qi + 1) * tq)
        nk = (qi + 1) * tq
        par = qi % 2
        accs = []
        for hh in range(2):
            s_scr[par, hh, nk - tq:nk, :] = jnp.where(visible, s_scr[par, hh, nk - tq:nk, :], -jnp.inf)
            m = jnp.max(s_scr[par, hh, :nk, :], axis=0, keepdims=True)
            p_scr[par, hh, :nk, :] = jnp.exp2(s_scr[par, hh, :nk, :] - m).astype(BF16)
            accs.append(jnp.dot(vaugt[hh, :, :nk], p_scr[par, hh, :nk, :], preferred_element_type=F32))
        a0, a1 = accs
        o_t = jnp.concatenate([a0[:hd] / a0[hd:], a1[hd:] / a1[:hd]], axis=0).astype(BF16)
        o_ref[0, qs, :] = lax.dot_general(eye_q, o_t, nt, preferred_element_type=F32).astype(o_ref.dtype)

    scores(0)
    for qi in range(nblk):
        if qi + 1 < nblk:
            scores(qi + 1)
        softmax_pv(qi)


def fox_attention(qp, kv, ext_q, ext_k, qnw2, knw2, att_w, tq):
    b, l, _ = qp.shape
    npair = att_w // LANES
    nblk = l // tq
    blk = lambda off: pl.BlockSpec((1, l, LANES), lambda i, j: (i, 0, j + off))
    return pl.pallas_call(
        functools.partial(_attn_kernel, tq=tq),
        grid=(b, npair),
        in_specs=[
            blk(0), blk(0), blk(npair), blk(0), blk(0),
            pl.BlockSpec((1, LANES), lambda i, j: (0, 0)),
            pl.BlockSpec((1, LANES), lambda i, j: (0, 0)),
        ],
        out_specs=blk(0),
        out_shape=jax.ShapeDtypeStruct((b, l, att_w), BF16),
        scratch_shapes=[
            pltpu.VMEM((2, l, LANES), BF16),
            pltpu.VMEM((2, LANES, l), BF16),
            pltpu.VMEM((2, 2, l, tq), F32),
            pltpu.VMEM((2, 2, l, tq), BF16),
        ],
        compiler_params=_params(("parallel", "parallel"), 40),
        name="fox_attention",
    )(qp, kv, kv, ext_q, ext_k, qnw2, knw2)


def _router_kernel(h_ref, nw_ref, wr_ref, xpk_ref, topt_ref, sel_ref, top_ref):
    x = h_ref[...]
    u = x * _rms_scale(x) * nw_ref[...]
    xpk_ref[...] = _pack_bf16_pairs(u)
    logits = jnp.dot(u, wr_ref[...], preferred_element_type=F32, precision=HIGHEST)
    lt = logits.T[:N_EXPERTS, :]
    sub = lax.broadcasted_iota(I32, lt.shape, 0)
    neg = jnp.float32(-jnp.inf)
    m1 = jnp.max(lt, axis=0, keepdims=True)
    i1 = jnp.min(jnp.where(lt == m1, sub, N_EXPERTS), axis=0, keepdims=True)
    rest = jnp.where(sub == i1, neg, lt)
    m2 = jnp.max(rest, axis=0, keepdims=True)
    i2 = jnp.min(jnp.where(rest == m2, sub, N_EXPERTS), axis=0, keepdims=True)
    e2 = jnp.exp(m2 - m1)
    g1 = 1.0 / (1.0 + e2)
    g2 = e2 / (1.0 + e2)
    sel_ref[...] = jnp.where((sub == i1) | (sub == i2), 1.0, 0.0).astype(F32)
    topt = jnp.where(sub == 0, i1.astype(F32),
                     jnp.where(sub == 1, i2.astype(F32), jnp.where(sub == 2, g1, jnp.where(sub == 3, g2, 0.0))))
    topt_ref[...] = topt
    pad = jnp.zeros((LANES - N_EXPERTS, topt.shape[1]), F32)
    top_ref[...] = jnp.concatenate([topt, pad], axis=0).T


def router(h, nw, wr_pad, tm):
    t, d = h.shape
    return pl.pallas_call(
        _router_kernel,
        grid=(t // tm,),
        in_specs=[
            pl.BlockSpec((tm, d), lambda i: (i, 0)),
            pl.BlockSpec((1, d), lambda i: (0, 0)),
            pl.BlockSpec((d, LANES), lambda i: (0, 0)),
        ],
        out_specs=[
            pl.BlockSpec((tm, d // 2), lambda i: (i, 0)),
            pl.BlockSpec((N_EXPERTS, tm), lambda i: (0, i)),
            pl.BlockSpec((N_EXPERTS, tm), lambda i: (0, i)),
            pl.BlockSpec((tm, LANES), lambda i: (i, 0)),
        ],
        out_shape=[
            jax.ShapeDtypeStruct((t, d // 2), U32),
            jax.ShapeDtypeStruct((N_EXPERTS, t), F32),
            jax.ShapeDtypeStruct((N_EXPERTS, t), F32),
            jax.ShapeDtypeStruct((t, LANES), F32),
        ],
        compiler_params=_params(("parallel",), 40),
        name="router",
    )(h, nw.reshape(1, d), wr_pad)


def _rank_kernel(sel_ref, rank_ref, cnt_ref, carry):
    @pl.when(pl.program_id(0) == 0)
    def _():
        carry[...] = jnp.zeros(carry.shape, F32)

    s = sel_ref[...]
    tb = s.shape[1]
    row = lax.broadcasted_iota(I32, (tb, tb), 0)
    col = lax.broadcasted_iota(I32, (tb, tb), 1)
    before = jnp.where(row < col, 1.0, 0.0).astype(BF16)
    c = carry[...]
    rank_ref[...] = jnp.dot(s.astype(BF16), before, preferred_element_type=F32) + c[:, :1]
    c = c + jnp.sum(s, axis=1, keepdims=True)
    carry[...] = c
    cnt_ref[...] = c


def moe_rank(sel, tb):
    ne, t = sel.shape
    return pl.pallas_call(
        _rank_kernel,
        grid=(t // tb,),
        in_specs=[pl.BlockSpec((ne, tb), lambda i: (0, i))],
        out_specs=[pl.BlockSpec((ne, tb), lambda i: (0, i)), pl.BlockSpec((ne, LANES), lambda i: (0, 0))],
        out_shape=[jax.ShapeDtypeStruct((ne, t), F32), jax.ShapeDtypeStruct((ne, LANES), F32)],
        scratch_shapes=[pltpu.VMEM((ne, LANES), F32)],
        compiler_params=_params(("arbitrary",), 32),
        name="moe_rank",
    )(sel)


def _pos_kernel(topt_ref, rank_ref, off_ref, pos_ref):
    posv = rank_ref[...] + off_ref[...][:, :1]
    sub = lax.broadcasted_iota(I32, posv.shape, 0).astype(F32)
    for k in range(TOP_K):
        ek = topt_ref[k:k + 1, :]
        pos_ref[0, k:k + 1, :] = jnp.sum(jnp.where(sub == ek, posv, 0.0), axis=0, keepdims=True).astype(I32)


def moe_pos(topt, rank, off, tb):
    ne, t = rank.shape
    return pl.pallas_call(
        _pos_kernel,
        grid=(t // tb,),
        in_specs=[
            pl.BlockSpec((ne, tb), lambda i: (0, i)),
            pl.BlockSpec((ne, tb), lambda i: (0, i)),
            pl.BlockSpec((ne, LANES), lambda i: (0, 0)),
        ],
        out_specs=pl.BlockSpec((1, TOP_K, tb), lambda i: (i, 0, 0)),
        out_shape=jax.ShapeDtypeStruct((t // tb, TOP_K, tb), I32),
        compiler_params=_params(("parallel",), 32),
        name="moe_pos",
    )(topt, rank, off)


def _pos_copy(pos_hbm, pos_smem, psem, tile, slot, n):
    return pltpu.make_async_copy(pos_hbm.at[pl.ds(pl.multiple_of(tile * n, n), n)],
                                 pos_smem.at[pl.ds(pl.multiple_of(slot * n, n), n)], psem.at[slot])


def _dispatch_kernel(x_ref, pos_hbm, init_hbm, xs_hbm, pos_smem, psem, sem, *, tb):
    del init_hbm
    i = pl.program_id(0)
    nsteps = pl.num_programs(0)
    slot = i % 2
    pos_copy = functools.partial(_pos_copy, pos_hbm, pos_smem, psem, n=TOP_K * tb)

    @pl.when(i == 0)
    def _():
        pos_copy(0, 0).start()

    @pl.when(i + 1 < nsteps)
    def _():
        pos_copy(i + 1, 1 - slot).start()

    pos_copy(i, slot).wait()

    def row_copy(r, k):
        return pltpu.make_async_copy(x_ref.at[pl.ds(r, 1)],
                                     xs_hbm.at[pl.ds(pos_smem[slot * (TOP_K * tb) + k * tb + r], 1)], sem)

    def issue(r, _):
        for k in range(TOP_K):
            row_copy(r, k).start()
        return 0

    lax.fori_loop(0, tb, issue, 0, unroll=8)

    def drain(r, _):
        for k in range(TOP_K):
            row_copy(r, k).wait()
        return 0

    lax.fori_loop(0, tb, drain, 0, unroll=8)


def moe_dispatch(xpk, pos_flat, rows, tb):
    t, dw = xpk.shape
    init = jnp.zeros((rows, dw), U32)
    return pl.pallas_call(
        functools.partial(_dispatch_kernel, tb=tb),
        grid=(t // tb,),
        in_specs=[
            pl.BlockSpec((tb, dw), lambda i: (i, 0)),
            pl.BlockSpec(memory_space=pl.ANY),
            pl.BlockSpec(memory_space=pl.ANY),
        ],
        out_specs=pl.BlockSpec(memory_space=pl.ANY),
        out_shape=jax.ShapeDtypeStruct((rows, dw), U32),
        scratch_shapes=[pltpu.SMEM((2 * TOP_K * tb,), I32), pltpu.SemaphoreType.DMA((2,)), pltpu.SemaphoreType.DMA],
        input_output_aliases={2: 0},
        compiler_params=_params(("arbitrary",), 32),
        name="moe_dispatch",
    )(xpk, pos_flat, init)


def _expert_ffn_kernel(te_ref, nv_ref, x_ref, wg_ref, wu_ref, wd_ref, o_ref, xn_ref, acc_ref):
    del te_ref
    i = pl.program_id(0)
    j = pl.program_id(1)

    @pl.when(i < nv_ref[0])
    def _():
        @pl.when(j == 0)
        def _():
            xn_ref[...] = _unpack_bf16_pairs(x_ref[...]).astype(BF16)
            acc_ref[...] = jnp.zeros(acc_ref.shape, F32)

        xn = xn_ref[...]
        g = jnp.dot(xn, wg_ref[0].astype(BF16), preferred_element_type=F32)
        u = jnp.dot(xn, wu_ref[0].astype(BF16), preferred_element_type=F32)
        a = (g * _sigmoid(g) * u).astype(BF16)
        acc_ref[...] += jnp.dot(a, wd_ref[0].astype(BF16), preferred_element_type=F32)

        @pl.when(j == pl.num_programs(1) - 1)
        def _():
            o_ref[...] = _pack_bf16_pairs(acc_ref[...])

    @pl.when((i >= nv_ref[0]) & (j == pl.num_programs(1) - 1))
    def _():
        o_ref[...] = jnp.zeros(o_ref.shape, o_ref.dtype)


def expert_ffn(xs, tile_expert, n_valid, wg, wu, wd, tm, tf):
    rows, dw = xs.shape
    ne, d, f = wg.shape
    nf = f // tf
    row_blk = lambda i, j, te, nv: (jnp.maximum(jnp.minimum(i, nv[0] - 1), 0), 0)
    out_blk = lambda i, j, te, nv: (i, 0)
    col_j = lambda i, j, nv: jnp.where(i < nv[0], j, nf - 1)
    return pl.pallas_call(
        _expert_ffn_kernel,
        grid_spec=pltpu.PrefetchScalarGridSpec(
            num_scalar_prefetch=2,
            grid=(rows // tm, nf),
            in_specs=[
                pl.BlockSpec((tm, dw), row_blk),
                pl.BlockSpec((1, d, tf), lambda i, j, te, nv: (te[i], 0, col_j(i, j, nv))),
                pl.BlockSpec((1, d, tf), lambda i, j, te, nv: (te[i], 0, col_j(i, j, nv))),
                pl.BlockSpec((1, tf, d), lambda i, j, te, nv: (te[i], col_j(i, j, nv), 0)),
            ],
            out_specs=pl.BlockSpec((tm, dw), out_blk),
            scratch_shapes=[pltpu.VMEM((tm, d), BF16), pltpu.VMEM((tm, d), F32)],
        ),
        out_shape=jax.ShapeDtypeStruct((rows, dw), U32),
        compiler_params=_params(("arbitrary", "arbitrary"), 48),
        name="expert_ffn",
    )(tile_expert, n_valid, xs, wg, wu, wd)


def _combine_ple_kernel(h_ref, top_ref, p_ref, nw_ref, wg_ref, wp_ref, pos_hbm, ys_hbm, o_ref,
                        ybuf, pos_smem, psem, sem, *, tb):
    i = pl.program_id(0)
    nsteps = pl.num_programs(0)
    slot = i % 2
    pos_copy = functools.partial(_pos_copy, pos_hbm, pos_smem, psem, n=TOP_K * tb)

    def row_copy(s, r, k):
        return pltpu.make_async_copy(ys_hbm.at[pl.ds(pos_smem[s * (TOP_K * tb) + k * tb + r], 1)],
                                     ybuf.at[s, k, pl.ds(r, 1)], sem.at[s])

    def gather_start(s):
        def issue(r, _):
            for k in range(TOP_K):
                row_copy(s, r, k).start()
            return 0
        lax.fori_loop(0, tb, issue, 0, unroll=8)

    def gather_wait(s):
        def drain(r, _):
            for k in range(TOP_K):
                row_copy(s, r, k).wait()
            return 0
        lax.fori_loop(0, tb, drain, 0, unroll=8)

    @pl.when(i == 0)
    def _():
        pos_copy(0, 0).start()
        pos_copy(0, 0).wait()
        gather_start(0)

        @pl.when(nsteps > 1)
        def _():
            pos_copy(1, 1).start()

    @pl.when(i + 1 < nsteps)
    def _():
        pos_copy(i + 1, 1 - slot).wait()
        gather_start(1 - slot)

    gather_wait(slot)

    @pl.when(i + 2 < nsteps)
    def _():
        pos_copy(i + 2, slot).start()

    top = top_ref[...]
    x = h_ref[...]
    for k in range(TOP_K):
        x = x + top[:, TOP_K + k:TOP_K + k + 1] * _unpack_bf16_pairs(ybuf[slot, k])
    o_ref[...] = _ple_update(x, p_ref[...], nw_ref[...], wg_ref[...], wp_ref[...])


def moe_combine_ple(h, top, pos_flat, ys, p, nw, wg, wp, tb):
    t, d = h.shape
    dw = ys.shape[1]
    pd = p.shape[1]
    resident = lambda shape: pl.BlockSpec(shape, lambda i: (0, 0), pipeline_mode=pl.Buffered(1))
    return pl.pallas_call(
        functools.partial(_combine_ple_kernel, tb=tb),
        grid=(t // tb,),
        in_specs=[
            pl.BlockSpec((tb, d), lambda i: (i, 0)),
            pl.BlockSpec((tb, LANES), lambda i: (i, 0)),
            pl.BlockSpec((tb, pd), lambda i: (i, 0)),
            resident((1, d)), resident((d, d)), resident((pd, d)),
            pl.BlockSpec(memory_space=pl.ANY),
            pl.BlockSpec(memory_space=pl.ANY),
        ],
        out_specs=pl.BlockSpec((tb, d), lambda i: (i, 0)),
        out_shape=jax.ShapeDtypeStruct((t, d), F32),
        scratch_shapes=[pltpu.VMEM((2, TOP_K, tb, dw), U32), pltpu.SMEM((2 * TOP_K * tb,), I32),
                        pltpu.SemaphoreType.DMA((2,)), pltpu.SemaphoreType.DMA((2,))],
        compiler_params=_params(("arbitrary",), 40),
        name="moe_combine_ple",
    )(h, top, p, nw.reshape(1, d), wg, wp, pos_flat, ys)


def moe_ple(h, nw, w_router, wg, wu, wd, p, ple_nw, ple_wg, ple_wp, tm_route, tm_expert, tb, tf):
    t, d = h.shape
    xpk, topt, sel, top = router(h, nw, _pad_cols(w_router, LANES), tm_route)
    rank, cnt = moe_rank(sel, tb)
    counts = cnt[:, 0].astype(I32)
    padded = ((counts + tm_expert - 1) // tm_expert) * tm_expert
    ends = jnp.cumsum(padded)
    n_tiles = (TOP_K * t) // tm_expert + N_EXPERTS
    n_valid = (ends[-1] // tm_expert).astype(I32)
    tile_start = jnp.arange(n_tiles, dtype=I32) * tm_expert
    first_row = jnp.minimum(tile_start, ends[-1] - 1)
    tile_expert = jnp.minimum(jnp.sum(ends[None, :] <= first_row[:, None], axis=1), N_EXPERTS - 1).astype(I32)
    off = jnp.broadcast_to((ends - padded).astype(F32)[:, None], (N_EXPERTS, LANES))
    pos_flat = moe_pos(topt, rank, off, tb).reshape(-1)
    xs = moe_dispatch(xpk, pos_flat, n_tiles * tm_expert, tb)
    ys = expert_ffn(xs, tile_expert, n_valid.reshape(1), wg, wu, wd, tm_expert, tf)
    return moe_combine_ple(h, top, pos_flat, ys, p, ple_nw, ple_wg, ple_wp, tb)


def _pad_cols(w, n):
    return jnp.pad(w, ((0, 0), (0, n - w.shape[1])))


def _tile3(v, n):
    return jnp.pad(jnp.tile(v, 3), (0, LANES - 3 * n)).reshape(1, LANES)


def kernel(x, p, ssm_norm_w, ssm_w_in, ssm_conv_w, ssm_conv_b, ssm_dt_bias, ssm_a_log, ssm_d, ssm_gn_w, ssm_w_out, kv_norm_w, w_kv, b_f, k_norm_w, att_norm_w, att_w_q, q_norm_w, att_w_o, ffn_norm_w, ffn_w_gate, ffn_w_up, ffn_w_down, moe_norm_w, moe_w_router, moe_w_gate, moe_w_up, moe_w_down, ple_norm_w, ple_w_gate, ple_w_proj):
    b, l, d = x.shape
    t = b * l
    n_a = ssm_norm_w.shape[0]
    depth = ple_norm_w.shape[0]
    d_inner = ssm_gn_w.shape[1]
    n_heads = ssm_dt_bias.shape[1]
    cdim = ssm_conv_w.shape[2]
    att_w = att_w_q.shape[2]
    n_att_heads = att_w // ATT_HEADDIM
    tm = min(1024, t)
    tq = min(256, l)

    h = x.reshape(t, d)
    r = jnp.arange(LANES)[:, None]
    c = jnp.arange(d_inner)[None, :]
    expand = ((r < 3 * n_heads) & ((r % n_heads) == (c // SSM_HEADDIM))).astype(BF16)

    kv = ext_q = ext_k = None
    for i in range(depth):
        if i < n_a:
            w_in = ssm_w_in[i].astype(BF16)
            w_dt = w_in[:, d_inner + cdim:]
            w_dt3 = _pad_cols(jnp.concatenate([w_dt, w_dt, w_dt], axis=1), LANES)
            z, xbc, dt_raw = norm_matmul(h, ssm_norm_w[i], [w_in[:, :d_inner], w_in[:, d_inner:d_inner + cdim], w_dt3],
                                         [BF16, BF16, F32], tm // 2)
            y = ssd(z.reshape(b, l, d_inner), xbc.reshape(b, l, cdim), dt_raw.reshape(b, l, LANES),
                    ssm_conv_w[i], ssm_conv_b[i].reshape(1, cdim), _tile3(ssm_dt_bias[i], n_heads),
                    _tile3(ssm_a_log[i], n_heads), jnp.repeat(ssm_d[i], SSM_HEADDIM).reshape(1, d_inner),
                    ssm_gn_w[i].reshape(1, d_inner), expand)
            h = matmul_residual(h, y.reshape(t, d_inner), ssm_w_out[i].astype(BF16), tm)
        else:
            j = i - n_a
            if i == n_a:
                wkv = w_kv.astype(BF16)
                w_f = wkv[:, 2 * att_w:]
                w_f3 = _pad_cols(jnp.concatenate([w_f, w_f, w_f], axis=1), LANES)
                kv, f = norm_matmul(h, kv_norm_w, [wkv[:, :2 * att_w], w_f3], [BF16, F32], tm)
                kv = kv.reshape(b, l, 2 * att_w)
                ext_q, ext_k = forget_ext(f.reshape(b, l, LANES), _tile3(b_f, n_att_heads), n_att_heads)
            qp, = norm_matmul(h, att_norm_w[j], [att_w_q[j].astype(BF16)], [BF16], tm)
            qp = qp.reshape(b, l, att_w)
            qnw2 = jnp.tile(q_norm_w[j], 2).reshape(1, LANES)
            knw2 = jnp.tile(k_norm_w, 2).reshape(1, LANES)
            o = fox_attention(qp, kv, ext_q, ext_k, qnw2, knw2, att_w, tq)
            h = matmul_residual(h, o.reshape(t, att_w), att_w_o[j].astype(BF16), tm)
        p_i = p[i].reshape(t, p.shape[-1])
        ple_wg, ple_wp = ple_w_gate[i].astype(BF16), ple_w_proj[i].astype(BF16)
        if i % 2 == 0:
            dd = i // 2
            h = ffn_ple(h, ffn_norm_w[dd], ffn_w_gate[dd].astype(BF16), ffn_w_up[dd].astype(BF16),
                        ffn_w_down[dd].astype(BF16), p_i, ple_norm_w[i], ple_wg, ple_wp, tm, 512)
        else:
            m = i // 2
            h = moe_ple(h, moe_norm_w[m], moe_w_router[m], moe_w_gate[m], moe_w_up[m], moe_w_down[m], p_i,
                        ple_norm_w[i], ple_wg, ple_wp, tm, tm, min(512, t), 512)
    return h.reshape(b, l, d)
```

```python
import functools
import math

import numpy as np
import jax
import jax.numpy as jnp
from jax import lax
from jax.experimental import pallas as pl
from jax.experimental.pallas import tpu as pltpu
from jax.experimental.pallas import tpu_sc as plsc

F32 = jnp.float32
BF16 = jnp.bfloat16
I32 = jnp.int32
U32 = jnp.uint32
EPS = 1e-6
HIGHEST = lax.Precision.HIGHEST
LOG2E = math.log2(math.e)

LANES = 128
SUBLANES = 8
MIB = 1024 * 1024

SSM_HEADDIM = 64
SSM_GROUPS = 4
SSM_STATE = 128
CONV_WIDTH = 4
CHUNK = 128
ATT_HEADDIM = 64
N_EXPERTS = 8
TOP_K = 2


def _params(sem, vmem_mib):
    return pltpu.CompilerParams(dimension_semantics=sem, vmem_limit_bytes=vmem_mib * MIB)


def _rms_scale(x):
    return lax.rsqrt(jnp.mean(x * x, axis=-1, keepdims=True) + EPS)


def _sigmoid(x):
    return 1.0 / (1.0 + jnp.exp2(x * (-LOG2E)))


def _softplus(x):
    return jnp.maximum(x, 0.0) + jnp.log(1.0 + jnp.exp(-jnp.abs(x)))


def _pack_bf16_pairs(x):
    n = x.shape[1] // 2
    bits = lax.bitcast_convert_type(x.astype(BF16).astype(F32), U32)
    return lax.shift_right_logical(bits[:, :n], jnp.uint32(16)) | (bits[:, n:] & jnp.uint32(0xFFFF0000))


def _unpack_bf16_pairs(w):
    lo = lax.bitcast_convert_type(lax.shift_left(w, jnp.uint32(16)), F32)
    hi = lax.bitcast_convert_type(w & jnp.uint32(0xFFFF0000), F32)
    return jnp.concatenate([lo, hi], axis=1)


def _norm_matmul_kernel(h_ref, nw_ref, *refs):
    n = len(refs) // 2
    x = h_ref[...]
    xn = (x * _rms_scale(x) * nw_ref[...]).astype(BF16)
    for w_ref, o_ref in zip(refs[:n], refs[n:]):
        o_ref[...] = jnp.dot(xn, w_ref[...], preferred_element_type=F32).astype(o_ref.dtype)


def norm_matmul(h, nw, ws, out_dtypes, tm):
    t, d = h.shape
    resident = lambda shape: pl.BlockSpec(shape, lambda i: (0, 0), pipeline_mode=pl.Buffered(1))
    return pl.pallas_call(
        _norm_matmul_kernel,
        grid=(t // tm,),
        in_specs=[pl.BlockSpec((tm, d), lambda i: (i, 0)), resident((1, d))] + [resident(w.shape) for w in ws],
        out_specs=[pl.BlockSpec((tm, w.shape[1]), lambda i: (i, 0)) for w in ws],
        out_shape=[jax.ShapeDtypeStruct((t, w.shape[1]), dt) for w, dt in zip(ws, out_dtypes)],
        compiler_params=_params(("parallel",), 48),
        name="norm_matmul",
    )(h, nw.reshape(1, d), *ws)


def _matmul_residual_kernel(h_ref, y_ref, w_ref, o_ref):
    o_ref[...] = h_ref[...] + jnp.dot(y_ref[...], w_ref[...], preferred_element_type=F32)


def matmul_residual(h, y, w, tm):
    t, n = h.shape
    k = y.shape[1]
    return pl.pallas_call(
        _matmul_residual_kernel,
        grid=(t // tm,),
        in_specs=[
            pl.BlockSpec((tm, n), lambda i: (i, 0)),
            pl.BlockSpec((tm, k), lambda i: (i, 0)),
            pl.BlockSpec((k, n), lambda i: (0, 0)),
        ],
        out_specs=pl.BlockSpec((tm, n), lambda i: (i, 0)),
        out_shape=jax.ShapeDtypeStruct((t, n), F32),
        compiler_params=_params(("parallel",), 48),
        name="matmul_residual",
    )(h, y, w)


def _ple_update(x, p, nw, wg, wp):
    xn = (x * _rms_scale(x) * nw).astype(BF16)
    gate = _sigmoid(jnp.dot(xn, wg, preferred_element_type=F32))
    proj = jnp.dot(p.astype(BF16), wp, preferred_element_type=F32)
    return x + proj * gate


def _ffn_ple_kernel(h_ref, nw_ref, wg_ref, wu_ref, wd_ref, p_ref, pnw_ref, pwg_ref, pwp_ref, o_ref, xn_ref, acc_ref):
    j = pl.program_id(1)

    @pl.when(j == 0)
    def _():
        x = h_ref[...]
        xn_ref[...] = (x * _rms_scale(x) * nw_ref[...]).astype(xn_ref.dtype)
        acc_ref[...] = x

    xn = xn_ref[...]
    g = jnp.dot(xn, wg_ref[...], preferred_element_type=F32)
    u = jnp.dot(xn, wu_ref[...], preferred_element_type=F32)
    a = (g * _sigmoid(g) * u).astype(BF16)
    acc_ref[...] += jnp.dot(a, wd_ref[...], preferred_element_type=F32)

    @pl.when(j == pl.num_programs(1) - 1)
    def _():
        o_ref[...] = _ple_update(acc_ref[...], p_ref[...], pnw_ref[...], pwg_ref[...], pwp_ref[...])


def ffn_ple(h, nw, wg, wu, wd, p, ple_nw, ple_wg, ple_wp, tm, tf):
    t, d = h.shape
    f = wg.shape[1]
    pd = p.shape[1]
    resident = lambda shape: pl.BlockSpec(shape, lambda i, j: (0, 0), pipeline_mode=pl.Buffered(1))
    return pl.pallas_call(
        _ffn_ple_kernel,
        grid=(t // tm, f // tf),
        in_specs=[
            pl.BlockSpec((tm, d), lambda i, j: (i, 0)),
            resident((1, d)),
            pl.BlockSpec((d, tf), lambda i, j: (0, j)),
            pl.BlockSpec((d, tf), lambda i, j: (0, j)),
            pl.BlockSpec((tf, d), lambda i, j: (j, 0)),
            pl.BlockSpec((tm, pd), lambda i, j: (i, 0)),
            resident((1, d)), resident((d, d)), resident((pd, d)),
        ],
        out_specs=pl.BlockSpec((tm, d), lambda i, j: (i, 0)),
        out_shape=jax.ShapeDtypeStruct((t, d), F32),
        scratch_shapes=[pltpu.VMEM((tm, d), BF16), pltpu.VMEM((tm, d), F32)],
        compiler_params=_params(("parallel", "arbitrary"), 52),
        name="ffn_ple",
    )(h, nw.reshape(1, d), wg, wu, wd, p, ple_nw.reshape(1, d), ple_wg, ple_wp)


def _split3(v, n):
    v1 = v.astype(BF16)
    r1 = v - v1.astype(F32)
    v2 = r1.astype(BF16)
    v3 = (r1 - v2.astype(F32)).astype(BF16)
    lane = lax.broadcasted_iota(I32, v.shape, 1)
    return jnp.where(lane < n, v1, jnp.where(lane < 2 * n, v2, v3))


def _ssd_kernel(z_ref, xbc_ref, dt_ref, cw_ref, cb_ref, dtb_ref, alog_ref, dskip_ref, gnw_ref, exp_ref,
                o_ref, cbuf, xact, state, acs_t, y_scr):
    d_inner = z_ref.shape[-1]
    gw = d_inner // SSM_GROUPS
    n_heads = d_inner // SSM_HEADDIM
    q = CHUNK

    tail = cbuf.shape[0] - q

    @pl.when(pl.program_id(1) == 0)
    def _():
        cbuf[0:tail, :] = jnp.zeros((tail, cbuf.shape[1]), BF16)
        state[...] = jnp.zeros(state.shape, F32)

    cbuf[tail:, :] = xbc_ref[0]
    cdim = cbuf.shape[1]
    nshift = CONV_WIDTH - 1
    srow = lax.broadcasted_iota(I32, (nshift * q, tail + q), 0)
    scol = lax.broadcasted_iota(I32, (nshift * q, tail + q), 1)
    shift = jnp.where(scol == srow - (srow // q) * (q - 1) + (tail - nshift), 1.0, 0.0).astype(BF16)
    cw_lanes = 2 * LANES
    for j in range(cdim // cw_lanes):
        sl = slice(j * cw_lanes, (j + 1) * cw_lanes)
        shifted = jnp.dot(shift, cbuf[:, sl], preferred_element_type=F32)
        acc = cb_ref[:, sl] + cbuf[tail:, sl].astype(F32) * cw_ref[nshift:nshift + 1, sl]
        for w in range(nshift):
            acc = acc + shifted[w * q:(w + 1) * q] * cw_ref[w:w + 1, sl]
        xact[:, sl] = acc * _sigmoid(acc)
    cbuf[0:tail, :] = cbuf[q:q + tail, :]

    dt = _softplus(dt_ref[0] + dtb_ref[...])
    a = dt * (-LOG2E * jnp.exp(alog_ref[...]))
    row = lax.broadcasted_iota(I32, (q, q), 0)
    col = lax.broadcasted_iota(I32, (q, q), 1)
    causal = row >= col
    tril = jnp.where(causal, 1.0, 0.0).astype(F32)
    acs = jnp.dot(tril, a, preferred_element_type=F32, precision=HIGHEST)
    acs_t[...] = acs.T
    a_last = acs[q - 1:q, :]
    dt3 = _split3(dt, n_heads)
    dec3 = _split3(jnp.exp2(a_last - acs), n_heads)
    e3 = _split3(jnp.exp2(acs), n_heads)

    for g in range(SSM_GROUPS):
        gl = slice(g * gw, (g + 1) * gw)
        ex = exp_ref[:, gl]
        xs = xact[:, gl]
        b_g = xact[:, d_inner + g * SSM_STATE:d_inner + (g + 1) * SSM_STATE].astype(BF16)
        c_g = xact[:, d_inner + (SSM_GROUPS + g) * SSM_STATE:d_inner + (SSM_GROUPS + g + 1) * SSM_STATE].astype(BF16)
        dt_e = jnp.dot(dt3, ex, preferred_element_type=F32)
        dec_e = jnp.dot(dec3, ex, preferred_element_type=F32)
        e_e = jnp.dot(e3, ex, preferred_element_type=F32)
        xc = xs * dt_e
        xc_b = xc.astype(BF16)
        xdec_b = (xc * dec_e).astype(BF16)
        scores = lax.dot_general(c_g, b_g, (((1,), (1,)), ((), ())), preferred_element_type=F32)
        st = state[g]
        y = jnp.dot(c_g, st.astype(BF16), preferred_element_type=F32) * e_e
        y = y + xs * dskip_ref[:, gl]
        lane = lax.broadcasted_iota(I32, (q, LANES), 1)
        ydiag = []
        for pr in range(gw // LANES):
            ms = []
            for hh in range(2):
                h = g * (gw // SSM_HEADDIM) + 2 * pr + hh
                seg = acs[:, h:h + 1] - acs_t[h:h + 1, :]
                ms.append(scores * jnp.exp2(jnp.where(causal, seg, -jnp.inf)))
            lhs = jnp.concatenate(ms, axis=1).astype(BF16)
            xp = xc_b[:, pr * LANES:(pr + 1) * LANES]
            zero = jnp.zeros_like(xp)
            rhs = jnp.concatenate([jnp.where(lane < SSM_HEADDIM, xp, zero),
                                   jnp.where(lane >= SSM_HEADDIM, xp, zero)], axis=0)
            ydiag.append(jnp.dot(lhs, rhs, preferred_element_type=F32))
        y_scr[:, gl] = y + jnp.concatenate(ydiag, axis=1)
        upd = lax.dot_general(b_g, xdec_b, (((0,), (0,)), ((), ())), preferred_element_type=F32)
        state[g] = st * e_e[q - 1:q, :] + upd

    zz = z_ref[0].astype(F32)
    y = y_scr[...] * (zz * _sigmoid(zz))
    o_ref[0] = (y * _rms_scale(y) * gnw_ref[...]).astype(o_ref.dtype)


def ssd(z, xbc, dt_raw, conv_w, conv_b, dt_bias3, a_log3, dskip_e, gn_w, expand):
    b, l, d_inner = z.shape
    cdim = xbc.shape[-1]
    gw = d_inner // SSM_GROUPS
    nc = l // CHUNK
    full = lambda shape: pl.BlockSpec(shape, lambda i, c: (0,) * len(shape))
    return pl.pallas_call(
        _ssd_kernel,
        grid=(b, nc),
        in_specs=[
            pl.BlockSpec((1, CHUNK, d_inner), lambda i, c: (i, c, 0)),
            pl.BlockSpec((1, CHUNK, cdim), lambda i, c: (i, c, 0)),
            pl.BlockSpec((1, CHUNK, LANES), lambda i, c: (i, c, 0)),
            full((CONV_WIDTH, cdim)), full((1, cdim)), full((1, LANES)), full((1, LANES)),
            full((1, d_inner)), full((1, d_inner)), full((LANES, d_inner)),
        ],
        out_specs=pl.BlockSpec((1, CHUNK, d_inner), lambda i, c: (i, c, 0)),
        out_shape=jax.ShapeDtypeStruct((b, l, d_inner), BF16),
        scratch_shapes=[
            pltpu.VMEM((CHUNK + 16, cdim), BF16),
            pltpu.VMEM((CHUNK, cdim), F32),
            pltpu.VMEM((SSM_GROUPS, SSM_STATE, gw), F32),
            pltpu.VMEM((LANES, CHUNK), F32),
            pltpu.VMEM((CHUNK, d_inner), F32),
        ],
        compiler_params=_params(("parallel", "arbitrary"), 40),
        name="ssd",
    )(z, xbc, dt_raw, conv_w, conv_b, dt_bias3, a_log3, dskip_e, gn_w, expand)


def _cum_kernel(f_ref, bf_ref, pq_ref, pk_ref, eq_ref, ek_ref, *, blk, n_heads):
    l = f_ref.shape[1]
    row = lax.broadcasted_iota(I32, (blk, blk), 0)
    col = lax.broadcasted_iota(I32, (blk, blk), 1)
    tril = jnp.where(row >= col, 1.0, 0.0).astype(F32)
    lane = lax.broadcasted_iota(I32, (blk, LANES), 1)
    carry = jnp.zeros((1, LANES), F32)
    for i in range(l // blk):
        rs = slice(i * blk, (i + 1) * blk)
        x = f_ref[0, rs, :] + bf_ref[...]
        lf = -_softplus(-x)
        cs = jnp.dot(tril, lf, preferred_element_type=F32, precision=HIGHEST) + carry
        carry = cs[blk - 1:blk, :]
        lhs = jnp.where(lane == 3 * n_heads, jnp.ones((), BF16), _split3(cs * LOG2E, n_heads))
        eq_ref[0, rs, :] = jnp.dot(lhs, pq_ref[...], preferred_element_type=F32).astype(BF16)
        ek_ref[0, rs, :] = jnp.dot(lhs, pk_ref[...], preferred_element_type=F32).astype(BF16)


def _ext_placement(n_heads):
    w = n_heads * ATT_HEADDIM
    pq = np.zeros((LANES, w), np.float32)
    pk = np.zeros((LANES, w), np.float32)
    one = 3 * n_heads
    for h in range(n_heads):
        base = (h // 2) * LANES + (ATT_HEADDIM if h % 2 == 0 else 0)
        for piece in range(3):
            pq[piece * n_heads + h, base + piece] = 1.0
            pq[one, base + 3 + piece] = 1.0
            pk[one, base + piece] = 1.0
            pk[piece * n_heads + h, base + 3 + piece] = -1.0
    return jnp.asarray(pq, BF16), jnp.asarray(pk, BF16)


def forget_ext(f, b_f3, n_heads):
    b, l, _ = f.shape
    w = n_heads * ATT_HEADDIM
    pq, pk = _ext_placement(n_heads)
    out = jax.ShapeDtypeStruct((b, l, w), BF16)
    return pl.pallas_call(
        functools.partial(_cum_kernel, blk=256, n_heads=n_heads),
        grid=(b,),
        in_specs=[
            pl.BlockSpec((1, l, LANES), lambda i: (i, 0, 0)),
            pl.BlockSpec((1, LANES), lambda i: (0, 0)),
            pl.BlockSpec((LANES, w), lambda i: (0, 0)),
            pl.BlockSpec((LANES, w), lambda i: (0, 0)),
        ],
        out_specs=[pl.BlockSpec((1, l, w), lambda i: (i, 0, 0)), pl.BlockSpec((1, l, w), lambda i: (i, 0, 0))],
        out_shape=[out, out],
        compiler_params=_params(("parallel",), 40),
        name="forget_ext",
    )(f, b_f3, pq, pk)


def _head_rms(x, lane_lo):
    sq = x * x
    s_lo = jnp.sum(jnp.where(lane_lo, sq, 0.0), axis=-1, keepdims=True)
    s_hi = jnp.sum(jnp.where(lane_lo, 0.0, sq), axis=-1, keepdims=True)
    inv = 1.0 / ATT_HEADDIM
    return jnp.where(lane_lo, lax.rsqrt(s_lo * inv + EPS), lax.rsqrt(s_hi * inv + EPS))


def _attn_kernel(q_ref, k_ref, v_ref, eq_ref, ek_ref, qnw_ref, knw_ref, o_ref, kaug, vaugt, s_scr, p_scr, *, tq):
    l = q_ref.shape[1]
    nblk = l // tq
    hd = ATT_HEADDIM
    lane_lo = lax.broadcasted_iota(I32, (tq, LANES), 1) < hd
    sub_lo = lax.broadcasted_iota(I32, (LANES, tq), 0) < hd
    row = lax.broadcasted_iota(I32, (tq, tq), 0)
    col = lax.broadcasted_iota(I32, (tq, tq), 1)
    visible = row <= col
    eye_q = jnp.where(row == col, 1.0, 0.0).astype(BF16)
    eye_l = eye_q[:LANES, :LANES]
    nt = (((1,), (1,)), ((), ()))
    qscale = qnw_ref[...] * (hd ** -0.5 * LOG2E)

    for i in range(nblk):
        rs = slice(i * tq, (i + 1) * tq)
        kk = k_ref[0, rs, :].astype(F32)
        kn = (kk * _head_rms(kk, lane_lo) * knw_ref[...]).astype(BF16)
        ek = ek_ref[0, rs, :]
        kaug[0, rs, :] = jnp.where(lane_lo, kn, ek)
        kaug[1, rs, :] = jnp.where(lane_lo, ek, kn)
        vt = lax.dot_general(eye_l, v_ref[0, rs, :], nt, preferred_element_type=F32).astype(BF16)
        one = jnp.ones_like(vt)
        vaugt[0, :, rs] = jnp.where(sub_lo, vt, one)
        vaugt[1, :, rs] = jnp.where(sub_lo, one, vt)

    def scores(qi):
        qs = slice(qi * tq, (qi + 1) * tq)
        nk = (qi + 1) * tq
        qq = q_ref[0, qs, :].astype(F32)
        qn = (qq * _head_rms(qq, lane_lo) * qscale).astype(BF16)
        eq = eq_ref[0, qs, :]
        for hh in range(2):
            qa = jnp.where(lane_lo, qn, eq) if hh == 0 else jnp.where(lane_lo, eq, qn)
            qt = lax.dot_general(eye_l, qa, nt, preferred_element_type=F32).astype(BF16)
            s = jnp.dot(kaug[hh, :nk, :], qt, preferred_element_type=F32)
            s_scr[qi % 2, hh, :nk, :] = s

    def softmax_pv(qi):
        qs = slice(qi * tq, (qi + 1) * tq)
        nk = (qi + 1) * tq
        par = qi % 2
        accs = []
        for hh in range(2):
            s_scr[par, hh, nk - tq:nk, :] = jnp.where(visible, s_scr[par, hh, nk - tq:nk, :], -jnp.inf)
            m = jnp.max(s_scr[par, hh, :nk, :], axis=0, keepdims=True)
            p_scr[par, hh, :nk, :] = jnp.exp2(s_scr[par, hh, :nk, :] - m).astype(BF16)
            accs.append(jnp.dot(vaugt[hh, :, :nk], p_scr[par, hh, :nk, :], preferred_element_type=F32))
        a0, a1 = accs
        o_t = jnp.concatenate([a0[:hd] / a0[hd:], a1[hd:] / a1[:hd]], axis=0).astype(BF16)
        o_ref[0, qs, :] = lax.dot_general(eye_q, o_t, nt, preferred_element_type=F32).astype(o_ref.dtype)

    scores(0)
    for qi in range(nblk):
        if qi + 1 < nblk:
            scores(qi + 1)
        softmax_pv(qi)


def fox_attention(qp, kv, ext_q, ext_k, qnw2, knw2, att_w, tq):
    b, l, _ = qp.shape
    npair = att_w // LANES
    nblk = l // tq
    blk = lambda off: pl.BlockSpec((1, l, LANES), lambda i, j: (i, 0, j + off))
    return pl.pallas_call(
        functools.partial(_attn_kernel, tq=tq),
        grid=(b, npair),
        in_specs=[
            blk(0), blk(0), blk(npair), blk(0), blk(0),
            pl.BlockSpec((1, LANES), lambda i, j: (0, 0)),
            pl.BlockSpec((1, LANES), lambda i, j: (0, 0)),
        ],
        out_specs=blk(0),
        out_shape=jax.ShapeDtypeStruct((b, l, att_w), BF16),
        scratch_shapes=[
            pltpu.VMEM((2, l, LANES), BF16),
            pltpu.VMEM((2, LANES, l), BF16),
            pltpu.VMEM((2, 2, l, tq), F32),
            pltpu.VMEM((2, 2, l, tq), BF16),
        ],
        compiler_params=_params(("parallel", "parallel"), 40),
        name="fox_attention",
    )(qp, kv, kv, ext_q, ext_k, qnw2, knw2)


def _router_kernel(h_ref, nw_ref, wr_ref, xpk_ref, topt_ref, sel_ref, top_ref):
    x = h_ref[...]
    u = x * _rms_scale(x) * nw_ref[...]
    xpk_ref[...] = _pack_bf16_pairs(u)
    logits = jnp.dot(u, wr_ref[...], preferred_element_type=F32, precision=HIGHEST)
    lt = logits.T[:N_EXPERTS, :]
    sub = lax.broadcasted_iota(I32, lt.shape, 0)
    neg = jnp.float32(-jnp.inf)
    m1 = jnp.max(lt, axis=0, keepdims=True)
    i1 = jnp.min(jnp.where(lt == m1, sub, N_EXPERTS), axis=0, keepdims=True)
    rest = jnp.where(sub == i1, neg, lt)
    m2 = jnp.max(rest, axis=0, keepdims=True)
    i2 = jnp.min(jnp.where(rest == m2, sub, N_EXPERTS), axis=0, keepdims=True)
    e2 = jnp.exp(m2 - m1)
    g1 = 1.0 / (1.0 + e2)
    g2 = e2 / (1.0 + e2)
    sel_ref[...] = jnp.where((sub == i1) | (sub == i2), 1.0, 0.0).astype(F32)
    topt = jnp.where(sub == 0, i1.astype(F32),
                     jnp.where(sub == 1, i2.astype(F32), jnp.where(sub == 2, g1, jnp.where(sub == 3, g2, 0.0))))
    topt_ref[...] = topt
    pad = jnp.zeros((LANES - N_EXPERTS, topt.shape[1]), F32)
    top_ref[...] = jnp.concatenate([topt, pad], axis=0).T


def router(h, nw, wr_pad, tm):
    t, d = h.shape
    return pl.pallas_call(
        _router_kernel,
        grid=(t // tm,),
        in_specs=[
            pl.BlockSpec((tm, d), lambda i: (i, 0)),
            pl.BlockSpec((1, d), lambda i: (0, 0)),
            pl.BlockSpec((d, LANES), lambda i: (0, 0)),
        ],
        out_specs=[
            pl.BlockSpec((tm, d // 2), lambda i: (i, 0)),
            pl.BlockSpec((N_EXPERTS, tm), lambda i: (0, i)),
            pl.BlockSpec((N_EXPERTS, tm), lambda i: (0, i)),
            pl.BlockSpec((tm, LANES), lambda i: (i, 0)),
        ],
        out_shape=[
            jax.ShapeDtypeStruct((t, d // 2), U32),
            jax.ShapeDtypeStruct((N_EXPERTS, t), F32),
            jax.ShapeDtypeStruct((N_EXPERTS, t), F32),
            jax.ShapeDtypeStruct((t, LANES), F32),
        ],
        compiler_params=_params(("parallel",), 40),
        name="router",
    )(h, nw.reshape(1, d), wr_pad)


def _rank_kernel(sel_ref, rank_ref, cnt_ref, carry):
    @pl.when(pl.program_id(0) == 0)
    def _():
        carry[...] = jnp.zeros(carry.shape, F32)

    s = sel_ref[...]
    tb = s.shape[1]
    row = lax.broadcasted_iota(I32, (tb, tb), 0)
    col = lax.broadcasted_iota(I32, (tb, tb), 1)
    before = jnp.where(row < col, 1.0, 0.0).astype(BF16)
    c = carry[...]
    rank_ref[...] = jnp.dot(s.astype(BF16), before, preferred_element_type=F32) + c[:, :1]
    c = c + jnp.sum(s, axis=1, keepdims=True)
    carry[...] = c
    cnt_ref[...] = c


def moe_rank(sel, tb):
    ne, t = sel.shape
    return pl.pallas_call(
        _rank_kernel,
        grid=(t // tb,),
        in_specs=[pl.BlockSpec((ne, tb), lambda i: (0, i))],
        out_specs=[pl.BlockSpec((ne, tb), lambda i: (0, i)), pl.BlockSpec((ne, LANES), lambda i: (0, 0))],
        out_shape=[jax.ShapeDtypeStruct((ne, t), F32), jax.ShapeDtypeStruct((ne, LANES), F32)],
        scratch_shapes=[pltpu.VMEM((ne, LANES), F32)],
        compiler_params=_params(("arbitrary",), 32),
        name="moe_rank",
    )(sel)


def _pos_kernel(topt_ref, rank_ref, off_ref, pos_ref):
    posv = rank_ref[...] + off_ref[...][:, :1]
    sub = lax.broadcasted_iota(I32, posv.shape, 0).astype(F32)
    for k in range(TOP_K):
        ek = topt_ref[k:k + 1, :]
        pos_ref[0, k:k + 1, :] = jnp.sum(jnp.where(sub == ek, posv, 0.0), axis=0, keepdims=True).astype(I32)


def moe_pos(topt, rank, off, tb):
    ne, t = rank.shape
    return pl.pallas_call(
        _pos_kernel,
        grid=(t // tb,),
        in_specs=[
            pl.BlockSpec((ne, tb), lambda i: (0, i)),
            pl.BlockSpec((ne, tb), lambda i: (0, i)),
            pl.BlockSpec((ne, LANES), lambda i: (0, 0)),
        ],
        out_specs=pl.BlockSpec((1, TOP_K, tb), lambda i: (i, 0, 0)),
        out_shape=jax.ShapeDtypeStruct((t // tb, TOP_K, tb), I32),
        compiler_params=_params(("parallel",), 32),
        name="moe_pos",
    )(topt, rank, off)


def _pos_copy(pos_hbm, pos_smem, psem, tile, slot, n):
    return pltpu.make_async_copy(pos_hbm.at[pl.ds(pl.multiple_of(tile * n, n), n)],
                                 pos_smem.at[pl.ds(pl.multiple_of(slot * n, n), n)], psem.at[slot])


def _dispatch_kernel(x_ref, pos_hbm, init_hbm, xs_hbm, pos_smem, psem, sem, *, tb):
    del init_hbm
    i = pl.program_id(0)
    nsteps = pl.num_programs(0)
    slot = i % 2
    pos_copy = functools.partial(_pos_copy, pos_hbm, pos_smem, psem, n=TOP_K * tb)

    @pl.when(i == 0)
    def _():
        pos_copy(0, 0).start()

    @pl.when(i + 1 < nsteps)
    def _():
        pos_copy(i + 1, 1 - slot).start()

    pos_copy(i, slot).wait()

    def row_copy(r, k):
        return pltpu.make_async_copy(x_ref.at[pl.ds(r, 1)],
                                     xs_hbm.at[pl.ds(pos_smem[slot * (TOP_K * tb) + k * tb + r], 1)], sem)

    def issue(r, _):
        for k in range(TOP_K):
            row_copy(r, k).start()
        return 0

    lax.fori_loop(0, tb, issue, 0, unroll=8)

    def drain(r, _):
        for k in range(TOP_K):
            row_copy(r, k).wait()
        return 0

    lax.fori_loop(0, tb, drain, 0, unroll=8)


def moe_dispatch(xpk, pos_flat, rows, tb):
    t, dw = xpk.shape
    init = jnp.zeros((rows, dw), U32)
    return pl.pallas_call(
        functools.partial(_dispatch_kernel, tb=tb),
        grid=(t // tb,),
        in_specs=[
            pl.BlockSpec((tb, dw), lambda i: (i, 0)),
            pl.BlockSpec(memory_space=pl.ANY),
            pl.BlockSpec(memory_space=pl.ANY),
        ],
        out_specs=pl.BlockSpec(memory_space=pl.ANY),
        out_shape=jax.ShapeDtypeStruct((rows, dw), U32),
        scratch_shapes=[pltpu.SMEM((2 * TOP_K * tb,), I32), pltpu.SemaphoreType.DMA((2,)), pltpu.SemaphoreType.DMA],
        input_output_aliases={2: 0},
        compiler_params=_params(("arbitrary",), 32),
        name="moe_dispatch",
    )(xpk, pos_flat, init)


def _expert_ffn_kernel(te_ref, nv_ref, x_ref, wg_ref, wu_ref, wd_ref, o_ref, xn_ref, acc_ref):
    del te_ref
    i = pl.program_id(0)
    j = pl.program_id(1)

    @pl.when(i < nv_ref[0])
    def _():
        @pl.when(j == 0)
        def _():
            xn_ref[...] = _unpack_bf16_pairs(x_ref[...]).astype(BF16)
            acc_ref[...] = jnp.zeros(acc_ref.shape, F32)

        xn = xn_ref[...]
        g = jnp.dot(xn, wg_ref[0].astype(BF16), preferred_element_type=F32)
        u = jnp.dot(xn, wu_ref[0].astype(BF16), preferred_element_type=F32)
        a = (g * _sigmoid(g) * u).astype(BF16)
        acc_ref[...] += jnp.dot(a, wd_ref[0].astype(BF16), preferred_element_type=F32)

        @pl.when(j == pl.num_programs(1) - 1)
        def _():
            o_ref[...] = _pack_bf16_pairs(acc_ref[...])

    @pl.when((i >= nv_ref[0]) & (j == pl.num_programs(1) - 1))
    def _():
        o_ref[...] = jnp.zeros(o_ref.shape, o_ref.dtype)


def expert_ffn(xs, tile_expert, n_valid, wg, wu, wd, tm, tf):
    rows, dw = xs.shape
    ne, d, f = wg.shape
    nf = f // tf
    row_blk = lambda i, j, te, nv: (jnp.maximum(jnp.minimum(i, nv[0] - 1), 0), 0)
    out_blk = lambda i, j, te, nv: (i, 0)
    col_j = lambda i, j, nv: jnp.where(i < nv[0], j, nf - 1)
    return pl.pallas_call(
        _expert_ffn_kernel,
        grid_spec=pltpu.PrefetchScalarGridSpec(
            num_scalar_prefetch=2,
            grid=(rows // tm, nf),
            in_specs=[
                pl.BlockSpec((tm, dw), row_blk),
                pl.BlockSpec((1, d, tf), lambda i, j, te, nv: (te[i], 0, col_j(i, j, nv))),
                pl.BlockSpec((1, d, tf), lambda i, j, te, nv: (te[i], 0, col_j(i, j, nv))),
                pl.BlockSpec((1, tf, d), lambda i, j, te, nv: (te[i], col_j(i, j, nv), 0)),
            ],
            out_specs=pl.BlockSpec((tm, dw), out_blk),
            scratch_shapes=[pltpu.VMEM((tm, d), BF16), pltpu.VMEM((tm, d), F32)],
        ),
        out_shape=jax.ShapeDtypeStruct((rows, dw), U32),
        compiler_params=_params(("arbitrary", "arbitrary"), 48),
        name="expert_ffn",
    )(tile_expert, n_valid, xs, wg, wu, wd)


SC_CORES = 2
SC_SUBCORES = 16
SC_INDEX_WINDOW = 128
SC_GATHER_ROWS = 64


def sc_row_gather(data, idx):
    n = idx.shape[0]
    d = data.shape[1]
    n_workers = SC_CORES * SC_SUBCORES
    per_worker = n // n_workers
    win, half = SC_INDEX_WINDOW, SC_GATHER_ROWS
    assert n % n_workers == 0 and per_worker % win == 0 and win == 2 * half
    mesh = plsc.VectorSubcoreMesh(core_axis_name="c", subcore_axis_name="s")

    @functools.partial(
        pl.kernel, out_type=jax.ShapeDtypeStruct((n, d), data.dtype), mesh=mesh,
        scratch_types=[pltpu.VMEM((win,), I32), pltpu.VMEM((half, d), data.dtype), pltpu.VMEM((half, d), data.dtype),
                       pltpu.SemaphoreType.DMA, pltpu.SemaphoreType.DMA])
    def gather_kernel(x_hbm, i_hbm, o_hbm, idx_v, rows_a, rows_b, sem_a, sem_b):
        wid = lax.axis_index("s") * SC_CORES + lax.axis_index("c")
        base = wid * per_worker

        @pl.loop(0, per_worker // win)
        def _(c):
            off = base + c * win
            pltpu.sync_copy(i_hbm.at[pl.ds(off, win)], idx_v)
            cp_a = pltpu.async_copy(x_hbm.at[idx_v.at[pl.ds(0, half)]], rows_a, sem_a)
            cp_b = pltpu.async_copy(x_hbm.at[idx_v.at[pl.ds(half, half)]], rows_b, sem_b)
            cp_a.wait()
            pltpu.sync_copy(rows_a, o_hbm.at[pl.ds(off, half)])
            cp_b.wait()
            pltpu.sync_copy(rows_b, o_hbm.at[pl.ds(off + half, half)])

    return gather_kernel(data, idx)


def _combine_dense_ple_kernel(h_ref, top_ref, y1_ref, y2_ref, p_ref, nw_ref, wg_ref, wp_ref, o_ref):
    top = top_ref[...]
    x = h_ref[...]
    for k, y_ref in enumerate((y1_ref, y2_ref)):
        y = _unpack_bf16_pairs(lax.bitcast_convert_type(y_ref[...], U32))
        x = x + top[:, TOP_K + k:TOP_K + k + 1] * y
    o_ref[...] = _ple_update(x, p_ref[...], nw_ref[...], wg_ref[...], wp_ref[...])


def moe_combine_dense_ple(h, top, yg, p, nw, wg, wp, tm):
    t, d = h.shape
    dw = yg.shape[1]
    pd = p.shape[1]
    nt = t // tm
    resident = lambda shape: pl.BlockSpec(shape, lambda i: (0, 0), pipeline_mode=pl.Buffered(1))
    return pl.pallas_call(
        _combine_dense_ple_kernel,
        grid=(nt,),
        in_specs=[
            pl.BlockSpec((tm, d), lambda i: (i, 0)),
            pl.BlockSpec((tm, LANES), lambda i: (i, 0)),
            pl.BlockSpec((tm, dw), lambda i: (i, 0)),
            pl.BlockSpec((tm, dw), lambda i: (i + nt, 0)),
            pl.BlockSpec((tm, pd), lambda i: (i, 0)),
            resident((1, d)), resident((d, d)), resident((pd, d)),
        ],
        out_specs=pl.BlockSpec((tm, d), lambda i: (i, 0)),
        out_shape=jax.ShapeDtypeStruct((t, d), F32),
        compiler_params=_params(("parallel",), 48),
        name="moe_combine_dense_ple",
    )(h, top, yg, yg, p, nw.reshape(1, d), wg, wp)


def _combine_ple_kernel(h_ref, top_ref, p_ref, nw_ref, wg_ref, wp_ref, pos_hbm, ys_hbm, o_ref,
                        ybuf, pos_smem, psem, sem, *, tb):
    i = pl.program_id(0)
    nsteps = pl.num_programs(0)
    slot = i % 2
    pos_copy = functools.partial(_pos_copy, pos_hbm, pos_smem, psem, n=TOP_K * tb)

    def row_copy(s, r, k):
        return pltpu.make_async_copy(ys_hbm.at[pl.ds(pos_smem[s * (TOP_K * tb) + k * tb + r], 1)],
                                     ybuf.at[s, k, pl.ds(r, 1)], sem.at[s])

    def gather_start(s):
        def issue(r, _):
            for k in range(TOP_K):
                row_copy(s, r, k).start()
            return 0
        lax.fori_loop(0, tb, issue, 0, unroll=8)

    def gather_wait(s):
        def drain(r, _):
            for k in range(TOP_K):
                row_copy(s, r, k).wait()
            return 0
        lax.fori_loop(0, tb, drain, 0, unroll=8)

    @pl.when(i == 0)
    def _():
        pos_copy(0, 0).start()
        pos_copy(0, 0).wait()
        gather_start(0)

        @pl.when(nsteps > 1)
        def _():
            pos_copy(1, 1).start()

    @pl.when(i + 1 < nsteps)
    def _():
        pos_copy(i + 1, 1 - slot).wait()
        gather_start(1 - slot)

    gather_wait(slot)

    @pl.when(i + 2 < nsteps)
    def _():
        pos_copy(i + 2, slot).start()

    top = top_ref[...]
    x = h_ref[...]
    for k in range(TOP_K):
        x = x + top[:, TOP_K + k:TOP_K + k + 1] * _unpack_bf16_pairs(ybuf[slot, k])
    o_ref[...] = _ple_update(x, p_ref[...], nw_ref[...], wg_ref[...], wp_ref[...])


def moe_combine_ple(h, top, pos_flat, ys, p, nw, wg, wp, tb):
    t, d = h.shape
    dw = ys.shape[1]
    pd = p.shape[1]
    resident = lambda shape: pl.BlockSpec(shape, lambda i: (0, 0), pipeline_mode=pl.Buffered(1))
    return pl.pallas_call(
        functools.partial(_combine_ple_kernel, tb=tb),
        grid=(t // tb,),
        in_specs=[
            pl.BlockSpec((tb, d), lambda i: (i, 0)),
            pl.BlockSpec((tb, LANES), lambda i: (i, 0)),
            pl.BlockSpec((tb, pd), lambda i: (i, 0)),
            resident((1, d)), resident((d, d)), resident((pd, d)),
            pl.BlockSpec(memory_space=pl.ANY),
            pl.BlockSpec(memory_space=pl.ANY),
        ],
        out_specs=pl.BlockSpec((tb, d), lambda i: (i, 0)),
        out_shape=jax.ShapeDtypeStruct((t, d), F32),
        scratch_shapes=[pltpu.VMEM((2, TOP_K, tb, dw), U32), pltpu.SMEM((2 * TOP_K * tb,), I32),
                        pltpu.SemaphoreType.DMA((2,)), pltpu.SemaphoreType.DMA((2,))],
        compiler_params=_params(("arbitrary",), 40),
        name="moe_combine_ple",
    )(h, top, p, nw.reshape(1, d), wg, wp, pos_flat, ys)


def moe_ple(h, nw, w_router, wg, wu, wd, p, ple_nw, ple_wg, ple_wp, tm_route, tm_expert, tb, tf):
    t, d = h.shape
    xpk, topt, sel, top = router(h, nw, _pad_cols(w_router, LANES), tm_route)
    rank, cnt = moe_rank(sel, tb)
    counts = cnt[:, 0].astype(I32)
    padded = ((counts + tm_expert - 1) // tm_expert) * tm_expert
    ends = jnp.cumsum(padded)
    n_tiles = (TOP_K * t) // tm_expert + N_EXPERTS
    n_valid = (ends[-1] // tm_expert).astype(I32)
    tile_start = jnp.arange(n_tiles, dtype=I32) * tm_expert
    first_row = jnp.minimum(tile_start, ends[-1] - 1)
    tile_expert = jnp.minimum(jnp.sum(ends[None, :] <= first_row[:, None], axis=1), N_EXPERTS - 1).astype(I32)
    off = jnp.broadcast_to((ends - padded).astype(F32)[:, None], (N_EXPERTS, LANES))
    pos = moe_pos(topt, rank, off, tb)
    xs = moe_dispatch(xpk, pos.reshape(-1), n_tiles * tm_expert, tb)
    ys = expert_ffn(xs, tile_expert, n_valid.reshape(1), wg, wu, wd, tm_expert, tf)
    pos_slot_major = jnp.transpose(pos, (1, 0, 2)).reshape(-1)
    yg = sc_row_gather(lax.bitcast_convert_type(ys, I32), pos_slot_major)
    return moe_combine_dense_ple(h, top, yg, p, ple_nw, ple_wg, ple_wp, tm_route)


def _pad_cols(w, n):
    return jnp.pad(w, ((0, 0), (0, n - w.shape[1])))


def _tile3(v, n):
    return jnp.pad(jnp.tile(v, 3), (0, LANES - 3 * n)).reshape(1, LANES)


def kernel(x, p, ssm_norm_w, ssm_w_in, ssm_conv_w, ssm_conv_b, ssm_dt_bias, ssm_a_log, ssm_d, ssm_gn_w, ssm_w_out, kv_norm_w, w_kv, b_f, k_norm_w, att_norm_w, att_w_q, q_norm_w, att_w_o, ffn_norm_w, ffn_w_gate, ffn_w_up, ffn_w_down, moe_norm_w, moe_w_router, moe_w_gate, moe_w_up, moe_w_down, ple_norm_w, ple_w_gate, ple_w_proj):
    b, l, d = x.shape
    t = b * l
    n_a = ssm_norm_w.shape[0]
    depth = ple_norm_w.shape[0]
    d_inner = ssm_gn_w.shape[1]
    n_heads = ssm_dt_bias.shape[1]
    cdim = ssm_conv_w.shape[2]
    att_w = att_w_q.shape[2]
    n_att_heads = att_w // ATT_HEADDIM
    tm = min(1024, t)
    tq = min(256, l)

    h = x.reshape(t, d)
    r = jnp.arange(LANES)[:, None]
    c = jnp.arange(d_inner)[None, :]
    expand = ((r < 3 * n_heads) & ((r % n_heads) == (c // SSM_HEADDIM))).astype(BF16)

    kv = ext_q = ext_k = None
    for i in range(depth):
        if i < n_a:
            w_in = ssm_w_in[i].astype(BF16)
            w_dt = w_in[:, d_inner + cdim:]
            w_dt3 = _pad_cols(jnp.concatenate([w_dt, w_dt, w_dt], axis=1), LANES)
            z, xbc, dt_raw = norm_matmul(h, ssm_norm_w[i], [w_in[:, :d_inner], w_in[:, d_inner:d_inner + cdim], w_dt3],
                                         [BF16, BF16, F32], tm // 2)
            y = ssd(z.reshape(b, l, d_inner), xbc.reshape(b, l, cdim), dt_raw.reshape(b, l, LANES),
                    ssm_conv_w[i], ssm_conv_b[i].reshape(1, cdim), _tile3(ssm_dt_bias[i], n_heads),
                    _tile3(ssm_a_log[i], n_heads), jnp.repeat(ssm_d[i], SSM_HEADDIM).reshape(1, d_inner),
                    ssm_gn_w[i].reshape(1, d_inner), expand)
            h = matmul_residual(h, y.reshape(t, d_inner), ssm_w_out[i].astype(BF16), tm)
        else:
            j = i - n_a
            if i == n_a:
                wkv = w_kv.astype(BF16)
                w_f = wkv[:, 2 * att_w:]
                w_f3 = _pad_cols(jnp.concatenate([w_f, w_f, w_f], axis=1), LANES)
                kv, f = norm_matmul(h, kv_norm_w, [wkv[:, :2 * att_w], w_f3], [BF16, F32], tm)
                kv = kv.reshape(b, l, 2 * att_w)
                ext_q, ext_k = forget_ext(f.reshape(b, l, LANES), _tile3(b_f, n_att_heads), n_att_heads)
            qp, = norm_matmul(h, att_norm_w[j], [att_w_q[j].astype(BF16)], [BF16], tm)
            qp = qp.reshape(b, l, att_w)
            qnw2 = jnp.tile(q_norm_w[j], 2).reshape(1, LANES)
            knw2 = jnp.tile(k_norm_w, 2).reshape(1, LANES)
            o = fox_attention(qp, kv, ext_q, ext_k, qnw2, knw2, att_w, tq)
            h = matmul_residual(h, o.reshape(t, att_w), att_w_o[j].astype(BF16), tm)
        p_i = p[i].reshape(t, p.shape[-1])
        ple_wg, ple_wp = ple_w_gate[i].astype(BF16), ple_w_proj[i].astype(BF16)
        if i % 2 == 0:
            dd = i // 2
            h = ffn_ple(h, ffn_norm_w[dd], ffn_w_gate[dd].astype(BF16), ffn_w_up[dd].astype(BF16),
                        ffn_w_down[dd].astype(BF16), p_i, ple_norm_w[i], ple_wg, ple_wp, tm, 512)
        else:
            m = i // 2
            h = moe_ple(h, moe_norm_w[m], moe_w_router[m], moe_w_gate[m], moe_w_up[m], moe_w_down[m], p_i,
                        ple_norm_w[i], ple_wg, ple_wp, tm, tm, min(512, t), 512)
    return h.reshape(b, l, d)
```

```python
import functools
import math

import numpy as np
import jax
import jax.numpy as jnp
from jax import lax
from jax.experimental import pallas as pl
from jax.experimental.pallas import tpu as pltpu
from jax.experimental.pallas import tpu_sc as plsc

F32 = jnp.float32
BF16 = jnp.bfloat16
I32 = jnp.int32
U32 = jnp.uint32
EPS = 1e-6
HIGHEST = lax.Precision.HIGHEST
LOG2E = math.log2(math.e)

LANES = 128
SUBLANES = 8
MIB = 1024 * 1024

SSM_HEADDIM = 64
SSM_GROUPS = 4
SSM_STATE = 128
CONV_WIDTH = 4
CHUNK = 128
ATT_HEADDIM = 64
N_EXPERTS = 8
TOP_K = 2


def _params(sem, vmem_mib):
    return pltpu.CompilerParams(dimension_semantics=sem, vmem_limit_bytes=vmem_mib * MIB)


def _rms_scale(x):
    return lax.rsqrt(jnp.mean(x * x, axis=-1, keepdims=True) + EPS)


def _sigmoid(x):
    return 1.0 / (1.0 + jnp.exp2(x * (-LOG2E)))


def _softplus(x):
    return jnp.maximum(x, 0.0) + jnp.log(1.0 + jnp.exp(-jnp.abs(x)))


def _pack_bf16_pairs(x):
    n = x.shape[1] // 2
    bits = lax.bitcast_convert_type(x.astype(BF16).astype(F32), U32)
    return lax.shift_right_logical(bits[:, :n], jnp.uint32(16)) | (bits[:, n:] & jnp.uint32(0xFFFF0000))


def _unpack_bf16_pairs(w):
    lo = lax.bitcast_convert_type(lax.shift_left(w, jnp.uint32(16)), F32)
    hi = lax.bitcast_convert_type(w & jnp.uint32(0xFFFF0000), F32)
    return jnp.concatenate([lo, hi], axis=1)


def _norm_matmul_kernel(h_ref, nw_ref, *refs):
    n = len(refs) // 2
    x = h_ref[...]
    xn = (x * _rms_scale(x) * nw_ref[...]).astype(BF16)
    for w_ref, o_ref in zip(refs[:n], refs[n:]):
        o_ref[...] = jnp.dot(xn, w_ref[...], preferred_element_type=F32).astype(o_ref.dtype)


def norm_matmul(h, nw, ws, out_dtypes, tm):
    t, d = h.shape
    resident = lambda shape: pl.BlockSpec(shape, lambda i: (0, 0), pipeline_mode=pl.Buffered(1))
    return pl.pallas_call(
        _norm_matmul_kernel,
        grid=(t // tm,),
        in_specs=[pl.BlockSpec((tm, d), lambda i: (i, 0)), resident((1, d))] + [resident(w.shape) for w in ws],
        out_specs=[pl.BlockSpec((tm, w.shape[1]), lambda i: (i, 0)) for w in ws],
        out_shape=[jax.ShapeDtypeStruct((t, w.shape[1]), dt) for w, dt in zip(ws, out_dtypes)],
        compiler_params=_params(("parallel",), 48),
        name="norm_matmul",
    )(h, nw.reshape(1, d), *ws)


def _matmul_residual_kernel(h_ref, y_ref, w_ref, o_ref):
    o_ref[...] = h_ref[...] + jnp.dot(y_ref[...], w_ref[...], preferred_element_type=F32)


def matmul_residual(h, y, w, tm):
    t, n = h.shape
    k = y.shape[1]
    return pl.pallas_call(
        _matmul_residual_kernel,
        grid=(t // tm,),
        in_specs=[
            pl.BlockSpec((tm, n), lambda i: (i, 0)),
            pl.BlockSpec((tm, k), lambda i: (i, 0)),
            pl.BlockSpec((k, n), lambda i: (0, 0)),
        ],
        out_specs=pl.BlockSpec((tm, n), lambda i: (i, 0)),
        out_shape=jax.ShapeDtypeStruct((t, n), F32),
        compiler_params=_params(("parallel",), 48),
        name="matmul_residual",
    )(h, y, w)


def _ple_update(x, p, nw, wg, wp):
    xn = (x * _rms_scale(x) * nw).astype(BF16)
    gate = _sigmoid(jnp.dot(xn, wg, preferred_element_type=F32))
    proj = jnp.dot(p.astype(BF16), wp, preferred_element_type=F32)
    return x + proj * gate


def _ffn_ple_kernel(h_ref, nw_ref, wg_ref, wu_ref, wd_ref, p_ref, pnw_ref, pwg_ref, pwp_ref, o_ref, xn_ref, acc_ref):
    j = pl.program_id(1)

    @pl.when(j == 0)
    def _():
        x = h_ref[...]
        xn_ref[...] = (x * _rms_scale(x) * nw_ref[...]).astype(xn_ref.dtype)
        acc_ref[...] = x

    xn = xn_ref[...]
    g = jnp.dot(xn, wg_ref[...], preferred_element_type=F32)
    u = jnp.dot(xn, wu_ref[...], preferred_element_type=F32)
    a = (g * _sigmoid(g) * u).astype(BF16)
    acc_ref[...] += jnp.dot(a, wd_ref[...], preferred_element_type=F32)

    @pl.when(j == pl.num_programs(1) - 1)
    def _():
        o_ref[...] = _ple_update(acc_ref[...], p_ref[...], pnw_ref[...], pwg_ref[...], pwp_ref[...])


def ffn_ple(h, nw, wg, wu, wd, p, layer, ple_nw, ple_wg, ple_wp, tm, tf):
    t, d = h.shape
    f = wg.shape[1]
    pd = p.shape[2]
    resident = lambda shape: pl.BlockSpec(shape, lambda i, j: (0, 0), pipeline_mode=pl.Buffered(1))
    return pl.pallas_call(
        _ffn_ple_kernel,
        grid=(t // tm, f // tf),
        in_specs=[
            pl.BlockSpec((tm, d), lambda i, j: (i, 0)),
            resident((1, d)),
            pl.BlockSpec((d, tf), lambda i, j: (0, j)),
            pl.BlockSpec((d, tf), lambda i, j: (0, j)),
            pl.BlockSpec((tf, d), lambda i, j: (j, 0)),
            pl.BlockSpec((None, tm, pd), lambda i, j: (layer, i, 0)),
            resident((1, d)), resident((d, d)), resident((pd, d)),
        ],
        out_specs=pl.BlockSpec((tm, d), lambda i, j: (i, 0)),
        out_shape=jax.ShapeDtypeStruct((t, d), F32),
        scratch_shapes=[pltpu.VMEM((tm, d), BF16), pltpu.VMEM((tm, d), F32)],
        compiler_params=_params(("parallel", "arbitrary"), 52),
        name="ffn_ple",
    )(h, nw.reshape(1, d), wg, wu, wd, p, ple_nw.reshape(1, d), ple_wg, ple_wp)


def _split3(v, n):
    v1 = v.astype(BF16)
    r1 = v - v1.astype(F32)
    v2 = r1.astype(BF16)
    v3 = (r1 - v2.astype(F32)).astype(BF16)
    lane = lax.broadcasted_iota(I32, v.shape, 1)
    return jnp.where(lane < n, v1, jnp.where(lane < 2 * n, v2, v3))


def _ssd_kernel(z_ref, xbc_ref, dt_ref, cw_ref, cb_ref, dtb_ref, alog_ref, dskip_ref, gnw_ref, exp_ref,
                o_ref, cbuf, xact, state, acs_t, y_scr):
    d_inner = z_ref.shape[-1]
    gw = d_inner // SSM_GROUPS
    n_heads = d_inner // SSM_HEADDIM
    q = CHUNK

    tail = cbuf.shape[0] - q

    @pl.when(pl.program_id(1) == 0)
    def _():
        cbuf[0:tail, :] = jnp.zeros((tail, cbuf.shape[1]), BF16)
        state[...] = jnp.zeros(state.shape, F32)

    cbuf[tail:, :] = xbc_ref[0]
    cdim = cbuf.shape[1]
    nshift = CONV_WIDTH - 1
    srow = lax.broadcasted_iota(I32, (nshift * q, tail + q), 0)
    scol = lax.broadcasted_iota(I32, (nshift * q, tail + q), 1)
    shift = jnp.where(scol == srow - (srow // q) * (q - 1) + (tail - nshift), 1.0, 0.0).astype(BF16)
    cw_lanes = 2 * LANES
    for j in range(cdim // cw_lanes):
        sl = slice(j * cw_lanes, (j + 1) * cw_lanes)
        shifted = jnp.dot(shift, cbuf[:, sl], preferred_element_type=F32)
        acc = cb_ref[:, sl] + cbuf[tail:, sl].astype(F32) * cw_ref[nshift:nshift + 1, sl]
        for w in range(nshift):
            acc = acc + shifted[w * q:(w + 1) * q] * cw_ref[w:w + 1, sl]
        xact[:, sl] = acc * _sigmoid(acc)
    cbuf[0:tail, :] = cbuf[q:q + tail, :]

    dt = _softplus(dt_ref[0] + dtb_ref[...])
    a = dt * (-LOG2E * jnp.exp(alog_ref[...]))
    row = lax.broadcasted_iota(I32, (q, q), 0)
    col = lax.broadcasted_iota(I32, (q, q), 1)
    causal = row >= col
    tril = jnp.where(causal, 1.0, 0.0).astype(F32)
    acs = jnp.dot(tril, a, preferred_element_type=F32, precision=HIGHEST)
    acs_t[...] = acs.T
    a_last = acs[q - 1:q, :]
    dt3 = _split3(dt, n_heads)
    dec3 = _split3(jnp.exp2(a_last - acs), n_heads)
    e3 = _split3(jnp.exp2(acs), n_heads)

    for g in range(SSM_GROUPS):
        gl = slice(g * gw, (g + 1) * gw)
        ex = exp_ref[:, gl]
        xs = xact[:, gl]
        b_g = xact[:, d_inner + g * SSM_STATE:d_inner + (g + 1) * SSM_STATE].astype(BF16)
        c_g = xact[:, d_inner + (SSM_GROUPS + g) * SSM_STATE:d_inner + (SSM_GROUPS + g + 1) * SSM_STATE].astype(BF16)
        dt_e = jnp.dot(dt3, ex, preferred_element_type=F32)
        dec_e = jnp.dot(dec3, ex, preferred_element_type=F32)
        e_e = jnp.dot(e3, ex, preferred_element_type=F32)
        xc = xs * dt_e
        xc_b = xc.astype(BF16)
        xdec_b = (xc * dec_e).astype(BF16)
        scores = lax.dot_general(c_g, b_g, (((1,), (1,)), ((), ())), preferred_element_type=F32)
        st = state[g]
        y = jnp.dot(c_g, st.astype(BF16), preferred_element_type=F32) * e_e
        y = y + xs * dskip_ref[:, gl]
        lane = lax.broadcasted_iota(I32, (q, LANES), 1)
        ydiag = []
        for pr in range(gw // LANES):
            ms = []
            for hh in range(2):
                h = g * (gw // SSM_HEADDIM) + 2 * pr + hh
                seg = acs[:, h:h + 1] - acs_t[h:h + 1, :]
                ms.append(scores * jnp.exp2(jnp.where(causal, seg, -jnp.inf)))
            lhs = jnp.concatenate(ms, axis=1).astype(BF16)
            xp = xc_b[:, pr * LANES:(pr + 1) * LANES]
            zero = jnp.zeros_like(xp)
            rhs = jnp.concatenate([jnp.where(lane < SSM_HEADDIM, xp, zero),
                                   jnp.where(lane >= SSM_HEADDIM, xp, zero)], axis=0)
            ydiag.append(jnp.dot(lhs, rhs, preferred_element_type=F32))
        y_scr[:, gl] = y + jnp.concatenate(ydiag, axis=1)
        upd = lax.dot_general(b_g, xdec_b, (((0,), (0,)), ((), ())), preferred_element_type=F32)
        state[g] = st * e_e[q - 1:q, :] + upd

    zz = z_ref[0].astype(F32)
    y = y_scr[...] * (zz * _sigmoid(zz))
    o_ref[0] = (y * _rms_scale(y) * gnw_ref[...]).astype(o_ref.dtype)


def ssd(z, xbc, dt_raw, conv_w, conv_b, dt_bias3, a_log3, dskip_e, gn_w, expand):
    b, l, d_inner = z.shape
    cdim = xbc.shape[-1]
    gw = d_inner // SSM_GROUPS
    nc = l // CHUNK
    full = lambda shape: pl.BlockSpec(shape, lambda i, c: (0,) * len(shape))
    return pl.pallas_call(
        _ssd_kernel,
        grid=(b, nc),
        in_specs=[
            pl.BlockSpec((1, CHUNK, d_inner), lambda i, c: (i, c, 0)),
            pl.BlockSpec((1, CHUNK, cdim), lambda i, c: (i, c, 0)),
            pl.BlockSpec((1, CHUNK, LANES), lambda i, c: (i, c, 0)),
            full((CONV_WIDTH, cdim)), full((1, cdim)), full((1, LANES)), full((1, LANES)),
            full((1, d_inner)), full((1, d_inner)), full((LANES, d_inner)),
        ],
        out_specs=pl.BlockSpec((1, CHUNK, d_inner), lambda i, c: (i, c, 0)),
        out_shape=jax.ShapeDtypeStruct((b, l, d_inner), BF16),
        scratch_shapes=[
            pltpu.VMEM((CHUNK + 16, cdim), BF16),
            pltpu.VMEM((CHUNK, cdim), F32),
            pltpu.VMEM((SSM_GROUPS, SSM_STATE, gw), F32),
            pltpu.VMEM((LANES, CHUNK), F32),
            pltpu.VMEM((CHUNK, d_inner), F32),
        ],
        compiler_params=_params(("parallel", "arbitrary"), 40),
        name="ssd",
    )(z, xbc, dt_raw, conv_w, conv_b, dt_bias3, a_log3, dskip_e, gn_w, expand)


def _cum_kernel(f_ref, bf_ref, pq_ref, pk_ref, eq_ref, ek_ref, *, blk, n_heads):
    l = f_ref.shape[1]
    row = lax.broadcasted_iota(I32, (blk, blk), 0)
    col = lax.broadcasted_iota(I32, (blk, blk), 1)
    tril = jnp.where(row >= col, 1.0, 0.0).astype(F32)
    lane = lax.broadcasted_iota(I32, (blk, LANES), 1)
    carry = jnp.zeros((1, LANES), F32)
    for i in range(l // blk):
        rs = slice(i * blk, (i + 1) * blk)
        x = f_ref[0, rs, :] + bf_ref[...]
        lf = -_softplus(-x)
        cs = jnp.dot(tril, lf, preferred_element_type=F32, precision=HIGHEST) + carry
        carry = cs[blk - 1:blk, :]
        lhs = jnp.where(lane == 3 * n_heads, jnp.ones((), BF16), _split3(cs * LOG2E, n_heads))
        eq_ref[0, rs, :] = jnp.dot(lhs, pq_ref[...], preferred_element_type=F32).astype(BF16)
        ek_ref[0, rs, :] = jnp.dot(lhs, pk_ref[...], preferred_element_type=F32).astype(BF16)


def _ext_placement(n_heads):
    w = n_heads * ATT_HEADDIM
    pq = np.zeros((LANES, w), np.float32)
    pk = np.zeros((LANES, w), np.float32)
    one = 3 * n_heads
    for h in range(n_heads):
        base = (h // 2) * LANES + (ATT_HEADDIM if h % 2 == 0 else 0)
        for piece in range(3):
            pq[piece * n_heads + h, base + piece] = 1.0
            pq[one, base + 3 + piece] = 1.0
            pk[one, base + piece] = 1.0
            pk[piece * n_heads + h, base + 3 + piece] = -1.0
    return jnp.asarray(pq, BF16), jnp.asarray(pk, BF16)


def forget_ext(f, b_f3, n_heads):
    b, l, _ = f.shape
    w = n_heads * ATT_HEADDIM
    pq, pk = _ext_placement(n_heads)
    out = jax.ShapeDtypeStruct((b, l, w), BF16)
    return pl.pallas_call(
        functools.partial(_cum_kernel, blk=256, n_heads=n_heads),
        grid=(b,),
        in_specs=[
            pl.BlockSpec((1, l, LANES), lambda i: (i, 0, 0)),
            pl.BlockSpec((1, LANES), lambda i: (0, 0)),
            pl.BlockSpec((LANES, w), lambda i: (0, 0)),
            pl.BlockSpec((LANES, w), lambda i: (0, 0)),
        ],
        out_specs=[pl.BlockSpec((1, l, w), lambda i: (i, 0, 0)), pl.BlockSpec((1, l, w), lambda i: (i, 0, 0))],
        out_shape=[out, out],
        compiler_params=_params(("parallel",), 40),
        name="forget_ext",
    )(f, b_f3, pq, pk)


def _head_rms(x, lane_lo):
    sq = x * x
    s_lo = jnp.sum(jnp.where(lane_lo, sq, 0.0), axis=-1, keepdims=True)
    s_hi = jnp.sum(jnp.where(lane_lo, 0.0, sq), axis=-1, keepdims=True)
    inv = 1.0 / ATT_HEADDIM
    return jnp.where(lane_lo, lax.rsqrt(s_lo * inv + EPS), lax.rsqrt(s_hi * inv + EPS))


def _attn_kernel(q_ref, k_ref, v_ref, eq_ref, ek_ref, qnw_ref, knw_ref, o_ref, kaug, vaugt, s_scr, p_scr, *, tq):
    l = q_ref.shape[1]
    nblk = l // tq
    hd = ATT_HEADDIM
    lane_lo = lax.broadcasted_iota(I32, (tq, LANES), 1) < hd
    sub_lo = lax.broadcasted_iota(I32, (LANES, tq), 0) < hd
    row = lax.broadcasted_iota(I32, (tq, tq), 0)
    col = lax.broadcasted_iota(I32, (tq, tq), 1)
    visible = row <= col
    eye_q = jnp.where(row == col, 1.0, 0.0).astype(BF16)
    eye_l = eye_q[:LANES, :LANES]
    nt = (((1,), (1,)), ((), ()))
    qscale = qnw_ref[...] * (hd ** -0.5 * LOG2E)

    for i in range(nblk):
        rs = slice(i * tq, (i + 1) * tq)
        kk = k_ref[0, rs, :].astype(F32)
        kn = (kk * _head_rms(kk, lane_lo) * knw_ref[...]).astype(BF16)
        ek = ek_ref[0, rs, :]
        kaug[0, rs, :] = jnp.where(lane_lo, kn, ek)
        kaug[1, rs, :] = jnp.where(lane_lo, ek, kn)
        vt = lax.dot_general(eye_l, v_ref[0, rs, :], nt, preferred_element_type=F32).astype(BF16)
        one = jnp.ones_like(vt)
        vaugt[0, :, rs] = jnp.where(sub_lo, vt, one)
        vaugt[1, :, rs] = jnp.where(sub_lo, one, vt)

    def scores(qi):
        qs = slice(qi * tq, (qi + 1) * tq)
        nk = (qi + 1) * tq
        qq = q_ref[0, qs, :].astype(F32)
        qn = (qq * _head_rms(qq, lane_lo) * qscale).astype(BF16)
        eq = eq_ref[0, qs, :]
        for hh in range(2):
            qa = jnp.where(lane_lo, qn, eq) if hh == 0 else jnp.where(lane_lo, eq, qn)
            qt = lax.dot_general(eye_l, qa, nt, preferred_element_type=F32).astype(BF16)
            s = jnp.dot(kaug[hh, :nk, :], qt, preferred_element_type=F32)
            s_scr[qi % 2, hh, :nk, :] = s

    def softmax_pv(qi):
        qs = slice(qi * tq, (qi + 1) * tq)
        nk = (qi + 1) * tq
        par = qi % 2
        accs = []
        for hh in range(2):
            s_scr[par, hh, nk - tq:nk, :] = jnp.where(visible, s_scr[par, hh, nk - tq:nk, :], -jnp.inf)
            m = jnp.max(s_scr[par, hh, :nk, :], axis=0, keepdims=True)
            p_scr[par, hh, :nk, :] = jnp.exp2(s_scr[par, hh, :nk, :] - m).astype(BF16)
            accs.append(jnp.dot(vaugt[hh, :, :nk], p_scr[par, hh, :nk, :], preferred_element_type=F32))
        a0, a1 = accs
        o_t = jnp.concatenate([a0[:hd] / a0[hd:], a1[hd:] / a1[:hd]], axis=0).astype(BF16)
        o_ref[0, qs, :] = lax.dot_general(eye_q, o_t, nt, preferred_element_type=F32).astype(o_ref.dtype)

    scores(0)
    for qi in range(nblk):
        if qi + 1 < nblk:
            scores(qi + 1)
        softmax_pv(qi)


def fox_attention(qp, kv, ext_q, ext_k, qnw2, knw2, att_w, tq):
    b, l, _ = qp.shape
    npair = att_w // LANES
    nblk = l // tq
    blk = lambda off: pl.BlockSpec((1, l, LANES), lambda i, j: (i, 0, j + off))
    return pl.pallas_call(
        functools.partial(_attn_kernel, tq=tq),
        grid=(b, npair),
        in_specs=[
            blk(0), blk(0), blk(npair), blk(0), blk(0),
            pl.BlockSpec((1, LANES), lambda i, j: (0, 0)),
            pl.BlockSpec((1, LANES), lambda i, j: (0, 0)),
        ],
        out_specs=blk(0),
        out_shape=jax.ShapeDtypeStruct((b, l, att_w), BF16),
        scratch_shapes=[
            pltpu.VMEM((2, l, LANES), BF16),
            pltpu.VMEM((2, LANES, l), BF16),
            pltpu.VMEM((2, 2, l, tq), F32),
            pltpu.VMEM((2, 2, l, tq), BF16),
        ],
        compiler_params=_params(("parallel", "parallel"), 40),
        name="fox_attention",
    )(qp, kv, kv, ext_q, ext_k, qnw2, knw2)


def _router_kernel(h_ref, nw_ref, wr_ref, xpk_ref, topt_ref, sel_ref, top_ref):
    x = h_ref[...]
    u = x * _rms_scale(x) * nw_ref[...]
    xpk_ref[...] = lax.bitcast_convert_type(_pack_bf16_pairs(u), I32)
    logits = jnp.dot(u, wr_ref[...], preferred_element_type=F32, precision=HIGHEST)
    lt = logits.T[:N_EXPERTS, :]
    sub = lax.broadcasted_iota(I32, lt.shape, 0)
    neg = jnp.float32(-jnp.inf)
    m1 = jnp.max(lt, axis=0, keepdims=True)
    i1 = jnp.min(jnp.where(lt == m1, sub, N_EXPERTS), axis=0, keepdims=True)
    rest = jnp.where(sub == i1, neg, lt)
    m2 = jnp.max(rest, axis=0, keepdims=True)
    i2 = jnp.min(jnp.where(rest == m2, sub, N_EXPERTS), axis=0, keepdims=True)
    e2 = jnp.exp(m2 - m1)
    g1 = 1.0 / (1.0 + e2)
    g2 = e2 / (1.0 + e2)
    sel_ref[...] = jnp.where((sub == i1) | (sub == i2), 1.0, 0.0).astype(F32)
    topt = jnp.where(sub == 0, i1.astype(F32),
                     jnp.where(sub == 1, i2.astype(F32), jnp.where(sub == 2, g1, jnp.where(sub == 3, g2, 0.0))))
    topt_ref[...] = topt
    pad = jnp.zeros((LANES - N_EXPERTS, topt.shape[1]), F32)
    top_ref[...] = jnp.concatenate([topt, pad], axis=0).T


def router(h, nw, wr_pad, tm):
    t, d = h.shape
    return pl.pallas_call(
        _router_kernel,
        grid=(t // tm,),
        in_specs=[
            pl.BlockSpec((tm, d), lambda i: (i, 0)),
            pl.BlockSpec((1, d), lambda i: (0, 0)),
            pl.BlockSpec((d, LANES), lambda i: (0, 0)),
        ],
        out_specs=[
            pl.BlockSpec((tm, d // 2), lambda i: (i, 0)),
            pl.BlockSpec((N_EXPERTS, tm), lambda i: (0, i)),
            pl.BlockSpec((N_EXPERTS, tm), lambda i: (0, i)),
            pl.BlockSpec((tm, LANES), lambda i: (i, 0)),
        ],
        out_shape=[
            jax.ShapeDtypeStruct((t, d // 2), I32),
            jax.ShapeDtypeStruct((N_EXPERTS, t), F32),
            jax.ShapeDtypeStruct((N_EXPERTS, t), F32),
            jax.ShapeDtypeStruct((t, LANES), F32),
        ],
        compiler_params=_params(("parallel",), 40),
        name="router",
    )(h, nw.reshape(1, d), wr_pad)


def _rank_kernel(sel_ref, rank_ref, cnt_ref, carry):
    @pl.when(pl.program_id(0) == 0)
    def _():
        carry[...] = jnp.zeros(carry.shape, F32)

    s = sel_ref[...]
    tb = s.shape[1]
    row = lax.broadcasted_iota(I32, (tb, tb), 0)
    col = lax.broadcasted_iota(I32, (tb, tb), 1)
    before = jnp.where(row < col, 1.0, 0.0).astype(BF16)
    c = carry[...]
    rank_ref[...] = jnp.dot(s.astype(BF16), before, preferred_element_type=F32) + c[:, :1]
    c = c + jnp.sum(s, axis=1, keepdims=True)
    carry[...] = c
    cnt_ref[...] = c


def moe_rank(sel, tb):
    ne, t = sel.shape
    return pl.pallas_call(
        _rank_kernel,
        grid=(t // tb,),
        in_specs=[pl.BlockSpec((ne, tb), lambda i: (0, i))],
        out_specs=[pl.BlockSpec((ne, tb), lambda i: (0, i)), pl.BlockSpec((ne, LANES), lambda i: (0, 0))],
        out_shape=[jax.ShapeDtypeStruct((ne, t), F32), jax.ShapeDtypeStruct((ne, LANES), F32)],
        scratch_shapes=[pltpu.VMEM((ne, LANES), F32)],
        compiler_params=_params(("arbitrary",), 32),
        name="moe_rank",
    )(sel)


def _pos_kernel(topt_ref, rank_ref, off_ref, pos_ref):
    posv = rank_ref[...] + off_ref[...][:, :1]
    sub = lax.broadcasted_iota(I32, posv.shape, 0).astype(F32)
    for k in range(TOP_K):
        ek = topt_ref[k:k + 1, :]
        pos_ref[0, k:k + 1, :] = jnp.sum(jnp.where(sub == ek, posv, 0.0), axis=0, keepdims=True).astype(I32)


def moe_pos(topt, rank, off, tb):
    ne, t = rank.shape
    return pl.pallas_call(
        _pos_kernel,
        grid=(t // tb,),
        in_specs=[
            pl.BlockSpec((ne, tb), lambda i: (0, i)),
            pl.BlockSpec((ne, tb), lambda i: (0, i)),
            pl.BlockSpec((ne, LANES), lambda i: (0, 0)),
        ],
        out_specs=pl.BlockSpec((1, TOP_K, tb), lambda i: (i, 0, 0)),
        out_shape=jax.ShapeDtypeStruct((t // tb, TOP_K, tb), I32),
        compiler_params=_params(("parallel",), 32),
        name="moe_pos",
    )(topt, rank, off)


def _expert_ffn_kernel(te_ref, nv_ref, nr_ref, x_ref, wg_ref, wu_ref, wd_ref, o_ref, xn_ref, acc_ref):
    del te_ref
    i = pl.program_id(0)
    j = pl.program_id(1)

    @pl.when(i < nv_ref[0])
    def _():
        @pl.when(j == 0)
        def _():
            xw = x_ref[...]
            live = lax.broadcasted_iota(I32, xw.shape, 0) < nr_ref[i]
            xw = lax.bitcast_convert_type(jnp.where(live, xw, 0), U32)
            xn_ref[...] = _unpack_bf16_pairs(xw).astype(BF16)
            acc_ref[...] = jnp.zeros(acc_ref.shape, F32)

        xn = xn_ref[...]
        g = jnp.dot(xn, wg_ref[0].astype(BF16), preferred_element_type=F32)
        u = jnp.dot(xn, wu_ref[0].astype(BF16), preferred_element_type=F32)
        a = (g * _sigmoid(g) * u).astype(BF16)
        acc_ref[...] += jnp.dot(a, wd_ref[0].astype(BF16), preferred_element_type=F32)

        @pl.when(j == pl.num_programs(1) - 1)
        def _():
            o_ref[...] = lax.bitcast_convert_type(_pack_bf16_pairs(acc_ref[...]), I32)

    @pl.when((i >= nv_ref[0]) & (j == pl.num_programs(1) - 1))
    def _():
        o_ref[...] = jnp.zeros(o_ref.shape, o_ref.dtype)


def expert_ffn(xs, tile_expert, n_valid, tile_rows, wg, wu, wd, tm, tf):
    rows, dw = xs.shape
    ne, d, f = wg.shape
    nf = f // tf
    row_blk = lambda i, j, te, nv, nr: (jnp.maximum(jnp.minimum(i, nv[0] - 1), 0), 0)
    out_blk = lambda i, j, te, nv, nr: (i, 0)
    col_j = lambda i, j, nv: jnp.where(i < nv[0], j, nf - 1)
    return pl.pallas_call(
        _expert_ffn_kernel,
        grid_spec=pltpu.PrefetchScalarGridSpec(
            num_scalar_prefetch=3,
            grid=(rows // tm, nf),
            in_specs=[
                pl.BlockSpec((tm, dw), row_blk),
                pl.BlockSpec((1, d, tf), lambda i, j, te, nv, nr: (te[i], 0, col_j(i, j, nv))),
                pl.BlockSpec((1, d, tf), lambda i, j, te, nv, nr: (te[i], 0, col_j(i, j, nv))),
                pl.BlockSpec((1, tf, d), lambda i, j, te, nv, nr: (te[i], col_j(i, j, nv), 0)),
            ],
            out_specs=pl.BlockSpec((tm, dw), out_blk),
            scratch_shapes=[pltpu.VMEM((tm, d), BF16), pltpu.VMEM((tm, d), F32)],
        ),
        out_shape=jax.ShapeDtypeStruct((rows, dw), I32),
        compiler_params=_params(("arbitrary", "arbitrary"), 48),
        name="expert_ffn",
    )(tile_expert, n_valid, tile_rows, xs, wg, wu, wd)


SC_CORES = 2
SC_SUBCORES = 16
SC_INDEX_WINDOW = 128
SC_GATHER_ROWS = 64


def sc_row_gather(data, idx):
    n = idx.shape[0]
    d = data.shape[1]
    n_workers = SC_CORES * SC_SUBCORES
    per_worker = n // n_workers
    win, half = SC_INDEX_WINDOW, SC_GATHER_ROWS
    assert n % n_workers == 0 and per_worker % win == 0 and win == 2 * half
    mesh = plsc.VectorSubcoreMesh(core_axis_name="c", subcore_axis_name="s")

    @functools.partial(
        pl.kernel, out_type=jax.ShapeDtypeStruct((n, d), data.dtype), mesh=mesh,
        scratch_types=[pltpu.VMEM((win,), I32), pltpu.VMEM((half, d), data.dtype), pltpu.VMEM((half, d), data.dtype),
                       pltpu.SemaphoreType.DMA, pltpu.SemaphoreType.DMA])
    def gather_kernel(x_hbm, i_hbm, o_hbm, idx_v, rows_a, rows_b, sem_a, sem_b):
        wid = lax.axis_index("s") * SC_CORES + lax.axis_index("c")
        base = wid * per_worker

        @pl.loop(0, per_worker // win)
        def _(c):
            off = base + c * win
            pltpu.sync_copy(i_hbm.at[pl.ds(off, win)], idx_v)
            cp_a = pltpu.async_copy(x_hbm.at[idx_v.at[pl.ds(0, half)]], rows_a, sem_a)
            cp_b = pltpu.async_copy(x_hbm.at[idx_v.at[pl.ds(half, half)]], rows_b, sem_b)
            cp_a.wait()
            pltpu.sync_copy(rows_a, o_hbm.at[pl.ds(off, half)])
            cp_b.wait()
            pltpu.sync_copy(rows_b, o_hbm.at[pl.ds(off + half, half)])

    return gather_kernel(data, idx)


def sc_row_scatter(rows, idx, n_out):
    r, d = rows.shape
    n = idx.shape[0]
    n_workers = SC_CORES * SC_SUBCORES
    per_worker = n // n_workers
    win = SC_INDEX_WINDOW
    assert n % n_workers == 0 and per_worker % win == 0 and r % per_worker == 0
    mesh = plsc.VectorSubcoreMesh(core_axis_name="c", subcore_axis_name="s")

    @functools.partial(
        pl.kernel, out_type=jax.ShapeDtypeStruct((n_out, d), rows.dtype), mesh=mesh,
        scratch_types=[pltpu.VMEM((win,), I32), pltpu.VMEM((win, d), rows.dtype)])
    def scatter_kernel(x_hbm, i_hbm, o_hbm, idx_v, rows_v):
        wid = lax.axis_index("s") * SC_CORES + lax.axis_index("c")
        base = wid * per_worker
        src_base = lax.rem(base, r)

        @pl.loop(0, per_worker // win)
        def _(c):
            pltpu.sync_copy(i_hbm.at[pl.ds(base + c * win, win)], idx_v)
            pltpu.sync_copy(x_hbm.at[pl.ds(src_base + c * win, win)], rows_v)
            pltpu.sync_copy(rows_v, o_hbm.at[idx_v])

    return scatter_kernel(rows, idx)


def _combine_dense_ple_kernel(h_ref, top_ref, y1_ref, y2_ref, p_ref, nw_ref, wg_ref, wp_ref, o_ref):
    top = top_ref[...]
    x = h_ref[...]
    for k, y_ref in enumerate((y1_ref, y2_ref)):
        y = _unpack_bf16_pairs(lax.bitcast_convert_type(y_ref[...], U32))
        x = x + top[:, TOP_K + k:TOP_K + k + 1] * y
    o_ref[...] = _ple_update(x, p_ref[...], nw_ref[...], wg_ref[...], wp_ref[...])


def moe_combine_dense_ple(h, top, yg, p, layer, nw, wg, wp, tm):
    t, d = h.shape
    dw = yg.shape[1]
    pd = p.shape[2]
    nt = t // tm
    resident = lambda shape: pl.BlockSpec(shape, lambda i: (0, 0), pipeline_mode=pl.Buffered(1))
    return pl.pallas_call(
        _combine_dense_ple_kernel,
        grid=(nt,),
        in_specs=[
            pl.BlockSpec((tm, d), lambda i: (i, 0)),
            pl.BlockSpec((tm, LANES), lambda i: (i, 0)),
            pl.BlockSpec((tm, dw), lambda i: (i, 0)),
            pl.BlockSpec((tm, dw), lambda i: (i + nt, 0)),
            pl.BlockSpec((None, tm, pd), lambda i: (layer, i, 0)),
            resident((1, d)), resident((d, d)), resident((pd, d)),
        ],
        out_specs=pl.BlockSpec((tm, d), lambda i: (i, 0)),
        out_shape=jax.ShapeDtypeStruct((t, d), F32),
        compiler_params=_params(("parallel",), 48),
        name="moe_combine_dense_ple",
    )(h, top, yg, yg, p, nw.reshape(1, d), wg, wp)


def moe_ple(h, nw, w_router, wg, wu, wd, p, layer, ple_nw, ple_wg, ple_wp, tm_route, tm_expert, tb, tf):
    t, d = h.shape
    xpk, topt, sel, top = router(h, nw, _pad_cols(w_router, LANES), tm_route)
    rank, cnt = moe_rank(sel, tb)
    counts = cnt[:, 0].astype(I32)
    padded = ((counts + tm_expert - 1) // tm_expert) * tm_expert
    ends = jnp.cumsum(padded)
    n_tiles = (TOP_K * t) // tm_expert + N_EXPERTS
    n_valid = (ends[-1] // tm_expert).astype(I32)
    tile_start = jnp.arange(n_tiles, dtype=I32) * tm_expert
    first_row = jnp.minimum(tile_start, ends[-1] - 1)
    tile_expert = jnp.minimum(jnp.sum(ends[None, :] <= first_row[:, None], axis=1), N_EXPERTS - 1).astype(I32)
    starts = ends - padded
    tile_rows = jnp.clip(counts[tile_expert] - (tile_start - starts[tile_expert]), 0, tm_expert)
    tile_rows = jnp.where(tile_start < ends[-1], tile_rows, 0).astype(I32)
    off = jnp.broadcast_to(starts.astype(F32)[:, None], (N_EXPERTS, LANES))
    pos = moe_pos(topt, rank, off, tb)
    pos_slot_major = jnp.transpose(pos, (1, 0, 2)).reshape(-1)
    xs = sc_row_scatter(xpk, pos_slot_major, n_tiles * tm_expert)
    ys = expert_ffn(xs, tile_expert, n_valid.reshape(1), tile_rows, wg, wu, wd, tm_expert, tf)
    yg = sc_row_gather(ys, pos_slot_major)
    return moe_combine_dense_ple(h, top, yg, p, layer, ple_nw, ple_wg, ple_wp, tm_route)


def _pad_cols(w, n):
    return jnp.pad(w, ((0, 0), (0, n - w.shape[1])))


def _tile3(v, n):
    return jnp.pad(jnp.tile(v, 3), (0, LANES - 3 * n)).reshape(1, LANES)


def kernel(x, p, ssm_norm_w, ssm_w_in, ssm_conv_w, ssm_conv_b, ssm_dt_bias, ssm_a_log, ssm_d, ssm_gn_w, ssm_w_out, kv_norm_w, w_kv, b_f, k_norm_w, att_norm_w, att_w_q, q_norm_w, att_w_o, ffn_norm_w, ffn_w_gate, ffn_w_up, ffn_w_down, moe_norm_w, moe_w_router, moe_w_gate, moe_w_up, moe_w_down, ple_norm_w, ple_w_gate, ple_w_proj):
    b, l, d = x.shape
    t = b * l
    n_a = ssm_norm_w.shape[0]
    depth = ple_norm_w.shape[0]
    d_inner = ssm_gn_w.shape[1]
    n_heads = ssm_dt_bias.shape[1]
    cdim = ssm_conv_w.shape[2]
    att_w = att_w_q.shape[2]
    n_att_heads = att_w // ATT_HEADDIM
    tm = min(1024, t)
    tq = min(256, l)

    h = x.reshape(t, d)
    p3 = p.reshape(depth, t, p.shape[-1])
    r = jnp.arange(LANES)[:, None]
    c = jnp.arange(d_inner)[None, :]
    expand = ((r < 3 * n_heads) & ((r % n_heads) == (c // SSM_HEADDIM))).astype(BF16)

    kv = ext_q = ext_k = None
    for i in range(depth):
        if i < n_a:
            w_in = ssm_w_in[i].astype(BF16)
            w_dt = w_in[:, d_inner + cdim:]
            w_dt3 = _pad_cols(jnp.concatenate([w_dt, w_dt, w_dt], axis=1), LANES)
            z, xbc, dt_raw = norm_matmul(h, ssm_norm_w[i], [w_in[:, :d_inner], w_in[:, d_inner:d_inner + cdim], w_dt3],
                                         [BF16, BF16, F32], tm // 2)
            y = ssd(z.reshape(b, l, d_inner), xbc.reshape(b, l, cdim), dt_raw.reshape(b, l, LANES),
                    ssm_conv_w[i], ssm_conv_b[i].reshape(1, cdim), _tile3(ssm_dt_bias[i], n_heads),
                    _tile3(ssm_a_log[i], n_heads), jnp.repeat(ssm_d[i], SSM_HEADDIM).reshape(1, d_inner),
                    ssm_gn_w[i].reshape(1, d_inner), expand)
            h = matmul_residual(h, y.reshape(t, d_inner), ssm_w_out[i].astype(BF16), tm)
        else:
            j = i - n_a
            if i == n_a:
                wkv = w_kv.astype(BF16)
                w_f = wkv[:, 2 * att_w:]
                w_f3 = _pad_cols(jnp.concatenate([w_f, w_f, w_f], axis=1), LANES)
                kv, f = norm_matmul(h, kv_norm_w, [wkv[:, :2 * att_w], w_f3], [BF16, F32], tm)
                kv = kv.reshape(b, l, 2 * att_w)
                ext_q, ext_k = forget_ext(f.reshape(b, l, LANES), _tile3(b_f, n_att_heads), n_att_heads)
            qp, = norm_matmul(h, att_norm_w[j], [att_w_q[j].astype(BF16)], [BF16], tm)
            qp = qp.reshape(b, l, att_w)
            qnw2 = jnp.tile(q_norm_w[j], 2).reshape(1, LANES)
            knw2 = jnp.tile(k_norm_w, 2).reshape(1, LANES)
            o = fox_attention(qp, kv, ext_q, ext_k, qnw2, knw2, att_w, tq)
            h = matmul_residual(h, o.reshape(t, att_w), att_w_o[j].astype(BF16), tm)
        ple_wg, ple_wp = ple_w_gate[i].astype(BF16), ple_w_proj[i].astype(BF16)
        if i % 2 == 0:
            dd = i // 2
            h = ffn_ple(h, ffn_norm_w[dd], ffn_w_gate[dd].astype(BF16), ffn_w_up[dd].astype(BF16),
                        ffn_w_down[dd].astype(BF16), p3, i, ple_norm_w[i], ple_wg, ple_wp, tm, 512)
        else:
            m = i // 2
            h = moe_ple(h, moe_norm_w[m], moe_w_router[m], moe_w_gate[m], moe_w_up[m], moe_w_down[m], p3, i,
                        ple_norm_w[i], ple_wg, ple_wp, tm, tm, min(512, t), 512)
    return h.reshape(b, l, d)
```

```python
import functools
import math

import numpy as np
import jax
import jax.numpy as jnp
from jax import lax
from jax.experimental import pallas as pl
from jax.experimental.pallas import tpu as pltpu
from jax.experimental.pallas import tpu_sc as plsc

F32 = jnp.float32
BF16 = jnp.bfloat16
I32 = jnp.int32
U32 = jnp.uint32
EPS = 1e-6
HIGHEST = lax.Precision.HIGHEST
LOG2E = math.log2(math.e)

LANES = 128
SUBLANES = 8
MIB = 1024 * 1024

SSM_HEADDIM = 64
SSM_GROUPS = 4
SSM_STATE = 128
CONV_WIDTH = 4
CHUNK = 128
ATT_HEADDIM = 64
N_EXPERTS = 8
TOP_K = 2


def _params(sem, vmem_mib):
    return pltpu.CompilerParams(dimension_semantics=sem, vmem_limit_bytes=vmem_mib * MIB)


def _rms_scale(x):
    return lax.rsqrt(jnp.mean(x * x, axis=-1, keepdims=True) + EPS)


def _sigmoid(x):
    return 1.0 / (1.0 + jnp.exp2(x * (-LOG2E)))


def _softplus(x):
    return jnp.maximum(x, 0.0) + jnp.log(1.0 + jnp.exp(-jnp.abs(x)))


def _pack_bf16_pairs(x):
    n = x.shape[1] // 2
    bits = lax.bitcast_convert_type(x.astype(BF16).astype(F32), U32)
    return lax.shift_right_logical(bits[:, :n], jnp.uint32(16)) | (bits[:, n:] & jnp.uint32(0xFFFF0000))


def _unpack_bf16_pairs(w):
    lo = lax.bitcast_convert_type(lax.shift_left(w, jnp.uint32(16)), F32)
    hi = lax.bitcast_convert_type(w & jnp.uint32(0xFFFF0000), F32)
    return jnp.concatenate([lo, hi], axis=1)


def _norm_matmul_kernel(h_ref, nw_ref, *refs, norm_of):
    n = len(refs) // 2
    x = h_ref[...]
    xs = x * _rms_scale(x)
    xn = [(xs * nw_ref[g:g + 1, :]).astype(BF16) for g in range(nw_ref.shape[0])]
    for w_ref, o_ref, g in zip(refs[:n], refs[n:], norm_of):
        o_ref[...] = jnp.dot(xn[g], w_ref[...], preferred_element_type=F32).astype(o_ref.dtype)


def norm_matmul(h, nws, ws, norm_of, out_dtypes, tm):
    t, d = h.shape
    resident = lambda shape: pl.BlockSpec(shape, lambda i: (0, 0), pipeline_mode=pl.Buffered(1))
    return pl.pallas_call(
        functools.partial(_norm_matmul_kernel, norm_of=tuple(norm_of)),
        grid=(t // tm,),
        in_specs=[pl.BlockSpec((tm, d), lambda i: (i, 0)), resident((len(nws), d))] + [resident(w.shape) for w in ws],
        out_specs=[pl.BlockSpec((tm, w.shape[1]), lambda i: (i, 0)) for w in ws],
        out_shape=[jax.ShapeDtypeStruct((t, w.shape[1]), dt) for w, dt in zip(ws, out_dtypes)],
        compiler_params=_params(("parallel",), 48),
        name="norm_matmul",
    )(h, jnp.stack(nws), *ws)


def _matmul_residual_kernel(h_ref, y_ref, w_ref, o_ref):
    o_ref[...] = h_ref[...] + jnp.dot(y_ref[...], w_ref[...], preferred_element_type=F32)


def matmul_residual(h, y, w, tm):
    t, n = h.shape
    k = y.shape[1]
    return pl.pallas_call(
        _matmul_residual_kernel,
        grid=(t // tm,),
        in_specs=[
            pl.BlockSpec((tm, n), lambda i: (i, 0)),
            pl.BlockSpec((tm, k), lambda i: (i, 0)),
            pl.BlockSpec((k, n), lambda i: (0, 0)),
        ],
        out_specs=pl.BlockSpec((tm, n), lambda i: (i, 0)),
        out_shape=jax.ShapeDtypeStruct((t, n), F32),
        compiler_params=_params(("parallel",), 48),
        name="matmul_residual",
    )(h, y, w)


def _ple_update(x, p, nw, wg, wp):
    xn = (x * _rms_scale(x) * nw).astype(BF16)
    gate = _sigmoid(jnp.dot(xn, wg, preferred_element_type=F32))
    proj = jnp.dot(p.astype(BF16), wp, preferred_element_type=F32)
    return x + proj * gate


def _ffn_ple_kernel(h_ref, nw_ref, wg_ref, wu_ref, wd_ref, p_ref, pnw_ref, pwg_ref, pwp_ref, o_ref, xn_ref, acc_ref):
    j = pl.program_id(1)

    @pl.when(j == 0)
    def _():
        x = h_ref[...]
        xn_ref[...] = (x * _rms_scale(x) * nw_ref[...]).astype(xn_ref.dtype)
        acc_ref[...] = x

    xn = xn_ref[...]
    g = jnp.dot(xn, wg_ref[...], preferred_element_type=F32)
    u = jnp.dot(xn, wu_ref[...], preferred_element_type=F32)
    a = (g * _sigmoid(g) * u).astype(BF16)
    acc_ref[...] += jnp.dot(a, wd_ref[...], preferred_element_type=F32)

    @pl.when(j == pl.num_programs(1) - 1)
    def _():
        o_ref[...] = _ple_update(acc_ref[...], p_ref[...], pnw_ref[...], pwg_ref[...], pwp_ref[...])


def ffn_ple(h, nw, wg, wu, wd, p, layer, ple_nw, ple_wg, ple_wp, tm, tf):
    t, d = h.shape
    f = wg.shape[1]
    pd = p.shape[2]
    resident = lambda shape: pl.BlockSpec(shape, lambda i, j: (0, 0), pipeline_mode=pl.Buffered(1))
    return pl.pallas_call(
        _ffn_ple_kernel,
        grid=(t // tm, f // tf),
        in_specs=[
            pl.BlockSpec((tm, d), lambda i, j: (i, 0)),
            resident((1, d)),
            pl.BlockSpec((d, tf), lambda i, j: (0, j)),
            pl.BlockSpec((d, tf), lambda i, j: (0, j)),
            pl.BlockSpec((tf, d), lambda i, j: (j, 0)),
            pl.BlockSpec((None, tm, pd), lambda i, j: (layer, i, 0)),
            resident((1, d)), resident((d, d)), resident((pd, d)),
        ],
        out_specs=pl.BlockSpec((tm, d), lambda i, j: (i, 0)),
        out_shape=jax.ShapeDtypeStruct((t, d), F32),
        scratch_shapes=[pltpu.VMEM((tm, d), BF16), pltpu.VMEM((tm, d), F32)],
        compiler_params=_params(("parallel", "arbitrary"), 52),
        name="ffn_ple",
    )(h, nw.reshape(1, d), wg, wu, wd, p, ple_nw.reshape(1, d), ple_wg, ple_wp)


def _split3(v, n):
    v1 = v.astype(BF16)
    r1 = v - v1.astype(F32)
    v2 = r1.astype(BF16)
    v3 = (r1 - v2.astype(F32)).astype(BF16)
    lane = lax.broadcasted_iota(I32, v.shape, 1)
    return jnp.where(lane < n, v1, jnp.where(lane < 2 * n, v2, v3))


def _ssd_kernel(z_ref, xbc_ref, dt_ref, cw_ref, cb_ref, dtb_ref, alog_ref, dskip_ref, gnw_ref, exp_ref,
                o_ref, cbuf, xact, state, acs_t, y_scr):
    q = CHUNK
    tail = cbuf.shape[0] - q

    @pl.when(pl.program_id(1) == 0)
    def _():
        cbuf[0:tail, :] = jnp.zeros((tail, cbuf.shape[1]), BF16)
        state[...] = jnp.zeros(state.shape, F32)

    for sub in range(z_ref.shape[1] // q):
        rows = slice(sub * q, (sub + 1) * q)
        _ssd_chunk(z_ref.at[0, rows], xbc_ref.at[0, rows], dt_ref.at[0, rows], cw_ref, cb_ref, dtb_ref, alog_ref,
                   dskip_ref, gnw_ref, exp_ref, o_ref.at[0, rows], cbuf, xact, state, acs_t, y_scr)


def _ssd_chunk(z_ref, xbc_ref, dt_ref, cw_ref, cb_ref, dtb_ref, alog_ref, dskip_ref, gnw_ref, exp_ref,
               o_ref, cbuf, xact, state, acs_t, y_scr):
    d_inner = z_ref.shape[-1]
    gw = d_inner // SSM_GROUPS
    n_heads = d_inner // SSM_HEADDIM
    q = CHUNK
    tail = cbuf.shape[0] - q

    cbuf[tail:, :] = xbc_ref[...]
    cdim = cbuf.shape[1]
    nshift = CONV_WIDTH - 1
    srow = lax.broadcasted_iota(I32, (nshift * q, tail + q), 0)
    scol = lax.broadcasted_iota(I32, (nshift * q, tail + q), 1)
    shift = jnp.where(scol == srow - (srow // q) * (q - 1) + (tail - nshift), 1.0, 0.0).astype(BF16)
    cw_lanes = 2 * LANES
    for j in range(cdim // cw_lanes):
        sl = slice(j * cw_lanes, (j + 1) * cw_lanes)
        shifted = jnp.dot(shift, cbuf[:, sl], preferred_element_type=F32)
        acc = cb_ref[:, sl] + cbuf[tail:, sl].astype(F32) * cw_ref[nshift:nshift + 1, sl]
        for w in range(nshift):
            acc = acc + shifted[w * q:(w + 1) * q] * cw_ref[w:w + 1, sl]
        xact[:, sl] = acc * _sigmoid(acc)
    cbuf[0:tail, :] = cbuf[q:q + tail, :]

    dt = _softplus(dt_ref[...] + dtb_ref[...])
    a = dt * (-LOG2E * jnp.exp(alog_ref[...]))
    row = lax.broadcasted_iota(I32, (q, q), 0)
    col = lax.broadcasted_iota(I32, (q, q), 1)
    causal = row >= col
    tril = jnp.where(causal, 1.0, 0.0).astype(F32)
    acs = jnp.dot(tril, a, preferred_element_type=F32, precision=HIGHEST)
    acs_t[...] = acs.T
    a_last = acs[q - 1:q, :]
    dt3 = _split3(dt, n_heads)
    dec3 = _split3(jnp.exp2(a_last - acs), n_heads)
    e3 = _split3(jnp.exp2(acs), n_heads)

    for g in range(SSM_GROUPS):
        gl = slice(g * gw, (g + 1) * gw)
        ex = exp_ref[:, gl]
        xs = xact[:, gl]
        b_g = xact[:, d_inner + g * SSM_STATE:d_inner + (g + 1) * SSM_STATE].astype(BF16)
        c_g = xact[:, d_inner + (SSM_GROUPS + g) * SSM_STATE:d_inner + (SSM_GROUPS + g + 1) * SSM_STATE].astype(BF16)
        dt_e = jnp.dot(dt3, ex, preferred_element_type=F32)
        dec_e = jnp.dot(dec3, ex, preferred_element_type=F32)
        e_e = jnp.dot(e3, ex, preferred_element_type=F32)
        xc = xs * dt_e
        xc_b = xc.astype(BF16)
        xdec_b = (xc * dec_e).astype(BF16)
        scores = lax.dot_general(c_g, b_g, (((1,), (1,)), ((), ())), preferred_element_type=F32)
        st = state[g]
        y = jnp.dot(c_g, st.astype(BF16), preferred_element_type=F32) * e_e
        y = y + xs * dskip_ref[:, gl]
        lane = lax.broadcasted_iota(I32, (q, LANES), 1)
        ydiag = []
        for pr in range(gw // LANES):
            ms = []
            for hh in range(2):
                h = g * (gw // SSM_HEADDIM) + 2 * pr + hh
                seg = acs[:, h:h + 1] - acs_t[h:h + 1, :]
                ms.append(scores * jnp.exp2(jnp.where(causal, seg, -jnp.inf)))
            lhs = jnp.concatenate(ms, axis=1).astype(BF16)
            xp = xc_b[:, pr * LANES:(pr + 1) * LANES]
            zero = jnp.zeros_like(xp)
            rhs = jnp.concatenate([jnp.where(lane < SSM_HEADDIM, xp, zero),
                                   jnp.where(lane >= SSM_HEADDIM, xp, zero)], axis=0)
            ydiag.append(jnp.dot(lhs, rhs, preferred_element_type=F32))
        y_scr[:, gl] = y + jnp.concatenate(ydiag, axis=1)
        upd = lax.dot_general(b_g, xdec_b, (((0,), (0,)), ((), ())), preferred_element_type=F32)
        state[g] = st * e_e[q - 1:q, :] + upd

    zz = z_ref[...].astype(F32)
    y = y_scr[...] * (zz * _sigmoid(zz))
    o_ref[...] = (y * _rms_scale(y) * gnw_ref[...]).astype(o_ref.dtype)


def ssd(z, xbc, dt_raw, conv_w, conv_b, dt_bias3, a_log3, dskip_e, gn_w, expand, chunks_per_step):
    b, l, d_inner = z.shape
    cdim = xbc.shape[-1]
    gw = d_inner // SSM_GROUPS
    rows = chunks_per_step * CHUNK
    nc = l // rows
    full = lambda shape: pl.BlockSpec(shape, lambda i, c: (0,) * len(shape))
    return pl.pallas_call(
        _ssd_kernel,
        grid=(b, nc),
        in_specs=[
            pl.BlockSpec((1, rows, d_inner), lambda i, c: (i, c, 0)),
            pl.BlockSpec((1, rows, cdim), lambda i, c: (i, c, 0)),
            pl.BlockSpec((1, rows, LANES), lambda i, c: (i, c, 0)),
            full((CONV_WIDTH, cdim)), full((1, cdim)), full((1, LANES)), full((1, LANES)),
            full((1, d_inner)), full((1, d_inner)), full((LANES, d_inner)),
        ],
        out_specs=pl.BlockSpec((1, rows, d_inner), lambda i, c: (i, c, 0)),
        out_shape=jax.ShapeDtypeStruct((b, l, d_inner), BF16),
        scratch_shapes=[
            pltpu.VMEM((CHUNK + 16, cdim), BF16),
            pltpu.VMEM((CHUNK, cdim), F32),
            pltpu.VMEM((SSM_GROUPS, SSM_STATE, gw), F32),
            pltpu.VMEM((LANES, CHUNK), F32),
            pltpu.VMEM((CHUNK, d_inner), F32),
        ],
        compiler_params=_params(("parallel", "arbitrary"), 40),
        name="ssd",
    )(z, xbc, dt_raw, conv_w, conv_b, dt_bias3, a_log3, dskip_e, gn_w, expand)


def _cum_kernel(f_ref, bf_ref, pq_ref, pk_ref, eq_ref, ek_ref, *, blk, n_heads):
    l = f_ref.shape[1]
    row = lax.broadcasted_iota(I32, (blk, blk), 0)
    col = lax.broadcasted_iota(I32, (blk, blk), 1)
    tril = jnp.where(row >= col, 1.0, 0.0).astype(F32)
    lane = lax.broadcasted_iota(I32, (blk, LANES), 1)
    carry = jnp.zeros((1, LANES), F32)
    for i in range(l // blk):
        rs = slice(i * blk, (i + 1) * blk)
        x = f_ref[0, rs, :] + bf_ref[...]
        lf = -_softplus(-x)
        cs = jnp.dot(tril, lf, preferred_element_type=F32, precision=HIGHEST) + carry
        carry = cs[blk - 1:blk, :]
        lhs = jnp.where(lane == 3 * n_heads, jnp.ones((), BF16), _split3(cs * LOG2E, n_heads))
        eq_ref[0, rs, :] = jnp.dot(lhs, pq_ref[...], preferred_element_type=F32).astype(BF16)
        ek_ref[0, rs, :] = jnp.dot(lhs, pk_ref[...], preferred_element_type=F32).astype(BF16)


def _ext_placement(n_heads):
    w = n_heads * ATT_HEADDIM
    pq = np.zeros((LANES, w), np.float32)
    pk = np.zeros((LANES, w), np.float32)
    one = 3 * n_heads
    for h in range(n_heads):
        base = (h // 2) * LANES + (ATT_HEADDIM if h % 2 == 0 else 0)
        for piece in range(3):
            pq[piece * n_heads + h, base + piece] = 1.0
            pq[one, base + 3 + piece] = 1.0
            pk[one, base + piece] = 1.0
            pk[piece * n_heads + h, base + 3 + piece] = -1.0
    return jnp.asarray(pq, BF16), jnp.asarray(pk, BF16)


def forget_ext(f, b_f3, n_heads):
    b, l, _ = f.shape
    w = n_heads * ATT_HEADDIM
    pq, pk = _ext_placement(n_heads)
    out = jax.ShapeDtypeStruct((b, l, w), BF16)
    return pl.pallas_call(
        functools.partial(_cum_kernel, blk=256, n_heads=n_heads),
        grid=(b,),
        in_specs=[
            pl.BlockSpec((1, l, LANES), lambda i: (i, 0, 0)),
            pl.BlockSpec((1, LANES), lambda i: (0, 0)),
            pl.BlockSpec((LANES, w), lambda i: (0, 0)),
            pl.BlockSpec((LANES, w), lambda i: (0, 0)),
        ],
        out_specs=[pl.BlockSpec((1, l, w), lambda i: (i, 0, 0)), pl.BlockSpec((1, l, w), lambda i: (i, 0, 0))],
        out_shape=[out, out],
        compiler_params=_params(("parallel",), 40),
        name="forget_ext",
    )(f, b_f3, pq, pk)


def _head_rms(x, lane_lo):
    sq = x * x
    s_lo = jnp.sum(jnp.where(lane_lo, sq, 0.0), axis=-1, keepdims=True)
    s_hi = jnp.sum(jnp.where(lane_lo, 0.0, sq), axis=-1, keepdims=True)
    inv = 1.0 / ATT_HEADDIM
    return jnp.where(lane_lo, lax.rsqrt(s_lo * inv + EPS), lax.rsqrt(s_hi * inv + EPS))


def _attn_kernel(q_ref, k_ref, v_ref, eq_ref, ek_ref, qnw_ref, knw_ref, o_ref, kaug, vaugt, s_scr, p_scr, *, tq):
    l = q_ref.shape[1]
    nblk = l // tq
    hd = ATT_HEADDIM
    lane_lo = lax.broadcasted_iota(I32, (tq, LANES), 1) < hd
    sub_lo = lax.broadcasted_iota(I32, (LANES, tq), 0) < hd
    row = lax.broadcasted_iota(I32, (tq, tq), 0)
    col = lax.broadcasted_iota(I32, (tq, tq), 1)
    visible = row <= col
    eye_q = jnp.where(row == col, 1.0, 0.0).astype(BF16)
    eye_l = eye_q[:LANES, :LANES]
    nt = (((1,), (1,)), ((), ()))
    qscale = qnw_ref[...] * (hd ** -0.5 * LOG2E)

    for i in range(nblk):
        rs = slice(i * tq, (i + 1) * tq)
        kk = k_ref[0, rs, :].astype(F32)
        kn = (kk * _head_rms(kk, lane_lo) * knw_ref[...]).astype(BF16)
        ek = ek_ref[0, rs, :]
        kaug[0, rs, :] = jnp.where(lane_lo, kn, ek)
        kaug[1, rs, :] = jnp.where(lane_lo, ek, kn)
        vt = lax.dot_general(eye_l, v_ref[0, rs, :], nt, preferred_element_type=F32).astype(BF16)
        one = jnp.ones_like(vt)
        vaugt[0, :, rs] = jnp.where(sub_lo, vt, one)
        vaugt[1, :, rs] = jnp.where(sub_lo, one, vt)

    def scores(qi):
        qs = slice(qi * tq, (qi + 1) * tq)
        nk = (qi + 1) * tq
        qq = q_ref[0, qs, :].astype(F32)
        qn = (qq * _head_rms(qq, lane_lo) * qscale).astype(BF16)
        eq = eq_ref[0, qs, :]
        for hh in range(2):
            qa = jnp.where(lane_lo, qn, eq) if hh == 0 else jnp.where(lane_lo, eq, qn)
            qt = lax.dot_general(eye_l, qa, nt, preferred_element_type=F32).astype(BF16)
            s = jnp.dot(kaug[hh, :nk, :], qt, preferred_element_type=F32)
            s_scr[qi % 2, hh, :nk, :] = s

    def softmax_pv(qi):
        qs = slice(qi * tq, (qi + 1) * tq)
        nk = (qi + 1) * tq
        par = qi % 2
        accs = []
        for hh in range(2):
            s_scr[par, hh, nk - tq:nk, :] = jnp.where(visible, s_scr[par, hh, nk - tq:nk, :], -jnp.inf)
            m = jnp.max(s_scr[par, hh, :nk, :], axis=0, keepdims=True)
            p_scr[par, hh, :nk, :] = jnp.exp2(s_scr[par, hh, :nk, :] - m).astype(BF16)
            accs.append(jnp.dot(vaugt[hh, :, :nk], p_scr[par, hh, :nk, :], preferred_element_type=F32))
        a0, a1 = accs
        o_t = jnp.concatenate([a0[:hd] / a0[hd:], a1[hd:] / a1[:hd]], axis=0).astype(BF16)
        o_ref[0, qs, :] = lax.dot_general(eye_q, o_t, nt, preferred_element_type=F32).astype(o_ref.dtype)

    scores(0)
    for qi in range(nblk):
        if qi + 1 < nblk:
            scores(qi + 1)
        softmax_pv(qi)


def fox_attention(qp, kv, ext_q, ext_k, qnw2, knw2, att_w, tq):
    b, l, _ = qp.shape
    npair = att_w // LANES
    nblk = l // tq
    blk = lambda off: pl.BlockSpec((1, l, LANES), lambda i, j: (i, 0, j + off))
    return pl.pallas_call(
        functools.partial(_attn_kernel, tq=tq),
        grid=(b, npair),
        in_specs=[
            blk(0), blk(0), blk(npair), blk(0), blk(0),
            pl.BlockSpec((1, LANES), lambda i, j: (0, 0)),
            pl.BlockSpec((1, LANES), lambda i, j: (0, 0)),
        ],
        out_specs=blk(0),
        out_shape=jax.ShapeDtypeStruct((b, l, att_w), BF16),
        scratch_shapes=[
            pltpu.VMEM((2, l, LANES), BF16),
            pltpu.VMEM((2, LANES, l), BF16),
            pltpu.VMEM((2, 2, l, tq), F32),
            pltpu.VMEM((2, 2, l, tq), BF16),
        ],
        compiler_params=_params(("parallel", "parallel"), 40),
        name="fox_attention",
    )(qp, kv, kv, ext_q, ext_k, qnw2, knw2)


def _oproj_router_kernel(h_ref, y_ref, wo_ref, nw_ref, wr_ref, hn_ref, xpk_ref, topt_ref, sel_ref, top_ref):
    x = h_ref[...] + jnp.dot(y_ref[...], wo_ref[...], preferred_element_type=F32)
    hn_ref[...] = x
    u = x * _rms_scale(x) * nw_ref[...]
    xpk_ref[...] = lax.bitcast_convert_type(_pack_bf16_pairs(u), I32)
    logits = jnp.dot(u, wr_ref[...], preferred_element_type=F32, precision=HIGHEST)
    lt = logits.T[:N_EXPERTS, :]
    sub = lax.broadcasted_iota(I32, lt.shape, 0)
    neg = jnp.float32(-jnp.inf)
    m1 = jnp.max(lt, axis=0, keepdims=True)
    i1 = jnp.min(jnp.where(lt == m1, sub, N_EXPERTS), axis=0, keepdims=True)
    rest = jnp.where(sub == i1, neg, lt)
    m2 = jnp.max(rest, axis=0, keepdims=True)
    i2 = jnp.min(jnp.where(rest == m2, sub, N_EXPERTS), axis=0, keepdims=True)
    e2 = jnp.exp(m2 - m1)
    g1 = 1.0 / (1.0 + e2)
    g2 = e2 / (1.0 + e2)
    sel_ref[...] = jnp.where((sub == i1) | (sub == i2), 1.0, 0.0).astype(F32)
    topt = jnp.where(sub == 0, i1.astype(F32),
                     jnp.where(sub == 1, i2.astype(F32), jnp.where(sub == 2, g1, jnp.where(sub == 3, g2, 0.0))))
    topt_ref[...] = topt
    pad = jnp.zeros((LANES - N_EXPERTS, topt.shape[1]), F32)
    top_ref[...] = jnp.concatenate([topt, pad], axis=0).T


def oproj_router(h, y, wo, nw, wr_pad, tm):
    t, d = h.shape
    k = y.shape[1]
    resident = lambda shape: pl.BlockSpec(shape, lambda i: (0, 0), pipeline_mode=pl.Buffered(1))
    return pl.pallas_call(
        _oproj_router_kernel,
        grid=(t // tm,),
        in_specs=[
            pl.BlockSpec((tm, d), lambda i: (i, 0)),
            pl.BlockSpec((tm, k), lambda i: (i, 0)),
            resident((k, d)), resident((1, d)), resident((d, LANES)),
        ],
        out_specs=[
            pl.BlockSpec((tm, d), lambda i: (i, 0)),
            pl.BlockSpec((tm, d // 2), lambda i: (i, 0)),
            pl.BlockSpec((N_EXPERTS, tm), lambda i: (0, i)),
            pl.BlockSpec((N_EXPERTS, tm), lambda i: (0, i)),
            pl.BlockSpec((tm, LANES), lambda i: (i, 0)),
        ],
        out_shape=[
            jax.ShapeDtypeStruct((t, d), F32),
            jax.ShapeDtypeStruct((t, d // 2), I32),
            jax.ShapeDtypeStruct((N_EXPERTS, t), F32),
            jax.ShapeDtypeStruct((N_EXPERTS, t), F32),
            jax.ShapeDtypeStruct((t, LANES), F32),
        ],
        compiler_params=_params(("parallel",), 48),
        name="oproj_router",
    )(h, y, wo, nw.reshape(1, d), wr_pad)


def _rank_kernel(sel_ref, rank_ref, cnt_ref, carry):
    @pl.when(pl.program_id(0) == 0)
    def _():
        carry[...] = jnp.zeros(carry.shape, F32)

    s = sel_ref[...]
    tb = s.shape[1]
    row = lax.broadcasted_iota(I32, (tb, tb), 0)
    col = lax.broadcasted_iota(I32, (tb, tb), 1)
    before = jnp.where(row < col, 1.0, 0.0).astype(BF16)
    c = carry[...]
    rank_ref[...] = jnp.dot(s.astype(BF16), before, preferred_element_type=F32) + c[:, :1]
    c = c + jnp.sum(s, axis=1, keepdims=True)
    carry[...] = c
    cnt_ref[...] = c


def moe_rank(sel, tb):
    ne, t = sel.shape
    return pl.pallas_call(
        _rank_kernel,
        grid=(t // tb,),
        in_specs=[pl.BlockSpec((ne, tb), lambda i: (0, i))],
        out_specs=[pl.BlockSpec((ne, tb), lambda i: (0, i)), pl.BlockSpec((ne, LANES), lambda i: (0, 0))],
        out_shape=[jax.ShapeDtypeStruct((ne, t), F32), jax.ShapeDtypeStruct((ne, LANES), F32)],
        scratch_shapes=[pltpu.VMEM((ne, LANES), F32)],
        compiler_params=_params(("arbitrary",), 32),
        name="moe_rank",
    )(sel)


def _pos_kernel(topt_ref, rank_ref, off_ref, pos_ref):
    posv = rank_ref[...] + off_ref[...][:, :1]
    sub = lax.broadcasted_iota(I32, posv.shape, 0).astype(F32)
    for k in range(TOP_K):
        ek = topt_ref[k:k + 1, :]
        pos_ref[0, k:k + 1, :] = jnp.sum(jnp.where(sub == ek, posv, 0.0), axis=0, keepdims=True).astype(I32)


def moe_pos(topt, rank, off, tb):
    ne, t = rank.shape
    return pl.pallas_call(
        _pos_kernel,
        grid=(t // tb,),
        in_specs=[
            pl.BlockSpec((ne, tb), lambda i: (0, i)),
            pl.BlockSpec((ne, tb), lambda i: (0, i)),
            pl.BlockSpec((ne, LANES), lambda i: (0, 0)),
        ],
        out_specs=pl.BlockSpec((1, TOP_K, tb), lambda i: (i, 0, 0)),
        out_shape=jax.ShapeDtypeStruct((t // tb, TOP_K, tb), I32),
        compiler_params=_params(("parallel",), 32),
        name="moe_pos",
    )(topt, rank, off)


def _expert_ffn_kernel(te_ref, nv_ref, nr_ref, x_ref, wg_ref, wu_ref, wd_ref, o_ref, xn_ref, acc_ref):
    del te_ref
    i = pl.program_id(0)
    j = pl.program_id(1)

    @pl.when(i < nv_ref[0])
    def _():
        @pl.when(j == 0)
        def _():
            xw = x_ref[...]
            live = lax.broadcasted_iota(I32, xw.shape, 0) < nr_ref[i]
            xw = lax.bitcast_convert_type(jnp.where(live, xw, 0), U32)
            xn_ref[...] = _unpack_bf16_pairs(xw).astype(BF16)
            acc_ref[...] = jnp.zeros(acc_ref.shape, F32)

        xn = xn_ref[...]
        g = jnp.dot(xn, wg_ref[0].astype(BF16), preferred_element_type=F32)
        u = jnp.dot(xn, wu_ref[0].astype(BF16), preferred_element_type=F32)
        a = (g * _sigmoid(g) * u).astype(BF16)
        acc_ref[...] += jnp.dot(a, wd_ref[0].astype(BF16), preferred_element_type=F32)

        @pl.when(j == pl.num_programs(1) - 1)
        def _():
            o_ref[...] = lax.bitcast_convert_type(_pack_bf16_pairs(acc_ref[...]), I32)

    @pl.when((i >= nv_ref[0]) & (j == pl.num_programs(1) - 1))
    def _():
        o_ref[...] = jnp.zeros(o_ref.shape, o_ref.dtype)


def expert_ffn(xs, tile_expert, n_valid, tile_rows, wg, wu, wd, tm, tf):
    rows, dw = xs.shape
    ne, d, f = wg.shape
    nf = f // tf
    row_blk = lambda i, j, te, nv, nr: (jnp.maximum(jnp.minimum(i, nv[0] - 1), 0), 0)
    out_blk = lambda i, j, te, nv, nr: (i, 0)
    col_j = lambda i, j, nv: jnp.where(i < nv[0], j, nf - 1)
    return pl.pallas_call(
        _expert_ffn_kernel,
        grid_spec=pltpu.PrefetchScalarGridSpec(
            num_scalar_prefetch=3,
            grid=(rows // tm, nf),
            in_specs=[
                pl.BlockSpec((tm, dw), row_blk),
                pl.BlockSpec((1, d, tf), lambda i, j, te, nv, nr: (te[i], 0, col_j(i, j, nv))),
                pl.BlockSpec((1, d, tf), lambda i, j, te, nv, nr: (te[i], 0, col_j(i, j, nv))),
                pl.BlockSpec((1, tf, d), lambda i, j, te, nv, nr: (te[i], col_j(i, j, nv), 0)),
            ],
            out_specs=pl.BlockSpec((tm, dw), out_blk),
            scratch_shapes=[pltpu.VMEM((tm, d), BF16), pltpu.VMEM((tm, d), F32)],
        ),
        out_shape=jax.ShapeDtypeStruct((rows, dw), I32),
        compiler_params=_params(("arbitrary", "arbitrary"), 48),
        name="expert_ffn",
    )(tile_expert, n_valid, tile_rows, xs, wg, wu, wd)


SC_CORES = 2
SC_SUBCORES = 16
SC_INDEX_WINDOW = 128
SC_GATHER_ROWS = 64


def sc_row_gather(data, idx):
    n = idx.shape[0]
    d = data.shape[1]
    n_workers = SC_CORES * SC_SUBCORES
    per_worker = n // n_workers
    win, half = SC_INDEX_WINDOW, SC_GATHER_ROWS
    assert n % n_workers == 0 and per_worker % win == 0 and win == 2 * half
    mesh = plsc.VectorSubcoreMesh(core_axis_name="c", subcore_axis_name="s")

    @functools.partial(
        pl.kernel, out_type=jax.ShapeDtypeStruct((n, d), data.dtype), mesh=mesh,
        scratch_types=[pltpu.VMEM((win,), I32), pltpu.VMEM((half, d), data.dtype), pltpu.VMEM((half, d), data.dtype),
                       pltpu.SemaphoreType.DMA, pltpu.SemaphoreType.DMA])
    def gather_kernel(x_hbm, i_hbm, o_hbm, idx_v, rows_a, rows_b, sem_a, sem_b):
        wid = lax.axis_index("s") * SC_CORES + lax.axis_index("c")
        base = wid * per_worker

        @pl.loop(0, per_worker // win)
        def _(c):
            off = base + c * win
            pltpu.sync_copy(i_hbm.at[pl.ds(off, win)], idx_v)
            cp_a = pltpu.async_copy(x_hbm.at[idx_v.at[pl.ds(0, half)]], rows_a, sem_a)
            cp_b = pltpu.async_copy(x_hbm.at[idx_v.at[pl.ds(half, half)]], rows_b, sem_b)
            cp_a.wait()
            pltpu.sync_copy(rows_a, o_hbm.at[pl.ds(off, half)])
            cp_b.wait()
            pltpu.sync_copy(rows_b, o_hbm.at[pl.ds(off + half, half)])

    return gather_kernel(data, idx)


def sc_row_scatter(rows, idx, n_out):
    r, d = rows.shape
    n = idx.shape[0]
    n_workers = SC_CORES * SC_SUBCORES
    per_worker = n // n_workers
    win = SC_INDEX_WINDOW
    assert n % n_workers == 0 and per_worker % win == 0 and r % per_worker == 0
    mesh = plsc.VectorSubcoreMesh(core_axis_name="c", subcore_axis_name="s")

    @functools.partial(
        pl.kernel, out_type=jax.ShapeDtypeStruct((n_out, d), rows.dtype), mesh=mesh,
        scratch_types=[pltpu.VMEM((win,), I32), pltpu.VMEM((win, d), rows.dtype)])
    def scatter_kernel(x_hbm, i_hbm, o_hbm, idx_v, rows_v):
        wid = lax.axis_index("s") * SC_CORES + lax.axis_index("c")
        base = wid * per_worker
        src_base = lax.rem(base, r)

        @pl.loop(0, per_worker // win)
        def _(c):
            pltpu.sync_copy(i_hbm.at[pl.ds(base + c * win, win)], idx_v)
            pltpu.sync_copy(x_hbm.at[pl.ds(src_base + c * win, win)], rows_v)
            pltpu.sync_copy(rows_v, o_hbm.at[idx_v])

    return scatter_kernel(rows, idx)


def _combine_dense_ple_kernel(h_ref, top_ref, y1_ref, y2_ref, p_ref, nw_ref, wg_ref, wp_ref, o_ref):
    top = top_ref[...]
    x = h_ref[...]
    for k, y_ref in enumerate((y1_ref, y2_ref)):
        y = _unpack_bf16_pairs(lax.bitcast_convert_type(y_ref[...], U32))
        x = x + top[:, TOP_K + k:TOP_K + k + 1] * y
    o_ref[...] = _ple_update(x, p_ref[...], nw_ref[...], wg_ref[...], wp_ref[...])


def moe_combine_dense_ple(h, top, yg, p, layer, nw, wg, wp, tm):
    t, d = h.shape
    dw = yg.shape[1]
    pd = p.shape[2]
    nt = t // tm
    resident = lambda shape: pl.BlockSpec(shape, lambda i: (0, 0), pipeline_mode=pl.Buffered(1))
    return pl.pallas_call(
        _combine_dense_ple_kernel,
        grid=(nt,),
        in_specs=[
            pl.BlockSpec((tm, d), lambda i: (i, 0)),
            pl.BlockSpec((tm, LANES), lambda i: (i, 0)),
            pl.BlockSpec((tm, dw), lambda i: (i, 0)),
            pl.BlockSpec((tm, dw), lambda i: (i + nt, 0)),
            pl.BlockSpec((None, tm, pd), lambda i: (layer, i, 0)),
            resident((1, d)), resident((d, d)), resident((pd, d)),
        ],
        out_specs=pl.BlockSpec((tm, d), lambda i: (i, 0)),
        out_shape=jax.ShapeDtypeStruct((t, d), F32),
        compiler_params=_params(("parallel",), 48),
        name="moe_combine_dense_ple",
    )(h, top, yg, yg, p, nw.reshape(1, d), wg, wp)


def oproj_moe_ple(h, y, wo, nw, w_router, wg, wu, wd, p, layer, ple_nw, ple_wg, ple_wp, tm_route, tm_expert, tb, tf):
    t, d = h.shape
    h, xpk, topt, sel, top = oproj_router(h, y, wo, nw, _pad_cols(w_router, LANES), tm_route)
    rank, cnt = moe_rank(sel, tb)
    counts = cnt[:, 0].astype(I32)
    padded = ((counts + tm_expert - 1) // tm_expert) * tm_expert
    ends = jnp.cumsum(padded)
    n_tiles = (TOP_K * t) // tm_expert + N_EXPERTS
    n_valid = (ends[-1] // tm_expert).astype(I32)
    tile_start = jnp.arange(n_tiles, dtype=I32) * tm_expert
    first_row = jnp.minimum(tile_start, ends[-1] - 1)
    tile_expert = jnp.minimum(jnp.sum(ends[None, :] <= first_row[:, None], axis=1), N_EXPERTS - 1).astype(I32)
    starts = ends - padded
    tile_rows = jnp.clip(counts[tile_expert] - (tile_start - starts[tile_expert]), 0, tm_expert)
    tile_rows = jnp.where(tile_start < ends[-1], tile_rows, 0).astype(I32)
    off = jnp.broadcast_to(starts.astype(F32)[:, None], (N_EXPERTS, LANES))
    pos = moe_pos(topt, rank, off, tb)
    pos_slot_major = jnp.transpose(pos, (1, 0, 2)).reshape(-1)
    xs = sc_row_scatter(xpk, pos_slot_major, n_tiles * tm_expert)
    ys = expert_ffn(xs, tile_expert, n_valid.reshape(1), tile_rows, wg, wu, wd, tm_expert, tf)
    yg = sc_row_gather(ys, pos_slot_major)
    return moe_combine_dense_ple(h, top, yg, p, layer, ple_nw, ple_wg, ple_wp, tm_route)


def _pad_cols(w, n):
    return jnp.pad(w, ((0, 0), (0, n - w.shape[1])))


def _tile3(v, n):
    return jnp.pad(jnp.tile(v, 3), (0, LANES - 3 * n)).reshape(1, LANES)


def kernel(x, p, ssm_norm_w, ssm_w_in, ssm_conv_w, ssm_conv_b, ssm_dt_bias, ssm_a_log, ssm_d, ssm_gn_w, ssm_w_out, kv_norm_w, w_kv, b_f, k_norm_w, att_norm_w, att_w_q, q_norm_w, att_w_o, ffn_norm_w, ffn_w_gate, ffn_w_up, ffn_w_down, moe_norm_w, moe_w_router, moe_w_gate, moe_w_up, moe_w_down, ple_norm_w, ple_w_gate, ple_w_proj):
    b, l, d = x.shape
    t = b * l
    n_a = ssm_norm_w.shape[0]
    depth = ple_norm_w.shape[0]
    d_inner = ssm_gn_w.shape[1]
    n_heads = ssm_dt_bias.shape[1]
    cdim = ssm_conv_w.shape[2]
    att_w = att_w_q.shape[2]
    n_att_heads = att_w // ATT_HEADDIM
    tm = min(1024, t)
    tq = min(256, l)

    h = x.reshape(t, d)
    p3 = p.reshape(depth, t, p.shape[-1])
    r = jnp.arange(LANES)[:, None]
    c = jnp.arange(d_inner)[None, :]
    expand = ((r < 3 * n_heads) & ((r % n_heads) == (c // SSM_HEADDIM))).astype(BF16)

    kv = ext_q = ext_k = None
    for i in range(depth):
        if i < n_a:
            w_in = ssm_w_in[i].astype(BF16)
            w_dt = w_in[:, d_inner + cdim:]
            w_dt3 = _pad_cols(jnp.concatenate([w_dt, w_dt, w_dt], axis=1), LANES)
            z, xbc, dt_raw = norm_matmul(h, [ssm_norm_w[i]], [w_in[:, :d_inner], w_in[:, d_inner:d_inner + cdim], w_dt3],
                                         [0, 0, 0], [BF16, BF16, F32], tm // 2)
            y = ssd(z.reshape(b, l, d_inner), xbc.reshape(b, l, cdim), dt_raw.reshape(b, l, LANES),
                    ssm_conv_w[i], ssm_conv_b[i].reshape(1, cdim), _tile3(ssm_dt_bias[i], n_heads),
                    _tile3(ssm_a_log[i], n_heads), jnp.repeat(ssm_d[i], SSM_HEADDIM).reshape(1, d_inner),
                    ssm_gn_w[i].reshape(1, d_inner), expand, 2)
            if i % 2 == 0:
                h = matmul_residual(h, y.reshape(t, d_inner), ssm_w_out[i].astype(BF16), tm)
        else:
            j = i - n_a
            w_q = att_w_q[j].astype(BF16)
            if i == n_a:
                wkv = w_kv.astype(BF16)
                w_f = wkv[:, 2 * att_w:]
                w_f3 = _pad_cols(jnp.concatenate([w_f, w_f, w_f], axis=1), LANES)
                kv, f, qp = norm_matmul(h, [kv_norm_w, att_norm_w[j]], [wkv[:, :2 * att_w], w_f3, w_q], [0, 0, 1],
                                        [BF16, F32, BF16], tm)
                kv = kv.reshape(b, l, 2 * att_w)
                ext_q, ext_k = forget_ext(f.reshape(b, l, LANES), _tile3(b_f, n_att_heads), n_att_heads)
            else:
                qp, = norm_matmul(h, [att_norm_w[j]], [w_q], [0], [BF16], tm)
            qp = qp.reshape(b, l, att_w)
            qnw2 = jnp.tile(q_norm_w[j], 2).reshape(1, LANES)
            knw2 = jnp.tile(k_norm_w, 2).reshape(1, LANES)
            o = fox_attention(qp, kv, ext_q, ext_k, qnw2, knw2, att_w, tq).reshape(t, att_w)
            if i % 2 == 0:
                h = matmul_residual(h, o, att_w_o[j].astype(BF16), tm)
        ple_wg, ple_wp = ple_w_gate[i].astype(BF16), ple_w_proj[i].astype(BF16)
        if i % 2 == 0:
            dd = i // 2
            h = ffn_ple(h, ffn_norm_w[dd], ffn_w_gate[dd].astype(BF16), ffn_w_up[dd].astype(BF16),
                        ffn_w_down[dd].astype(BF16), p3, i, ple_norm_w[i], ple_wg, ple_wp, tm, 512)
        else:
            m = i // 2
            if i < n_a:
                mix, w_mix = y.reshape(t, d_inner), ssm_w_out[i].astype(BF16)
            else:
                mix, w_mix = o, att_w_o[i - n_a].astype(BF16)
            h = oproj_moe_ple(h, mix, w_mix, moe_norm_w[m], moe_w_router[m], moe_w_gate[m], moe_w_up[m], moe_w_down[m],
                              p3, i, ple_norm_w[i], ple_wg, ple_wp, tm, tm, min(512, t), 512)
    return h.reshape(b, l, d)
```

```python
import functools
import math

import numpy as np
import jax
import jax.numpy as jnp
from jax import lax
from jax.experimental import pallas as pl
from jax.experimental.pallas import tpu as pltpu
from jax.experimental.pallas import tpu_sc as plsc

F32 = jnp.float32
BF16 = jnp.bfloat16
I32 = jnp.int32
U32 = jnp.uint32
EPS = 1e-6
HIGHEST = lax.Precision.HIGHEST
LOG2E = math.log2(math.e)

LANES = 128
SUBLANES = 8
MIB = 1024 * 1024

SSM_HEADDIM = 64
SSM_GROUPS = 4
SSM_STATE = 128
CONV_WIDTH = 4
CHUNK = 128
ATT_HEADDIM = 64
N_EXPERTS = 8
TOP_K = 2


def _params(sem, vmem_mib):
    return pltpu.CompilerParams(dimension_semantics=sem, vmem_limit_bytes=vmem_mib * MIB)


def _rms_scale(x):
    return lax.rsqrt(jnp.mean(x * x, axis=-1, keepdims=True) + EPS)


def _sigmoid(x):
    return 1.0 / (1.0 + jnp.exp2(x * (-LOG2E)))


def _softplus(x):
    return jnp.maximum(x, 0.0) + jnp.log(1.0 + jnp.exp(-jnp.abs(x)))


def _pack_bf16_pairs(x):
    n = x.shape[1] // 2
    bits = lax.bitcast_convert_type(x.astype(BF16).astype(F32), U32)
    return lax.shift_right_logical(bits[:, :n], jnp.uint32(16)) | (bits[:, n:] & jnp.uint32(0xFFFF0000))


def _unpack_bf16_pairs(w):
    lo = lax.bitcast_convert_type(lax.shift_left(w, jnp.uint32(16)), F32)
    hi = lax.bitcast_convert_type(w & jnp.uint32(0xFFFF0000), F32)
    return jnp.concatenate([lo, hi], axis=1)


def _norm_matmul_kernel(h_ref, nw_ref, *refs, norm_of):
    n = len(refs) // 2
    x = h_ref[...]
    xs = x * _rms_scale(x)
    xn = [(xs * nw_ref[g:g + 1, :]).astype(BF16) for g in range(nw_ref.shape[0])]
    for w_ref, o_ref, g in zip(refs[:n], refs[n:], norm_of):
        o_ref[...] = jnp.dot(xn[g], w_ref[...], preferred_element_type=F32).astype(o_ref.dtype)


def norm_matmul(h, nws, ws, norm_of, out_dtypes, tm):
    t, d = h.shape
    resident = lambda shape: pl.BlockSpec(shape, lambda i: (0, 0), pipeline_mode=pl.Buffered(1))
    return pl.pallas_call(
        functools.partial(_norm_matmul_kernel, norm_of=tuple(norm_of)),
        grid=(t // tm,),
        in_specs=[pl.BlockSpec((tm, d), lambda i: (i, 0)), resident((len(nws), d))] + [resident(w.shape) for w in ws],
        out_specs=[pl.BlockSpec((tm, w.shape[1]), lambda i: (i, 0)) for w in ws],
        out_shape=[jax.ShapeDtypeStruct((t, w.shape[1]), dt) for w, dt in zip(ws, out_dtypes)],
        compiler_params=_params(("parallel",), 48),
        name="norm_matmul",
    )(h, jnp.stack(nws), *ws)


def _matmul_residual_kernel(h_ref, y_ref, w_ref, o_ref):
    o_ref[...] = h_ref[...] + jnp.dot(y_ref[...], w_ref[...], preferred_element_type=F32)


def matmul_residual(h, y, w, tm):
    t, n = h.shape
    k = y.shape[1]
    return pl.pallas_call(
        _matmul_residual_kernel,
        grid=(t // tm,),
        in_specs=[
            pl.BlockSpec((tm, n), lambda i: (i, 0)),
            pl.BlockSpec((tm, k), lambda i: (i, 0)),
            pl.BlockSpec((k, n), lambda i: (0, 0)),
        ],
        out_specs=pl.BlockSpec((tm, n), lambda i: (i, 0)),
        out_shape=jax.ShapeDtypeStruct((t, n), F32),
        compiler_params=_params(("parallel",), 48),
        name="matmul_residual",
    )(h, y, w)


def _ple_update(x, p, nw, wg, wp):
    xn = (x * _rms_scale(x) * nw).astype(BF16)
    gate = _sigmoid(jnp.dot(xn, wg, preferred_element_type=F32))
    proj = jnp.dot(p.astype(BF16), wp, preferred_element_type=F32)
    return x + proj * gate


def _ffn_ple_kernel(h_ref, nw_ref, wg_ref, wu_ref, wd_ref, p_ref, pnw_ref, pwg_ref, pwp_ref, o_ref, xn_ref, acc_ref):
    j = pl.program_id(1)

    @pl.when(j == 0)
    def _():
        x = h_ref[...]
        xn_ref[...] = (x * _rms_scale(x) * nw_ref[...]).astype(xn_ref.dtype)
        acc_ref[...] = x

    xn = xn_ref[...]
    g = jnp.dot(xn, wg_ref[...], preferred_element_type=F32)
    u = jnp.dot(xn, wu_ref[...], preferred_element_type=F32)
    a = (g * _sigmoid(g) * u).astype(BF16)
    acc_ref[...] += jnp.dot(a, wd_ref[...], preferred_element_type=F32)

    @pl.when(j == pl.num_programs(1) - 1)
    def _():
        o_ref[...] = _ple_update(acc_ref[...], p_ref[...], pnw_ref[...], pwg_ref[...], pwp_ref[...])


def ffn_ple(h, nw, wg, wu, wd, p, layer, ple_nw, ple_wg, ple_wp, tm, tf):
    t, d = h.shape
    f = wg.shape[1]
    pd = p.shape[2]
    resident = lambda shape: pl.BlockSpec(shape, lambda i, j: (0, 0), pipeline_mode=pl.Buffered(1))
    return pl.pallas_call(
        _ffn_ple_kernel,
        grid=(t // tm, f // tf),
        in_specs=[
            pl.BlockSpec((tm, d), lambda i, j: (i, 0)),
            resident((1, d)),
            pl.BlockSpec((d, tf), lambda i, j: (0, j)),
            pl.BlockSpec((d, tf), lambda i, j: (0, j)),
            pl.BlockSpec((tf, d), lambda i, j: (j, 0)),
            pl.BlockSpec((None, tm, pd), lambda i, j: (layer, i, 0)),
            resident((1, d)), resident((d, d)), resident((pd, d)),
        ],
        out_specs=pl.BlockSpec((tm, d), lambda i, j: (i, 0)),
        out_shape=jax.ShapeDtypeStruct((t, d), F32),
        scratch_shapes=[pltpu.VMEM((tm, d), BF16), pltpu.VMEM((tm, d), F32)],
        compiler_params=_params(("parallel", "arbitrary"), 52),
        name="ffn_ple",
    )(h, nw.reshape(1, d), wg, wu, wd, p, ple_nw.reshape(1, d), ple_wg, ple_wp)


def _split3(v, n):
    v1 = v.astype(BF16)
    r1 = v - v1.astype(F32)
    v2 = r1.astype(BF16)
    v3 = (r1 - v2.astype(F32)).astype(BF16)
    lane = lax.broadcasted_iota(I32, v.shape, 1)
    return jnp.where(lane < n, v1, jnp.where(lane < 2 * n, v2, v3))


def _ssd_kernel(z_ref, xbc_ref, dt_ref, cw_ref, cb_ref, dtb_ref, alog_ref, dskip_ref, gnw_ref, exp_ref,
                o_ref, cbuf, xact, state, acs_t, y_scr):
    q = CHUNK
    tail = cbuf.shape[0] - q

    @pl.when(pl.program_id(1) == 0)
    def _():
        cbuf[0:tail, :] = jnp.zeros((tail, cbuf.shape[1]), BF16)
        state[...] = jnp.zeros(state.shape, F32)

    for sub in range(z_ref.shape[1] // q):
        rows = slice(sub * q, (sub + 1) * q)
        _ssd_chunk(z_ref.at[0, rows], xbc_ref.at[0, rows], dt_ref.at[0, rows], cw_ref, cb_ref, dtb_ref, alog_ref,
                   dskip_ref, gnw_ref, exp_ref, o_ref.at[0, rows], cbuf, xact, state, acs_t, y_scr)


def _ssd_chunk(z_ref, xbc_ref, dt_ref, cw_ref, cb_ref, dtb_ref, alog_ref, dskip_ref, gnw_ref, exp_ref,
               o_ref, cbuf, xact, state, acs_t, y_scr):
    d_inner = z_ref.shape[-1]
    gw = d_inner // SSM_GROUPS
    n_heads = d_inner // SSM_HEADDIM
    q = CHUNK
    tail = cbuf.shape[0] - q

    cbuf[tail:, :] = xbc_ref[...]
    cdim = cbuf.shape[1]
    nshift = CONV_WIDTH - 1
    srow = lax.broadcasted_iota(I32, (nshift * q, tail + q), 0)
    scol = lax.broadcasted_iota(I32, (nshift * q, tail + q), 1)
    shift = jnp.where(scol == srow - (srow // q) * (q - 1) + (tail - nshift), 1.0, 0.0).astype(BF16)
    cw_lanes = 2 * LANES
    for j in range(cdim // cw_lanes):
        sl = slice(j * cw_lanes, (j + 1) * cw_lanes)
        shifted = jnp.dot(shift, cbuf[:, sl], preferred_element_type=F32)
        acc = cb_ref[:, sl] + cbuf[tail:, sl].astype(F32) * cw_ref[nshift:nshift + 1, sl]
        for w in range(nshift):
            acc = acc + shifted[w * q:(w + 1) * q] * cw_ref[w:w + 1, sl]
        xact[:, sl] = acc * _sigmoid(acc)
    cbuf[0:tail, :] = cbuf[q:q + tail, :]

    dt = _softplus(dt_ref[...] + dtb_ref[...])
    a = dt * (-LOG2E * jnp.exp(alog_ref[...]))
    row = lax.broadcasted_iota(I32, (q, q), 0)
    col = lax.broadcasted_iota(I32, (q, q), 1)
    causal = row >= col
    tril = jnp.where(causal, 1.0, 0.0).astype(F32)
    acs = jnp.dot(tril, a, preferred_element_type=F32, precision=HIGHEST)
    acs_t[...] = acs.T
    a_last = acs[q - 1:q, :]
    dt3 = _split3(dt, n_heads)
    dec3 = _split3(jnp.exp2(a_last - acs), n_heads)
    e3 = _split3(jnp.exp2(acs), n_heads)

    for g in range(SSM_GROUPS):
        gl = slice(g * gw, (g + 1) * gw)
        ex = exp_ref[:, gl]
        xs = xact[:, gl]
        b_g = xact[:, d_inner + g * SSM_STATE:d_inner + (g + 1) * SSM_STATE].astype(BF16)
        c_g = xact[:, d_inner + (SSM_GROUPS + g) * SSM_STATE:d_inner + (SSM_GROUPS + g + 1) * SSM_STATE].astype(BF16)
        dt_e = jnp.dot(dt3, ex, preferred_element_type=F32)
        dec_e = jnp.dot(dec3, ex, preferred_element_type=F32)
        e_e = jnp.dot(e3, ex, preferred_element_type=F32)
        xc = xs * dt_e
        xc_b = xc.astype(BF16)
        xdec_b = (xc * dec_e).astype(BF16)
        scores = lax.dot_general(c_g, b_g, (((1,), (1,)), ((), ())), preferred_element_type=F32)
        st = state[g]
        y = jnp.dot(c_g, st.astype(BF16), preferred_element_type=F32) * e_e
        y = y + xs * dskip_ref[:, gl]
        lane = lax.broadcasted_iota(I32, (q, LANES), 1)
        ydiag = []
        for pr in range(gw // LANES):
            ms = []
            for hh in range(2):
                h = g * (gw // SSM_HEADDIM) + 2 * pr + hh
                seg = acs[:, h:h + 1] - acs_t[h:h + 1, :]
                ms.append(scores * jnp.exp2(jnp.where(causal, seg, -jnp.inf)))
            lhs = jnp.concatenate(ms, axis=1).astype(BF16)
            xp = xc_b[:, pr * LANES:(pr + 1) * LANES]
            zero = jnp.zeros_like(xp)
            rhs = jnp.concatenate([jnp.where(lane < SSM_HEADDIM, xp, zero),
                                   jnp.where(lane >= SSM_HEADDIM, xp, zero)], axis=0)
            ydiag.append(jnp.dot(lhs, rhs, preferred_element_type=F32))
        y_scr[:, gl] = y + jnp.concatenate(ydiag, axis=1)
        upd = lax.dot_general(b_g, xdec_b, (((0,), (0,)), ((), ())), preferred_element_type=F32)
        state[g] = st * e_e[q - 1:q, :] + upd

    zz = z_ref[...].astype(F32)
    y = y_scr[...] * (zz * _sigmoid(zz))
    o_ref[...] = (y * _rms_scale(y) * gnw_ref[...]).astype(o_ref.dtype)


def ssd(z, xbc, dt_raw, conv_w, conv_b, dt_bias3, a_log3, dskip_e, gn_w, expand, chunks_per_step):
    b, l, d_inner = z.shape
    cdim = xbc.shape[-1]
    gw = d_inner // SSM_GROUPS
    rows = chunks_per_step * CHUNK
    nc = l // rows
    full = lambda shape: pl.BlockSpec(shape, lambda i, c: (0,) * len(shape))
    return pl.pallas_call(
        _ssd_kernel,
        grid=(b, nc),
        in_specs=[
            pl.BlockSpec((1, rows, d_inner), lambda i, c: (i, c, 0)),
            pl.BlockSpec((1, rows, cdim), lambda i, c: (i, c, 0)),
            pl.BlockSpec((1, rows, LANES), lambda i, c: (i, c, 0)),
            full((CONV_WIDTH, cdim)), full((1, cdim)), full((1, LANES)), full((1, LANES)),
            full((1, d_inner)), full((1, d_inner)), full((LANES, d_inner)),
        ],
        out_specs=pl.BlockSpec((1, rows, d_inner), lambda i, c: (i, c, 0)),
        out_shape=jax.ShapeDtypeStruct((b, l, d_inner), BF16),
        scratch_shapes=[
            pltpu.VMEM((CHUNK + 16, cdim), BF16),
            pltpu.VMEM((CHUNK, cdim), F32),
            pltpu.VMEM((SSM_GROUPS, SSM_STATE, gw), F32),
            pltpu.VMEM((LANES, CHUNK), F32),
            pltpu.VMEM((CHUNK, d_inner), F32),
        ],
        compiler_params=_params(("parallel", "arbitrary"), 40),
        name="ssd",
    )(z, xbc, dt_raw, conv_w, conv_b, dt_bias3, a_log3, dskip_e, gn_w, expand)


def _cum_kernel(f_ref, bf_ref, pq_ref, pk_ref, eq_ref, ek_ref, *, blk, n_heads):
    l = f_ref.shape[1]
    row = lax.broadcasted_iota(I32, (blk, blk), 0)
    col = lax.broadcasted_iota(I32, (blk, blk), 1)
    tril = jnp.where(row >= col, 1.0, 0.0).astype(F32)
    lane = lax.broadcasted_iota(I32, (blk, LANES), 1)
    carry = jnp.zeros((1, LANES), F32)
    for i in range(l // blk):
        rs = slice(i * blk, (i + 1) * blk)
        x = f_ref[0, rs, :] + bf_ref[...]
        lf = -_softplus(-x)
        cs = jnp.dot(tril, lf, preferred_element_type=F32, precision=HIGHEST) + carry
        carry = cs[blk - 1:blk, :]
        lhs = jnp.where(lane == 3 * n_heads, jnp.ones((), BF16), _split3(cs * LOG2E, n_heads))
        eq_ref[0, rs, :] = jnp.dot(lhs, pq_ref[...], preferred_element_type=F32).astype(BF16)
        ek_ref[0, rs, :] = jnp.dot(lhs, pk_ref[...], preferred_element_type=F32).astype(BF16)


def _ext_placement(n_heads):
    w = n_heads * ATT_HEADDIM
    pq = np.zeros((LANES, w), np.float32)
    pk = np.zeros((LANES, w), np.float32)
    one = 3 * n_heads
    for h in range(n_heads):
        base = (h // 2) * LANES + (ATT_HEADDIM if h % 2 == 0 else 0)
        for piece in range(3):
            pq[piece * n_heads + h, base + piece] = 1.0
            pq[one, base + 3 + piece] = 1.0
            pk[one, base + piece] = 1.0
            pk[piece * n_heads + h, base + 3 + piece] = -1.0
    return jnp.asarray(pq, BF16), jnp.asarray(pk, BF16)


def forget_ext(f, b_f3, n_heads):
    b, l, _ = f.shape
    w = n_heads * ATT_HEADDIM
    pq, pk = _ext_placement(n_heads)
    out = jax.ShapeDtypeStruct((b, l, w), BF16)
    return pl.pallas_call(
        functools.partial(_cum_kernel, blk=256, n_heads=n_heads),
        grid=(b,),
        in_specs=[
            pl.BlockSpec((1, l, LANES), lambda i: (i, 0, 0)),
            pl.BlockSpec((1, LANES), lambda i: (0, 0)),
            pl.BlockSpec((LANES, w), lambda i: (0, 0)),
            pl.BlockSpec((LANES, w), lambda i: (0, 0)),
        ],
        out_specs=[pl.BlockSpec((1, l, w), lambda i: (i, 0, 0)), pl.BlockSpec((1, l, w), lambda i: (i, 0, 0))],
        out_shape=[out, out],
        compiler_params=_params(("parallel",), 40),
        name="forget_ext",
    )(f, b_f3, pq, pk)


def _head_rms(x, lane_lo):
    sq = x * x
    s_lo = jnp.sum(jnp.where(lane_lo, sq, 0.0), axis=-1, keepdims=True)
    s_hi = jnp.sum(jnp.where(lane_lo, 0.0, sq), axis=-1, keepdims=True)
    inv = 1.0 / ATT_HEADDIM
    return jnp.where(lane_lo, lax.rsqrt(s_lo * inv + EPS), lax.rsqrt(s_hi * inv + EPS))


def _attn_kernel(q_ref, k_ref, v_ref, eq_ref, ek_ref, qnw_ref, knw_ref, o_ref, kaug, vaugt, s_scr, p_scr, *, tq):
    l = q_ref.shape[1]
    nblk = l // tq
    hd = ATT_HEADDIM
    lane_lo = lax.broadcasted_iota(I32, (tq, LANES), 1) < hd
    sub_lo = lax.broadcasted_iota(I32, (LANES, tq), 0) < hd
    row = lax.broadcasted_iota(I32, (tq, tq), 0)
    col = lax.broadcasted_iota(I32, (tq, tq), 1)
    visible = row <= col
    eye_q = jnp.where(row == col, 1.0, 0.0).astype(BF16)
    eye_l = eye_q[:LANES, :LANES]
    nt = (((1,), (1,)), ((), ()))
    qscale = qnw_ref[...] * (hd ** -0.5 * LOG2E)

    for i in range(nblk):
        rs = slice(i * tq, (i + 1) * tq)
        kk = k_ref[0, rs, :].astype(F32)
        kn = (kk * _head_rms(kk, lane_lo) * knw_ref[...]).astype(BF16)
        ek = ek_ref[0, rs, :]
        kaug[0, rs, :] = jnp.where(lane_lo, kn, ek)
        kaug[1, rs, :] = jnp.where(lane_lo, ek, kn)
        vt = lax.dot_general(eye_l, v_ref[0, rs, :], nt, preferred_element_type=F32).astype(BF16)
        one = jnp.ones_like(vt)
        vaugt[0, :, rs] = jnp.where(sub_lo, vt, one)
        vaugt[1, :, rs] = jnp.where(sub_lo, one, vt)

    def scores(qi):
        qs = slice(qi * tq, (qi + 1) * tq)
        nk = (qi + 1) * tq
        qq = q_ref[0, qs, :].astype(F32)
        qn = (qq * _head_rms(qq, lane_lo) * qscale).astype(BF16)
        eq = eq_ref[0, qs, :]
        for hh in range(2):
            qa = jnp.where(lane_lo, qn, eq) if hh == 0 else jnp.where(lane_lo, eq, qn)
            qt = lax.dot_general(eye_l, qa, nt, preferred_element_type=F32).astype(BF16)
            s = jnp.dot(kaug[hh, :nk, :], qt, preferred_element_type=F32)
            s_scr[qi % 2, hh, :nk, :] = s

    def softmax_pv(qi):
        qs = slice(qi * tq, (qi + 1) * tq)
        nk = (qi + 1) * tq
        par = qi % 2
        accs = []
        for hh in range(2):
            s_scr[par, hh, nk - tq:nk, :] = jnp.where(visible, s_scr[par, hh, nk - tq:nk, :], -jnp.inf)
            m = jnp.max(s_scr[par, hh, :nk, :], axis=0, keepdims=True)
            p_scr[par, hh, :nk, :] = jnp.exp2(s_scr[par, hh, :nk, :] - m).astype(BF16)
            accs.append(jnp.dot(vaugt[hh, :, :nk], p_scr[par, hh, :nk, :], preferred_element_type=F32))
        a0, a1 = accs
        o_t = jnp.concatenate([a0[:hd] / a0[hd:], a1[hd:] / a1[:hd]], axis=0).astype(BF16)
        o_ref[0, qs, :] = lax.dot_general(eye_q, o_t, nt, preferred_element_type=F32).astype(o_ref.dtype)

    scores(0)
    for qi in range(nblk):
        if qi + 1 < nblk:
            scores(qi + 1)
        softmax_pv(qi)


def fox_attention(qp, kv, ext_q, ext_k, qnw2, knw2, att_w, tq):
    b, l, _ = qp.shape
    npair = att_w // LANES
    nblk = l // tq
    blk = lambda off: pl.BlockSpec((1, l, LANES), lambda i, j: (i, 0, j + off))
    return pl.pallas_call(
        functools.partial(_attn_kernel, tq=tq),
        grid=(b, npair),
        in_specs=[
            blk(0), blk(0), blk(npair), blk(0), blk(0),
            pl.BlockSpec((1, LANES), lambda i, j: (0, 0)),
            pl.BlockSpec((1, LANES), lambda i, j: (0, 0)),
        ],
        out_specs=blk(0),
        out_shape=jax.ShapeDtypeStruct((b, l, att_w), BF16),
        scratch_shapes=[
            pltpu.VMEM((2, l, LANES), BF16),
            pltpu.VMEM((2, LANES, l), BF16),
            pltpu.VMEM((2, 2, l, tq), F32),
            pltpu.VMEM((2, 2, l, tq), BF16),
        ],
        compiler_params=_params(("parallel", "parallel"), 40),
        name="fox_attention",
    )(qp, kv, kv, ext_q, ext_k, qnw2, knw2)


def _router_kernel(h_ref, nw_ref, wr_ref, xpk_ref, topt_ref, sel_ref, top_ref):
    x = h_ref[...]
    u = x * _rms_scale(x) * nw_ref[...]
    xpk_ref[...] = lax.bitcast_convert_type(_pack_bf16_pairs(u), I32)
    logits = jnp.dot(u, wr_ref[...], preferred_element_type=F32, precision=HIGHEST)
    lt = logits.T[:N_EXPERTS, :]
    sub = lax.broadcasted_iota(I32, lt.shape, 0)
    neg = jnp.float32(-jnp.inf)
    m1 = jnp.max(lt, axis=0, keepdims=True)
    i1 = jnp.min(jnp.where(lt == m1, sub, N_EXPERTS), axis=0, keepdims=True)
    rest = jnp.where(sub == i1, neg, lt)
    m2 = jnp.max(rest, axis=0, keepdims=True)
    i2 = jnp.min(jnp.where(rest == m2, sub, N_EXPERTS), axis=0, keepdims=True)
    e2 = jnp.exp(m2 - m1)
    g1 = 1.0 / (1.0 + e2)
    g2 = e2 / (1.0 + e2)
    sel_ref[...] = jnp.where((sub == i1) | (sub == i2), 1.0, 0.0).astype(F32)
    topt = jnp.where(sub == 0, i1.astype(F32),
                     jnp.where(sub == 1, i2.astype(F32), jnp.where(sub == 2, g1, jnp.where(sub == 3, g2, 0.0))))
    topt_ref[...] = topt
    pad = jnp.zeros((LANES - N_EXPERTS, topt.shape[1]), F32)
    top_ref[...] = jnp.concatenate([topt, pad], axis=0).T


def router(h, nw, wr_pad, tm):
    t, d = h.shape
    return pl.pallas_call(
        _router_kernel,
        grid=(t // tm,),
        in_specs=[
            pl.BlockSpec((tm, d), lambda i: (i, 0)),
            pl.BlockSpec((1, d), lambda i: (0, 0)),
            pl.BlockSpec((d, LANES), lambda i: (0, 0)),
        ],
        out_specs=[
            pl.BlockSpec((tm, d // 2), lambda i: (i, 0)),
            pl.BlockSpec((N_EXPERTS, tm), lambda i: (0, i)),
            pl.BlockSpec((N_EXPERTS, tm), lambda i: (0, i)),
            pl.BlockSpec((tm, LANES), lambda i: (i, 0)),
        ],
        out_shape=[
            jax.ShapeDtypeStruct((t, d // 2), I32),
            jax.ShapeDtypeStruct((N_EXPERTS, t), F32),
            jax.ShapeDtypeStruct((N_EXPERTS, t), F32),
            jax.ShapeDtypeStruct((t, LANES), F32),
        ],
        compiler_params=_params(("parallel",), 40),
        name="router",
    )(h, nw.reshape(1, d), wr_pad)


def _rank_kernel(sel_ref, rank_ref, cnt_ref, carry):
    @pl.when(pl.program_id(0) == 0)
    def _():
        carry[...] = jnp.zeros(carry.shape, F32)

    s = sel_ref[...]
    tb = s.shape[1]
    row = lax.broadcasted_iota(I32, (tb, tb), 0)
    col = lax.broadcasted_iota(I32, (tb, tb), 1)
    before = jnp.where(row < col, 1.0, 0.0).astype(BF16)
    c = carry[...]
    rank_ref[...] = jnp.dot(s.astype(BF16), before, preferred_element_type=F32) + c[:, :1]
    c = c + jnp.sum(s, axis=1, keepdims=True)
    carry[...] = c
    cnt_ref[...] = c


def moe_rank(sel, tb):
    ne, t = sel.shape
    return pl.pallas_call(
        _rank_kernel,
        grid=(t // tb,),
        in_specs=[pl.BlockSpec((ne, tb), lambda i: (0, i))],
        out_specs=[pl.BlockSpec((ne, tb), lambda i: (0, i)), pl.BlockSpec((ne, LANES), lambda i: (0, 0))],
        out_shape=[jax.ShapeDtypeStruct((ne, t), F32), jax.ShapeDtypeStruct((ne, LANES), F32)],
        scratch_shapes=[pltpu.VMEM((ne, LANES), F32)],
        compiler_params=_params(("arbitrary",), 32),
        name="moe_rank",
    )(sel)


def _pos_kernel(topt_ref, rank_ref, off_ref, pos_ref):
    posv = rank_ref[...] + off_ref[...][:, :1]
    sub = lax.broadcasted_iota(I32, posv.shape, 0).astype(F32)
    for k in range(TOP_K):
        ek = topt_ref[k:k + 1, :]
        pos_ref[0, k:k + 1, :] = jnp.sum(jnp.where(sub == ek, posv, 0.0), axis=0, keepdims=True).astype(I32)


def moe_pos(topt, rank, off, tb):
    ne, t = rank.shape
    return pl.pallas_call(
        _pos_kernel,
        grid=(t // tb,),
        in_specs=[
            pl.BlockSpec((ne, tb), lambda i: (0, i)),
            pl.BlockSpec((ne, tb), lambda i: (0, i)),
            pl.BlockSpec((ne, LANES), lambda i: (0, 0)),
        ],
        out_specs=pl.BlockSpec((1, TOP_K, tb), lambda i: (i, 0, 0)),
        out_shape=jax.ShapeDtypeStruct((t // tb, TOP_K, tb), I32),
        compiler_params=_params(("parallel",), 32),
        name="moe_pos",
    )(topt, rank, off)


def _expert_ffn_kernel(te_ref, nv_ref, nr_ref, x_ref, wg_ref, wu_ref, wd_ref, o_ref, xn_ref, acc_ref):
    del te_ref
    i = pl.program_id(0)
    j = pl.program_id(1)

    @pl.when(i < nv_ref[0])
    def _():
        @pl.when(j == 0)
        def _():
            xw = x_ref[...]
            live = lax.broadcasted_iota(I32, xw.shape, 0) < nr_ref[i]
            xw = lax.bitcast_convert_type(jnp.where(live, xw, 0), U32)
            xn_ref[...] = _unpack_bf16_pairs(xw).astype(BF16)
            acc_ref[...] = jnp.zeros(acc_ref.shape, F32)

        xn = xn_ref[...]
        g = jnp.dot(xn, wg_ref[0].astype(BF16), preferred_element_type=F32)
        u = jnp.dot(xn, wu_ref[0].astype(BF16), preferred_element_type=F32)
        a = (g * _sigmoid(g) * u).astype(BF16)
        acc_ref[...] += jnp.dot(a, wd_ref[0].astype(BF16), preferred_element_type=F32)

        @pl.when(j == pl.num_programs(1) - 1)
        def _():
            o_ref[...] = lax.bitcast_convert_type(_pack_bf16_pairs(acc_ref[...]), I32)

    @pl.when((i >= nv_ref[0]) & (j == pl.num_programs(1) - 1))
    def _():
        o_ref[...] = jnp.zeros(o_ref.shape, o_ref.dtype)


def expert_ffn(xs, tile_expert, n_valid, tile_rows, wg, wu, wd, tm, tf):
    rows, dw = xs.shape
    ne, d, f = wg.shape
    nf = f // tf
    row_blk = lambda i, j, te, nv, nr: (jnp.maximum(jnp.minimum(i, nv[0] - 1), 0), 0)
    out_blk = lambda i, j, te, nv, nr: (i, 0)
    col_j = lambda i, j, nv: jnp.where(i < nv[0], j, nf - 1)
    return pl.pallas_call(
        _expert_ffn_kernel,
        grid_spec=pltpu.PrefetchScalarGridSpec(
            num_scalar_prefetch=3,
            grid=(rows // tm, nf),
            in_specs=[
                pl.BlockSpec((tm, dw), row_blk),
                pl.BlockSpec((1, d, tf), lambda i, j, te, nv, nr: (te[i], 0, col_j(i, j, nv))),
                pl.BlockSpec((1, d, tf), lambda i, j, te, nv, nr: (te[i], 0, col_j(i, j, nv))),
                pl.BlockSpec((1, tf, d), lambda i, j, te, nv, nr: (te[i], col_j(i, j, nv), 0)),
            ],
            out_specs=pl.BlockSpec((tm, dw), out_blk),
            scratch_shapes=[pltpu.VMEM((tm, d), BF16), pltpu.VMEM((tm, d), F32)],
        ),
        out_shape=jax.ShapeDtypeStruct((rows, dw), I32),
        compiler_params=_params(("arbitrary", "arbitrary"), 48),
        name="expert_ffn",
    )(tile_expert, n_valid, tile_rows, xs, wg, wu, wd)


SC_CORES = 2
SC_SUBCORES = 16
SC_INDEX_WINDOW = 128
SC_GATHER_ROWS = 64


def sc_row_gather(data, idx):
    n = idx.shape[0]
    d = data.shape[1]
    n_workers = SC_CORES * SC_SUBCORES
    per_worker = n // n_workers
    win, half = SC_INDEX_WINDOW, SC_GATHER_ROWS
    assert n % n_workers == 0 and per_worker % win == 0 and win == 2 * half
    mesh = plsc.VectorSubcoreMesh(core_axis_name="c", subcore_axis_name="s")

    @functools.partial(
        pl.kernel, out_type=jax.ShapeDtypeStruct((n, d), data.dtype), mesh=mesh,
        scratch_types=[pltpu.VMEM((win,), I32), pltpu.VMEM((half, d), data.dtype), pltpu.VMEM((half, d), data.dtype),
                       pltpu.SemaphoreType.DMA, pltpu.SemaphoreType.DMA])
    def gather_kernel(x_hbm, i_hbm, o_hbm, idx_v, rows_a, rows_b, sem_a, sem_b):
        wid = lax.axis_index("s") * SC_CORES + lax.axis_index("c")
        base = wid * per_worker

        @pl.loop(0, per_worker // win)
        def _(c):
            off = base + c * win
            pltpu.sync_copy(i_hbm.at[pl.ds(off, win)], idx_v)
            cp_a = pltpu.async_copy(x_hbm.at[idx_v.at[pl.ds(0, half)]], rows_a, sem_a)
            cp_b = pltpu.async_copy(x_hbm.at[idx_v.at[pl.ds(half, half)]], rows_b, sem_b)
            cp_a.wait()
            pltpu.sync_copy(rows_a, o_hbm.at[pl.ds(off, half)])
            cp_b.wait()
            pltpu.sync_copy(rows_b, o_hbm.at[pl.ds(off + half, half)])

    return gather_kernel(data, idx)


def sc_row_scatter(rows, idx, n_out):
    r, d = rows.shape
    n = idx.shape[0]
    n_workers = SC_CORES * SC_SUBCORES
    per_worker = n // n_workers
    win = SC_INDEX_WINDOW
    assert n % n_workers == 0 and per_worker % win == 0 and r % per_worker == 0
    mesh = plsc.VectorSubcoreMesh(core_axis_name="c", subcore_axis_name="s")

    @functools.partial(
        pl.kernel, out_type=jax.ShapeDtypeStruct((n_out, d), rows.dtype), mesh=mesh,
        scratch_types=[pltpu.VMEM((win,), I32), pltpu.VMEM((win, d), rows.dtype)])
    def scatter_kernel(x_hbm, i_hbm, o_hbm, idx_v, rows_v):
        wid = lax.axis_index("s") * SC_CORES + lax.axis_index("c")
        base = wid * per_worker
        src_base = lax.rem(base, r)

        @pl.loop(0, per_worker // win)
        def _(c):
            pltpu.sync_copy(i_hbm.at[pl.ds(base + c * win, win)], idx_v)
            pltpu.sync_copy(x_hbm.at[pl.ds(src_base + c * win, win)], rows_v)
            pltpu.sync_copy(rows_v, o_hbm.at[idx_v])

    return scatter_kernel(rows, idx)


def _combine_dense_ple_kernel(h_ref, top_ref, y1_ref, y2_ref, p_ref, nw_ref, wg_ref, wp_ref, o_ref):
    top = top_ref[...]
    x = h_ref[...]
    for k, y_ref in enumerate((y1_ref, y2_ref)):
        y = _unpack_bf16_pairs(lax.bitcast_convert_type(y_ref[...], U32))
        x = x + top[:, TOP_K + k:TOP_K + k + 1] * y
    o_ref[...] = _ple_update(x, p_ref[...], nw_ref[...], wg_ref[...], wp_ref[...])


def moe_combine_dense_ple(h, top, yg, p, layer, nw, wg, wp, tm):
    t, d = h.shape
    dw = yg.shape[1]
    pd = p.shape[2]
    nt = t // tm
    resident = lambda shape: pl.BlockSpec(shape, lambda i: (0, 0), pipeline_mode=pl.Buffered(1))
    return pl.pallas_call(
        _combine_dense_ple_kernel,
        grid=(nt,),
        in_specs=[
            pl.BlockSpec((tm, d), lambda i: (i, 0)),
            pl.BlockSpec((tm, LANES), lambda i: (i, 0)),
            pl.BlockSpec((tm, dw), lambda i: (i, 0)),
            pl.BlockSpec((tm, dw), lambda i: (i + nt, 0)),
            pl.BlockSpec((None, tm, pd), lambda i: (layer, i, 0)),
            resident((1, d)), resident((d, d)), resident((pd, d)),
        ],
        out_specs=pl.BlockSpec((tm, d), lambda i: (i, 0)),
        out_shape=jax.ShapeDtypeStruct((t, d), F32),
        compiler_params=_params(("parallel",), 48),
        name="moe_combine_dense_ple",
    )(h, top, yg, yg, p, nw.reshape(1, d), wg, wp)


def moe_ple(h, nw, w_router, wg, wu, wd, p, layer, ple_nw, ple_wg, ple_wp, tm_route, tm_expert, tb, tf):
    t, d = h.shape
    xpk, topt, sel, top = router(h, nw, _pad_cols(w_router, LANES), tm_route)
    rank, cnt = moe_rank(sel, tb)
    counts = cnt[:, 0].astype(I32)
    padded = ((counts + tm_expert - 1) // tm_expert) * tm_expert
    ends = jnp.cumsum(padded)
    n_tiles = (TOP_K * t) // tm_expert + N_EXPERTS
    n_valid = (ends[-1] // tm_expert).astype(I32)
    tile_start = jnp.arange(n_tiles, dtype=I32) * tm_expert
    first_row = jnp.minimum(tile_start, ends[-1] - 1)
    tile_expert = jnp.minimum(jnp.sum(ends[None, :] <= first_row[:, None], axis=1), N_EXPERTS - 1).astype(I32)
    starts = ends - padded
    tile_rows = jnp.clip(counts[tile_expert] - (tile_start - starts[tile_expert]), 0, tm_expert)
    tile_rows = jnp.where(tile_start < ends[-1], tile_rows, 0).astype(I32)
    off = jnp.broadcast_to(starts.astype(F32)[:, None], (N_EXPERTS, LANES))
    pos = moe_pos(topt, rank, off, tb)
    pos_slot_major = jnp.transpose(pos, (1, 0, 2)).reshape(-1)
    xs = sc_row_scatter(xpk, pos_slot_major, n_tiles * tm_expert)
    ys = expert_ffn(xs, tile_expert, n_valid.reshape(1), tile_rows, wg, wu, wd, tm_expert, tf)
    yg = sc_row_gather(ys, pos_slot_major)
    return moe_combine_dense_ple(h, top, yg, p, layer, ple_nw, ple_wg, ple_wp, tm_route)


def _pad_cols(w, n):
    return jnp.pad(w, ((0, 0), (0, n - w.shape[1])))


def _tile3(v, n):
    return jnp.pad(jnp.tile(v, 3), (0, LANES - 3 * n)).reshape(1, LANES)


def kernel(x, p, ssm_norm_w, ssm_w_in, ssm_conv_w, ssm_conv_b, ssm_dt_bias, ssm_a_log, ssm_d, ssm_gn_w, ssm_w_out, kv_norm_w, w_kv, b_f, k_norm_w, att_norm_w, att_w_q, q_norm_w, att_w_o, ffn_norm_w, ffn_w_gate, ffn_w_up, ffn_w_down, moe_norm_w, moe_w_router, moe_w_gate, moe_w_up, moe_w_down, ple_norm_w, ple_w_gate, ple_w_proj):
    b, l, d = x.shape
    t = b * l
    n_a = ssm_norm_w.shape[0]
    depth = ple_norm_w.shape[0]
    d_inner = ssm_gn_w.shape[1]
    n_heads = ssm_dt_bias.shape[1]
    cdim = ssm_conv_w.shape[2]
    att_w = att_w_q.shape[2]
    n_att_heads = att_w // ATT_HEADDIM
    tm = min(1024, t)
    tq = min(256, l)

    h = x.reshape(t, d)
    p3 = p.reshape(depth, t, p.shape[-1])
    r = jnp.arange(LANES)[:, None]
    c = jnp.arange(d_inner)[None, :]
    expand = ((r < 3 * n_heads) & ((r % n_heads) == (c // SSM_HEADDIM))).astype(BF16)

    kv = ext_q = ext_k = None
    for i in range(depth):
        if i < n_a:
            w_in = ssm_w_in[i].astype(BF16)
            w_dt = w_in[:, d_inner + cdim:]
            w_dt3 = _pad_cols(jnp.concatenate([w_dt, w_dt, w_dt], axis=1), LANES)
            z, xbc, dt_raw = norm_matmul(h, [ssm_norm_w[i]], [w_in[:, :d_inner], w_in[:, d_inner:d_inner + cdim], w_dt3],
                                         [0, 0, 0], [BF16, BF16, F32], tm // 2)
            y = ssd(z.reshape(b, l, d_inner), xbc.reshape(b, l, cdim), dt_raw.reshape(b, l, LANES),
                    ssm_conv_w[i], ssm_conv_b[i].reshape(1, cdim), _tile3(ssm_dt_bias[i], n_heads),
                    _tile3(ssm_a_log[i], n_heads), jnp.repeat(ssm_d[i], SSM_HEADDIM).reshape(1, d_inner),
                    ssm_gn_w[i].reshape(1, d_inner), expand, 2)
            h = matmul_residual(h, y.reshape(t, d_inner), ssm_w_out[i].astype(BF16), tm)
        else:
            j = i - n_a
            w_q = att_w_q[j].astype(BF16)
            if i == n_a:
                wkv = w_kv.astype(BF16)
                w_f = wkv[:, 2 * att_w:]
                w_f3 = _pad_cols(jnp.concatenate([w_f, w_f, w_f], axis=1), LANES)
                kv, f, qp = norm_matmul(h, [kv_norm_w, att_norm_w[j]], [wkv[:, :2 * att_w], w_f3, w_q], [0, 0, 1],
                                        [BF16, F32, BF16], tm)
                kv = kv.reshape(b, l, 2 * att_w)
                ext_q, ext_k = forget_ext(f.reshape(b, l, LANES), _tile3(b_f, n_att_heads), n_att_heads)
            else:
                qp, = norm_matmul(h, [att_norm_w[j]], [w_q], [0], [BF16], tm)
            qp = qp.reshape(b, l, att_w)
            qnw2 = jnp.tile(q_norm_w[j], 2).reshape(1, LANES)
            knw2 = jnp.tile(k_norm_w, 2).reshape(1, LANES)
            o = fox_attention(qp, kv, ext_q, ext_k, qnw2, knw2, att_w, tq)
            h = matmul_residual(h, o.reshape(t, att_w), att_w_o[j].astype(BF16), tm)
        ple_wg, ple_wp = ple_w_gate[i].astype(BF16), ple_w_proj[i].astype(BF16)
        if i % 2 == 0:
            dd = i // 2
            h = ffn_ple(h, ffn_norm_w[dd], ffn_w_gate[dd].astype(BF16), ffn_w_up[dd].astype(BF16),
                        ffn_w_down[dd].astype(BF16), p3, i, ple_norm_w[i], ple_wg, ple_wp, tm, 512)
        else:
            m = i // 2
            h = moe_ple(h, moe_norm_w[m], moe_w_router[m], moe_w_gate[m], moe_w_up[m], moe_w_down[m], p3, i,
                        ple_norm_w[i], ple_wg, ple_wp, tm, tm, min(512, t), 512)
    return h.reshape(b, l, d)
```

```python
import functools
import math

import numpy as np
import jax
import jax.numpy as jnp
from jax import lax
from jax.experimental import pallas as pl
from jax.experimental.pallas import tpu as pltpu
from jax.experimental.pallas import tpu_sc as plsc

F32 = jnp.float32
BF16 = jnp.bfloat16
I32 = jnp.int32
U32 = jnp.uint32
EPS = 1e-6
HIGHEST = lax.Precision.HIGHEST
LOG2E = math.log2(math.e)

LANES = 128
SUBLANES = 8
MIB = 1024 * 1024

SSM_HEADDIM = 64
SSM_GROUPS = 4
SSM_STATE = 128
CONV_WIDTH = 4
CHUNK = 128
ATT_HEADDIM = 64
N_EXPERTS = 8
TOP_K = 2


def _params(sem, vmem_mib):
    return pltpu.CompilerParams(dimension_semantics=sem, vmem_limit_bytes=vmem_mib * MIB)


def _rms_scale(x):
    return lax.rsqrt(jnp.mean(x * x, axis=-1, keepdims=True) + EPS)


def _sigmoid(x):
    return 1.0 / (1.0 + jnp.exp2(x * (-LOG2E)))


def _softplus(x):
    return jnp.maximum(x, 0.0) + jnp.log(1.0 + jnp.exp(-jnp.abs(x)))


def _pack_bf16_pairs(x):
    n = x.shape[1] // 2
    bits = lax.bitcast_convert_type(x.astype(BF16).astype(F32), U32)
    return lax.shift_right_logical(bits[:, :n], jnp.uint32(16)) | (bits[:, n:] & jnp.uint32(0xFFFF0000))


def _unpack_bf16_pairs(w):
    lo = lax.bitcast_convert_type(lax.shift_left(w, jnp.uint32(16)), F32)
    hi = lax.bitcast_convert_type(w & jnp.uint32(0xFFFF0000), F32)
    return jnp.concatenate([lo, hi], axis=1)


def _norm_matmul_kernel(h_ref, nw_ref, *refs, norm_of):
    n = len(refs) // 2
    x = h_ref[...]
    xs = x * _rms_scale(x)
    xn = [(xs * nw_ref[g:g + 1, :]).astype(BF16) for g in range(nw_ref.shape[0])]
    for w_ref, o_ref, g in zip(refs[:n], refs[n:], norm_of):
        o_ref[...] = jnp.dot(xn[g], w_ref[...], preferred_element_type=F32).astype(o_ref.dtype)


def norm_matmul(h, nws, ws, norm_of, out_dtypes, tm):
    t, d = h.shape
    resident = lambda shape: pl.BlockSpec(shape, lambda i: (0, 0), pipeline_mode=pl.Buffered(1))
    return pl.pallas_call(
        functools.partial(_norm_matmul_kernel, norm_of=tuple(norm_of)),
        grid=(t // tm,),
        in_specs=[pl.BlockSpec((tm, d), lambda i: (i, 0)), resident((len(nws), d))] + [resident(w.shape) for w in ws],
        out_specs=[pl.BlockSpec((tm, w.shape[1]), lambda i: (i, 0)) for w in ws],
        out_shape=[jax.ShapeDtypeStruct((t, w.shape[1]), dt) for w, dt in zip(ws, out_dtypes)],
        compiler_params=_params(("parallel",), 48),
        name="norm_matmul",
    )(h, jnp.stack(nws), *ws)


def _matmul_residual_kernel(h_ref, y_ref, w_ref, o_ref):
    o_ref[...] = h_ref[...] + jnp.dot(y_ref[...], w_ref[...], preferred_element_type=F32)


def matmul_residual(h, y, w, tm):
    t, n = h.shape
    k = y.shape[1]
    return pl.pallas_call(
        _matmul_residual_kernel,
        grid=(t // tm,),
        in_specs=[
            pl.BlockSpec((tm, n), lambda i: (i, 0)),
            pl.BlockSpec((tm, k), lambda i: (i, 0)),
            pl.BlockSpec((k, n), lambda i: (0, 0)),
        ],
        out_specs=pl.BlockSpec((tm, n), lambda i: (i, 0)),
        out_shape=jax.ShapeDtypeStruct((t, n), F32),
        compiler_params=_params(("parallel",), 48),
        name="matmul_residual",
    )(h, y, w)


def _ple_update(x, p, nw, wg, wp):
    xn = (x * _rms_scale(x) * nw).astype(BF16)
    gate = _sigmoid(jnp.dot(xn, wg, preferred_element_type=F32))
    proj = jnp.dot(p.astype(BF16), wp, preferred_element_type=F32)
    return x + proj * gate


def _ffn_ple_kernel(h_ref, nw_ref, wg_ref, wu_ref, wd_ref, p_ref, pnw_ref, pwg_ref, pwp_ref, o_ref, xn_ref, acc_ref):
    j = pl.program_id(1)

    @pl.when(j == 0)
    def _():
        x = h_ref[...]
        xn_ref[...] = (x * _rms_scale(x) * nw_ref[...]).astype(xn_ref.dtype)
        acc_ref[...] = x

    xn = xn_ref[...]
    g = jnp.dot(xn, wg_ref[...], preferred_element_type=F32)
    u = jnp.dot(xn, wu_ref[...], preferred_element_type=F32)
    a = (g * _sigmoid(g) * u).astype(BF16)
    acc_ref[...] += jnp.dot(a, wd_ref[...], preferred_element_type=F32)

    @pl.when(j == pl.num_programs(1) - 1)
    def _():
        o_ref[...] = _ple_update(acc_ref[...], p_ref[...], pnw_ref[...], pwg_ref[...], pwp_ref[...])


def ffn_ple(h, nw, wg, wu, wd, p, layer, ple_nw, ple_wg, ple_wp, tm, tf):
    t, d = h.shape
    f = wg.shape[1]
    pd = p.shape[2]
    resident = lambda shape: pl.BlockSpec(shape, lambda i, j: (0, 0), pipeline_mode=pl.Buffered(1))
    return pl.pallas_call(
        _ffn_ple_kernel,
        grid=(t // tm, f // tf),
        in_specs=[
            pl.BlockSpec((tm, d), lambda i, j: (i, 0)),
            resident((1, d)),
            pl.BlockSpec((d, tf), lambda i, j: (0, j)),
            pl.BlockSpec((d, tf), lambda i, j: (0, j)),
            pl.BlockSpec((tf, d), lambda i, j: (j, 0)),
            pl.BlockSpec((None, tm, pd), lambda i, j: (layer, i, 0)),
            resident((1, d)), resident((d, d)), resident((pd, d)),
        ],
        out_specs=pl.BlockSpec((tm, d), lambda i, j: (i, 0)),
        out_shape=jax.ShapeDtypeStruct((t, d), F32),
        scratch_shapes=[pltpu.VMEM((tm, d), BF16), pltpu.VMEM((tm, d), F32)],
        compiler_params=_params(("parallel", "arbitrary"), 52),
        name="ffn_ple",
    )(h, nw.reshape(1, d), wg, wu, wd, p, ple_nw.reshape(1, d), ple_wg, ple_wp)


def _split3(v, n):
    v1 = v.astype(BF16)
    r1 = v - v1.astype(F32)
    v2 = r1.astype(BF16)
    v3 = (r1 - v2.astype(F32)).astype(BF16)
    lane = lax.broadcasted_iota(I32, v.shape, 1)
    return jnp.where(lane < n, v1, jnp.where(lane < 2 * n, v2, v3))


def _ssd_kernel(z_ref, xbc_ref, dt_ref, cw_ref, cb_ref, dtb_ref, alog_ref, dskip_ref, gnw_ref, exp_ref,
                o_ref, cbuf, xact, state, acs_t, y_scr):
    q = CHUNK
    tail = cbuf.shape[0] - q

    @pl.when(pl.program_id(1) == 0)
    def _():
        cbuf[0:tail, :] = jnp.zeros((tail, cbuf.shape[1]), BF16)
        state[...] = jnp.zeros(state.shape, F32)

    for sub in range(z_ref.shape[1] // q):
        rows = slice(sub * q, (sub + 1) * q)
        _ssd_chunk(z_ref.at[0, rows], xbc_ref.at[0, rows], dt_ref.at[0, rows], cw_ref, cb_ref, dtb_ref, alog_ref,
                   dskip_ref, gnw_ref, exp_ref, o_ref.at[0, rows], cbuf, xact, state, acs_t, y_scr)


def _ssd_chunk(z_ref, xbc_ref, dt_ref, cw_ref, cb_ref, dtb_ref, alog_ref, dskip_ref, gnw_ref, exp_ref,
               o_ref, cbuf, xact, state, acs_t, y_scr):
    d_inner = z_ref.shape[-1]
    gw = d_inner // SSM_GROUPS
    n_heads = d_inner // SSM_HEADDIM
    q = CHUNK
    tail = cbuf.shape[0] - q

    cbuf[tail:, :] = xbc_ref[...]
    cdim = cbuf.shape[1]
    nshift = CONV_WIDTH - 1
    srow = lax.broadcasted_iota(I32, (nshift * q, tail + q), 0)
    scol = lax.broadcasted_iota(I32, (nshift * q, tail + q), 1)
    shift = jnp.where(scol == srow - (srow // q) * (q - 1) + (tail - nshift), 1.0, 0.0).astype(BF16)
    cw_lanes = 2 * LANES
    for j in range(cdim // cw_lanes):
        sl = slice(j * cw_lanes, (j + 1) * cw_lanes)
        shifted = jnp.dot(shift, cbuf[:, sl], preferred_element_type=F32)
        acc = cb_ref[:, sl] + cbuf[tail:, sl].astype(F32) * cw_ref[nshift:nshift + 1, sl]
        for w in range(nshift):
            acc = acc + shifted[w * q:(w + 1) * q] * cw_ref[w:w + 1, sl]
        xact[:, sl] = acc * _sigmoid(acc)
    cbuf[0:tail, :] = cbuf[q:q + tail, :]

    dt = _softplus(dt_ref[...] + dtb_ref[...])
    a = dt * (-LOG2E * jnp.exp(alog_ref[...]))
    row = lax.broadcasted_iota(I32, (q, q), 0)
    col = lax.broadcasted_iota(I32, (q, q), 1)
    causal = row >= col
    tril = jnp.where(causal, 1.0, 0.0).astype(F32)
    acs = jnp.dot(tril, a, preferred_element_type=F32, precision=HIGHEST)
    acs_t[...] = acs.T
    a_last = acs[q - 1:q, :]
    dt3 = _split3(dt, n_heads)
    dec3 = _split3(jnp.exp2(a_last - acs), n_heads)
    e3 = _split3(jnp.exp2(acs), n_heads)

    for g in range(SSM_GROUPS):
        gl = slice(g * gw, (g + 1) * gw)
        ex = exp_ref[:, gl]
        xs = xact[:, gl]
        b_g = xact[:, d_inner + g * SSM_STATE:d_inner + (g + 1) * SSM_STATE].astype(BF16)
        c_g = xact[:, d_inner + (SSM_GROUPS + g) * SSM_STATE:d_inner + (SSM_GROUPS + g + 1) * SSM_STATE].astype(BF16)
        dt_e = jnp.dot(dt3, ex, preferred_element_type=F32)
        dec_e = jnp.dot(dec3, ex, preferred_element_type=F32)
        e_e = jnp.dot(e3, ex, preferred_element_type=F32)
        xc = xs * dt_e
        xc_b = xc.astype(BF16)
        xdec_b = (xc * dec_e).astype(BF16)
        scores = lax.dot_general(c_g, b_g, (((1,), (1,)), ((), ())), preferred_element_type=F32)
        st = state[g]
        y = jnp.dot(c_g, st.astype(BF16), preferred_element_type=F32) * e_e
        y = y + xs * dskip_ref[:, gl]
        lane = lax.broadcasted_iota(I32, (q, LANES), 1)
        ydiag = []
        for pr in range(gw // LANES):
            ms = []
            for hh in range(2):
                h = g * (gw // SSM_HEADDIM) + 2 * pr + hh
                seg = acs[:, h:h + 1] - acs_t[h:h + 1, :]
                ms.append(scores * jnp.exp2(jnp.where(causal, seg, -jnp.inf)))
            lhs = jnp.concatenate(ms, axis=1).astype(BF16)
            xp = xc_b[:, pr * LANES:(pr + 1) * LANES]
            zero = jnp.zeros_like(xp)
            rhs = jnp.concatenate([jnp.where(lane < SSM_HEADDIM, xp, zero),
                                   jnp.where(lane >= SSM_HEADDIM, xp, zero)], axis=0)
            ydiag.append(jnp.dot(lhs, rhs, preferred_element_type=F32))
        y_scr[:, gl] = y + jnp.concatenate(ydiag, axis=1)
        upd = lax.dot_general(b_g, xdec_b, (((0,), (0,)), ((), ())), preferred_element_type=F32)
        state[g] = st * e_e[q - 1:q, :] + upd

    zz = z_ref[...].astype(F32)
    y = y_scr[...] * (zz * _sigmoid(zz))
    o_ref[...] = (y * _rms_scale(y) * gnw_ref[...]).astype(o_ref.dtype)


def ssd(z, xbc, dt_raw, conv_w, conv_b, dt_bias3, a_log3, dskip_e, gn_w, expand, chunks_per_step):
    b, l, d_inner = z.shape
    cdim = xbc.shape[-1]
    gw = d_inner // SSM_GROUPS
    rows = chunks_per_step * CHUNK
    nc = l // rows
    full = lambda shape: pl.BlockSpec(shape, lambda i, c: (0,) * len(shape))
    return pl.pallas_call(
        _ssd_kernel,
        grid=(b, nc),
        in_specs=[
            pl.BlockSpec((1, rows, d_inner), lambda i, c: (i, c, 0)),
            pl.BlockSpec((1, rows, cdim), lambda i, c: (i, c, 0)),
            pl.BlockSpec((1, rows, LANES), lambda i, c: (i, c, 0)),
            full((CONV_WIDTH, cdim)), full((1, cdim)), full((1, LANES)), full((1, LANES)),
            full((1, d_inner)), full((1, d_inner)), full((LANES, d_inner)),
        ],
        out_specs=pl.BlockSpec((1, rows, d_inner), lambda i, c: (i, c, 0)),
        out_shape=jax.ShapeDtypeStruct((b, l, d_inner), BF16),
        scratch_shapes=[
            pltpu.VMEM((CHUNK + 16, cdim), BF16),
            pltpu.VMEM((CHUNK, cdim), F32),
            pltpu.VMEM((SSM_GROUPS, SSM_STATE, gw), F32),
            pltpu.VMEM((LANES, CHUNK), F32),
            pltpu.VMEM((CHUNK, d_inner), F32),
        ],
        compiler_params=_params(("parallel", "arbitrary"), 40),
        name="ssd",
    )(z, xbc, dt_raw, conv_w, conv_b, dt_bias3, a_log3, dskip_e, gn_w, expand)


def _cum_kernel(f_ref, bf_ref, pq_ref, pk_ref, eq_ref, ek_ref, *, blk, n_heads):
    l = f_ref.shape[1]
    row = lax.broadcasted_iota(I32, (blk, blk), 0)
    col = lax.broadcasted_iota(I32, (blk, blk), 1)
    tril = jnp.where(row >= col, 1.0, 0.0).astype(F32)
    lane = lax.broadcasted_iota(I32, (blk, LANES), 1)
    carry = jnp.zeros((1, LANES), F32)
    for i in range(l // blk):
        rs = slice(i * blk, (i + 1) * blk)
        x = f_ref[0, rs, :] + bf_ref[...]
        lf = -_softplus(-x)
        cs = jnp.dot(tril, lf, preferred_element_type=F32, precision=HIGHEST) + carry
        carry = cs[blk - 1:blk, :]
        lhs = jnp.where(lane == 3 * n_heads, jnp.ones((), BF16), _split3(cs * LOG2E, n_heads))
        eq_ref[0, rs, :] = jnp.dot(lhs, pq_ref[...], preferred_element_type=F32).astype(BF16)
        ek_ref[0, rs, :] = jnp.dot(lhs, pk_ref[...], preferred_element_type=F32).astype(BF16)


def _ext_placement(n_heads):
    w = n_heads * ATT_HEADDIM
    pq = np.zeros((LANES, w), np.float32)
    pk = np.zeros((LANES, w), np.float32)
    one = 3 * n_heads
    for h in range(n_heads):
        base = (h // 2) * LANES + (ATT_HEADDIM if h % 2 == 0 else 0)
        for piece in range(3):
            pq[piece * n_heads + h, base + piece] = 1.0
            pq[one, base + 3 + piece] = 1.0
            pk[one, base + piece] = 1.0
            pk[piece * n_heads + h, base + 3 + piece] = -1.0
    return jnp.asarray(pq, BF16), jnp.asarray(pk, BF16)


def forget_ext(f, b_f3, n_heads):
    b, l, _ = f.shape
    w = n_heads * ATT_HEADDIM
    pq, pk = _ext_placement(n_heads)
    out = jax.ShapeDtypeStruct((b, l, w), BF16)
    return pl.pallas_call(
        functools.partial(_cum_kernel, blk=256, n_heads=n_heads),
        grid=(b,),
        in_specs=[
            pl.BlockSpec((1, l, LANES), lambda i: (i, 0, 0)),
            pl.BlockSpec((1, LANES), lambda i: (0, 0)),
            pl.BlockSpec((LANES, w), lambda i: (0, 0)),
            pl.BlockSpec((LANES, w), lambda i: (0, 0)),
        ],
        out_specs=[pl.BlockSpec((1, l, w), lambda i: (i, 0, 0)), pl.BlockSpec((1, l, w), lambda i: (i, 0, 0))],
        out_shape=[out, out],
        compiler_params=_params(("parallel",), 40),
        name="forget_ext",
    )(f, b_f3, pq, pk)


def _head_rms(x, lane_lo):
    sq = x * x
    s_lo = jnp.sum(jnp.where(lane_lo, sq, 0.0), axis=-1, keepdims=True)
    s_hi = jnp.sum(jnp.where(lane_lo, 0.0, sq), axis=-1, keepdims=True)
    inv = 1.0 / ATT_HEADDIM
    return jnp.where(lane_lo, lax.rsqrt(s_lo * inv + EPS), lax.rsqrt(s_hi * inv + EPS))


def _attn_kernel(q_ref, k_ref, v_ref, eq_ref, ek_ref, qnw_ref, knw_ref, o_ref, kaug, vaugt, s_scr, p_scr, *, tq):
    l = q_ref.shape[1]
    nblk = l // tq
    hd = ATT_HEADDIM
    lane_lo = lax.broadcasted_iota(I32, (tq, LANES), 1) < hd
    sub_lo = lax.broadcasted_iota(I32, (LANES, tq), 0) < hd
    row = lax.broadcasted_iota(I32, (tq, tq), 0)
    col = lax.broadcasted_iota(I32, (tq, tq), 1)
    visible = row <= col
    eye_q = jnp.where(row == col, 1.0, 0.0).astype(BF16)
    eye_l = eye_q[:LANES, :LANES]
    nt = (((1,), (1,)), ((), ()))
    qscale = qnw_ref[...] * (hd ** -0.5 * LOG2E)

    for i in range(nblk):
        rs = slice(i * tq, (i + 1) * tq)
        kk = k_ref[0, rs, :].astype(F32)
        kn = (kk * _head_rms(kk, lane_lo) * knw_ref[...]).astype(BF16)
        ek = ek_ref[0, rs, :]
        kaug[0, rs, :] = jnp.where(lane_lo, kn, ek)
        kaug[1, rs, :] = jnp.where(lane_lo, ek, kn)
        vt = lax.dot_general(eye_l, v_ref[0, rs, :], nt, preferred_element_type=F32).astype(BF16)
        one = jnp.ones_like(vt)
        vaugt[0, :, rs] = jnp.where(sub_lo, vt, one)
        vaugt[1, :, rs] = jnp.where(sub_lo, one, vt)

    def scores(qi):
        qs = slice(qi * tq, (qi + 1) * tq)
        nk = (qi + 1) * tq
        qq = q_ref[0, qs, :].astype(F32)
        qn = (qq * _head_rms(qq, lane_lo) * qscale).astype(BF16)
        eq = eq_ref[0, qs, :]
        for hh in range(2):
            qa = jnp.where(lane_lo, qn, eq) if hh == 0 else jnp.where(lane_lo, eq, qn)
            qt = lax.dot_general(eye_l, qa, nt, preferred_element_type=F32).astype(BF16)
            s = jnp.dot(kaug[hh, :nk, :], qt, preferred_element_type=F32)
            s_scr[qi % 2, hh, :nk, :] = s

    def softmax_pv(qi):
        qs = slice(qi * tq, (qi + 1) * tq)
        nk = (qi + 1) * tq
        par = qi % 2
        accs = []
        for hh in range(2):
            s_scr[par, hh, nk - tq:nk, :] = jnp.where(visible, s_scr[par, hh, nk - tq:nk, :], -jnp.inf)
            m = jnp.max(s_scr[par, hh, :nk, :], axis=0, keepdims=True)
            p_scr[par, hh, :nk, :] = jnp.exp2(s_scr[par, hh, :nk, :] - m).astype(BF16)
            accs.append(jnp.dot(vaugt[hh, :, :nk], p_scr[par, hh, :nk, :], preferred_element_type=F32))
        a0, a1 = accs
        o_t = jnp.concatenate([a0[:hd] / a0[hd:], a1[hd:] / a1[:hd]], axis=0).astype(BF16)
        o_ref[0, qs, :] = lax.dot_general(eye_q, o_t, nt, preferred_element_type=F32).astype(o_ref.dtype)

    scores(0)
    for qi in range(nblk):
        if qi + 1 < nblk:
            scores(qi + 1)
        softmax_pv(qi)


def fox_attention(qp, kv, ext_q, ext_k, qnw2, knw2, att_w, tq):
    b, l, _ = qp.shape
    npair = att_w // LANES
    nblk = l // tq
    blk = lambda off: pl.BlockSpec((1, l, LANES), lambda i, j: (i, 0, j + off))
    return pl.pallas_call(
        functools.partial(_attn_kernel, tq=tq),
        grid=(b, npair),
        in_specs=[
            blk(0), blk(0), blk(npair), blk(0), blk(0),
            pl.BlockSpec((1, LANES), lambda i, j: (0, 0)),
            pl.BlockSpec((1, LANES), lambda i, j: (0, 0)),
        ],
        out_specs=blk(0),
        out_shape=jax.ShapeDtypeStruct((b, l, att_w), BF16),
        scratch_shapes=[
            pltpu.VMEM((2, l, LANES), BF16),
            pltpu.VMEM((2, LANES, l), BF16),
            pltpu.VMEM((2, 2, l, tq), F32),
            pltpu.VMEM((2, 2, l, tq), BF16),
        ],
        compiler_params=_params(("parallel", "parallel"), 40),
        name="fox_attention",
    )(qp, kv, kv, ext_q, ext_k, qnw2, knw2)


def _router_kernel(h_ref, nw_ref, wr_ref, xpk_ref, topt_ref, sel_ref, top_ref):
    x = h_ref[...]
    u = x * _rms_scale(x) * nw_ref[...]
    xpk_ref[...] = lax.bitcast_convert_type(_pack_bf16_pairs(u), I32)
    logits = jnp.dot(u, wr_ref[...], preferred_element_type=F32, precision=HIGHEST)
    lt = logits.T[:N_EXPERTS, :]
    sub = lax.broadcasted_iota(I32, lt.shape, 0)
    neg = jnp.float32(-jnp.inf)
    m1 = jnp.max(lt, axis=0, keepdims=True)
    i1 = jnp.min(jnp.where(lt == m1, sub, N_EXPERTS), axis=0, keepdims=True)
    rest = jnp.where(sub == i1, neg, lt)
    m2 = jnp.max(rest, axis=0, keepdims=True)
    i2 = jnp.min(jnp.where(rest == m2, sub, N_EXPERTS), axis=0, keepdims=True)
    e2 = jnp.exp(m2 - m1)
    g1 = 1.0 / (1.0 + e2)
    g2 = e2 / (1.0 + e2)
    sel_ref[...] = jnp.where((sub == i1) | (sub == i2), 1.0, 0.0).astype(F32)
    topt = jnp.where(sub == 0, i1.astype(F32),
                     jnp.where(sub == 1, i2.astype(F32), jnp.where(sub == 2, g1, jnp.where(sub == 3, g2, 0.0))))
    topt_ref[...] = topt
    pad = jnp.zeros((LANES - N_EXPERTS, topt.shape[1]), F32)
    top_ref[...] = jnp.concatenate([topt, pad], axis=0).T


def router(h, nw, wr_pad, tm):
    t, d = h.shape
    return pl.pallas_call(
        _router_kernel,
        grid=(t // tm,),
        in_specs=[
            pl.BlockSpec((tm, d), lambda i: (i, 0)),
            pl.BlockSpec((1, d), lambda i: (0, 0)),
            pl.BlockSpec((d, LANES), lambda i: (0, 0)),
        ],
        out_specs=[
            pl.BlockSpec((tm, d // 2), lambda i: (i, 0)),
            pl.BlockSpec((N_EXPERTS, tm), lambda i: (0, i)),
            pl.BlockSpec((N_EXPERTS, tm), lambda i: (0, i)),
            pl.BlockSpec((tm, LANES), lambda i: (i, 0)),
        ],
        out_shape=[
            jax.ShapeDtypeStruct((t, d // 2), I32),
            jax.ShapeDtypeStruct((N_EXPERTS, t), F32),
            jax.ShapeDtypeStruct((N_EXPERTS, t), F32),
            jax.ShapeDtypeStruct((t, LANES), F32),
        ],
        compiler_params=_params(("parallel",), 40),
        name="router",
    )(h, nw.reshape(1, d), wr_pad)


def _rank_kernel(sel_ref, rank_ref, cnt_ref, carry):
    @pl.when(pl.program_id(0) == 0)
    def _():
        carry[...] = jnp.zeros(carry.shape, F32)

    s = sel_ref[...]
    tb = s.shape[1]
    row = lax.broadcasted_iota(I32, (tb, tb), 0)
    col = lax.broadcasted_iota(I32, (tb, tb), 1)
    before = jnp.where(row < col, 1.0, 0.0).astype(BF16)
    c = carry[...]
    rank_ref[...] = jnp.dot(s.astype(BF16), before, preferred_element_type=F32) + c[:, :1]
    c = c + jnp.sum(s, axis=1, keepdims=True)
    carry[...] = c
    cnt_ref[...] = c


def moe_rank(sel, tb):
    ne, t = sel.shape
    return pl.pallas_call(
        _rank_kernel,
        grid=(t // tb,),
        in_specs=[pl.BlockSpec((ne, tb), lambda i: (0, i))],
        out_specs=[pl.BlockSpec((ne, tb), lambda i: (0, i)), pl.BlockSpec((ne, LANES), lambda i: (0, 0))],
        out_shape=[jax.ShapeDtypeStruct((ne, t), F32), jax.ShapeDtypeStruct((ne, LANES), F32)],
        scratch_shapes=[pltpu.VMEM((ne, LANES), F32)],
        compiler_params=_params(("arbitrary",), 32),
        name="moe_rank",
    )(sel)


def _pos_kernel(topt_ref, rank_ref, off_ref, pos_ref):
    posv = rank_ref[...] + off_ref[...][:, :1]
    sub = lax.broadcasted_iota(I32, posv.shape, 0).astype(F32)
    for k in range(TOP_K):
        ek = topt_ref[k:k + 1, :]
        pos_ref[0, k:k + 1, :] = jnp.sum(jnp.where(sub == ek, posv, 0.0), axis=0, keepdims=True).astype(I32)


def moe_pos(topt, rank, off, tb):
    ne, t = rank.shape
    return pl.pallas_call(
        _pos_kernel,
        grid=(t // tb,),
        in_specs=[
            pl.BlockSpec((ne, tb), lambda i: (0, i)),
            pl.BlockSpec((ne, tb), lambda i: (0, i)),
            pl.BlockSpec((ne, LANES), lambda i: (0, 0)),
        ],
        out_specs=pl.BlockSpec((1, TOP_K, tb), lambda i: (i, 0, 0)),
        out_shape=jax.ShapeDtypeStruct((t // tb, TOP_K, tb), I32),
        compiler_params=_params(("parallel",), 32),
        name="moe_pos",
    )(topt, rank, off)


def _expert_ffn_kernel(te_ref, nv_ref, nr_ref, x_ref, wg_ref, wu_ref, wd_ref, o_ref, xn_ref, acc_ref):
    del te_ref
    i = pl.program_id(0)
    j = pl.program_id(1)

    @pl.when(i < nv_ref[0])
    def _():
        @pl.when(j == 0)
        def _():
            xw = x_ref[...]
            live = lax.broadcasted_iota(I32, xw.shape, 0) < nr_ref[i]
            xw = lax.bitcast_convert_type(jnp.where(live, xw, 0), U32)
            xn_ref[...] = _unpack_bf16_pairs(xw).astype(BF16)
            acc_ref[...] = jnp.zeros(acc_ref.shape, F32)

        xn = xn_ref[...]
        g = jnp.dot(xn, wg_ref[0].astype(BF16), preferred_element_type=F32)
        u = jnp.dot(xn, wu_ref[0].astype(BF16), preferred_element_type=F32)
        a = (g * _sigmoid(g) * u).astype(BF16)
        acc_ref[...] += jnp.dot(a, wd_ref[0].astype(BF16), preferred_element_type=F32)

        @pl.when(j == pl.num_programs(1) - 1)
        def _():
            o_ref[...] = lax.bitcast_convert_type(_pack_bf16_pairs(acc_ref[...]), I32)

    @pl.when((i >= nv_ref[0]) & (j == pl.num_programs(1) - 1))
    def _():
        o_ref[...] = jnp.zeros(o_ref.shape, o_ref.dtype)


def expert_ffn(xs, tile_expert, n_valid, tile_rows, wg, wu, wd, tm, tf):
    rows, dw = xs.shape
    ne, d, f = wg.shape
    nf = f // tf
    row_blk = lambda i, j, te, nv, nr: (jnp.maximum(jnp.minimum(i, nv[0] - 1), 0), 0)
    out_blk = lambda i, j, te, nv, nr: (i, 0)
    col_j = lambda i, j, nv: jnp.where(i < nv[0], j, nf - 1)
    return pl.pallas_call(
        _expert_ffn_kernel,
        grid_spec=pltpu.PrefetchScalarGridSpec(
            num_scalar_prefetch=3,
            grid=(rows // tm, nf),
            in_specs=[
                pl.BlockSpec((tm, dw), row_blk),
                pl.BlockSpec((1, d, tf), lambda i, j, te, nv, nr: (te[i], 0, col_j(i, j, nv))),
                pl.BlockSpec((1, d, tf), lambda i, j, te, nv, nr: (te[i], 0, col_j(i, j, nv))),
                pl.BlockSpec((1, tf, d), lambda i, j, te, nv, nr: (te[i], col_j(i, j, nv), 0)),
            ],
            out_specs=pl.BlockSpec((tm, dw), out_blk),
            scratch_shapes=[pltpu.VMEM((tm, d), BF16), pltpu.VMEM((tm, d), F32)],
        ),
        out_shape=jax.ShapeDtypeStruct((rows, dw), I32),
        compiler_params=_params(("arbitrary", "arbitrary"), 48),
        name="expert_ffn",
    )(tile_expert, n_valid, tile_rows, xs, wg, wu, wd)


SC_CORES = 2
SC_SUBCORES = 16
SC_INDEX_WINDOW = 128
SC_GATHER_ROWS = 64


def sc_row_gather(data, idx):
    n = idx.shape[0]
    d = data.shape[1]
    n_workers = SC_CORES * SC_SUBCORES
    per_worker = n // n_workers
    win, half = SC_INDEX_WINDOW, SC_GATHER_ROWS
    assert n % n_workers == 0 and per_worker % win == 0 and win == 2 * half
    mesh = plsc.VectorSubcoreMesh(core_axis_name="c", subcore_axis_name="s")

    @functools.partial(
        pl.kernel, out_type=jax.ShapeDtypeStruct((n, d), data.dtype), mesh=mesh,
        scratch_types=[pltpu.VMEM((win,), I32), pltpu.VMEM((half, d), data.dtype), pltpu.VMEM((half, d), data.dtype),
                       pltpu.SemaphoreType.DMA, pltpu.SemaphoreType.DMA])
    def gather_kernel(x_hbm, i_hbm, o_hbm, idx_v, rows_a, rows_b, sem_a, sem_b):
        wid = lax.axis_index("s") * SC_CORES + lax.axis_index("c")
        base = wid * per_worker

        @pl.loop(0, per_worker // win)
        def _(c):
            off = base + c * win
            pltpu.sync_copy(i_hbm.at[pl.ds(off, win)], idx_v)
            cp_a = pltpu.async_copy(x_hbm.at[idx_v.at[pl.ds(0, half)]], rows_a, sem_a)
            cp_b = pltpu.async_copy(x_hbm.at[idx_v.at[pl.ds(half, half)]], rows_b, sem_b)
            cp_a.wait()
            pltpu.sync_copy(rows_a, o_hbm.at[pl.ds(off, half)])
            cp_b.wait()
            pltpu.sync_copy(rows_b, o_hbm.at[pl.ds(off + half, half)])

    return gather_kernel(data, idx)


def sc_row_scatter(rows, idx, n_out):
    r, d = rows.shape
    n = idx.shape[0]
    n_workers = SC_CORES * SC_SUBCORES
    per_worker = n // n_workers
    win = SC_INDEX_WINDOW
    assert n % n_workers == 0 and per_worker % win == 0 and r % per_worker == 0
    mesh = plsc.VectorSubcoreMesh(core_axis_name="c", subcore_axis_name="s")

    @functools.partial(
        pl.kernel, out_type=jax.ShapeDtypeStruct((n_out, d), rows.dtype), mesh=mesh,
        scratch_types=[pltpu.VMEM((win,), I32), pltpu.VMEM((win, d), rows.dtype)])
    def scatter_kernel(x_hbm, i_hbm, o_hbm, idx_v, rows_v):
        wid = lax.axis_index("s") * SC_CORES + lax.axis_index("c")
        base = wid * per_worker
        src_base = lax.rem(base, r)

        @pl.loop(0, per_worker // win)
        def _(c):
            pltpu.sync_copy(i_hbm.at[pl.ds(base + c * win, win)], idx_v)
            pltpu.sync_copy(x_hbm.at[pl.ds(src_base + c * win, win)], rows_v)
            pltpu.sync_copy(rows_v, o_hbm.at[idx_v])

    return scatter_kernel(rows, idx)


def _combine_dense_ple_kernel(h_ref, top_ref, y1_ref, y2_ref, p_ref, nw_ref, wg_ref, wp_ref, o_ref):
    top = top_ref[...]
    x = h_ref[...]
    for k, y_ref in enumerate((y1_ref, y2_ref)):
        y = _unpack_bf16_pairs(lax.bitcast_convert_type(y_ref[...], U32))
        x = x + top[:, TOP_K + k:TOP_K + k + 1] * y
    o_ref[...] = _ple_update(x, p_ref[...], nw_ref[...], wg_ref[...], wp_ref[...])


def moe_combine_dense_ple(h, top, yg, p, layer, nw, wg, wp, tm):
    t, d = h.shape
    dw = yg.shape[1]
    pd = p.shape[2]
    nt = t // tm
    resident = lambda shape: pl.BlockSpec(shape, lambda i: (0, 0), pipeline_mode=pl.Buffered(1))
    return pl.pallas_call(
        _combine_dense_ple_kernel,
        grid=(nt,),
        in_specs=[
            pl.BlockSpec((tm, d), lambda i: (i, 0)),
            pl.BlockSpec((tm, LANES), lambda i: (i, 0)),
            pl.BlockSpec((tm, dw), lambda i: (i, 0)),
            pl.BlockSpec((tm, dw), lambda i: (i + nt, 0)),
            pl.BlockSpec((None, tm, pd), lambda i: (layer, i, 0)),
            resident((1, d)), resident((d, d)), resident((pd, d)),
        ],
        out_specs=pl.BlockSpec((tm, d), lambda i: (i, 0)),
        out_shape=jax.ShapeDtypeStruct((t, d), F32),
        compiler_params=_params(("parallel",), 48),
        name="moe_combine_dense_ple",
    )(h, top, yg, yg, p, nw.reshape(1, d), wg, wp)


def moe_ple(h, nw, w_router, wg, wu, wd, p, layer, ple_nw, ple_wg, ple_wp, tm_route, tm_expert, tb, tf):
    t, d = h.shape
    xpk, topt, sel, top = router(h, nw, _pad_cols(w_router, LANES), tm_route)
    rank, cnt = moe_rank(sel, tb)
    counts = cnt[:, 0].astype(I32)
    padded = ((counts + tm_expert - 1) // tm_expert) * tm_expert
    ends = jnp.cumsum(padded)
    n_tiles = (TOP_K * t) // tm_expert + N_EXPERTS
    n_valid = (ends[-1] // tm_expert).astype(I32)
    tile_start = jnp.arange(n_tiles, dtype=I32) * tm_expert
    first_row = jnp.minimum(tile_start, ends[-1] - 1)
    tile_expert = jnp.minimum(jnp.sum(ends[None, :] <= first_row[:, None], axis=1), N_EXPERTS - 1).astype(I32)
    starts = ends - padded
    tile_rows = jnp.clip(counts[tile_expert] - (tile_start - starts[tile_expert]), 0, tm_expert)
    tile_rows = jnp.where(tile_start < ends[-1], tile_rows, 0).astype(I32)
    off = jnp.broadcast_to(starts.astype(F32)[:, None], (N_EXPERTS, LANES))
    pos = moe_pos(topt, rank, off, tb)
    pos_slot_major = jnp.transpose(pos, (1, 0, 2)).reshape(-1)
    xs = sc_row_scatter(xpk, pos_slot_major, n_tiles * tm_expert)
    ys = expert_ffn(xs, tile_expert, n_valid.reshape(1), tile_rows, wg, wu, wd, tm_expert, tf)
    yg = sc_row_gather(ys, pos_slot_major)
    return moe_combine_dense_ple(h, top, yg, p, layer, ple_nw, ple_wg, ple_wp, tm_route)


def _pad_cols(w, n):
    return jnp.pad(w, ((0, 0), (0, n - w.shape[1])))


def _tile3(v, n):
    return jnp.pad(jnp.tile(v, 3), (0, LANES - 3 * n)).reshape(1, LANES)


def kernel(x, p, ssm_norm_w, ssm_w_in, ssm_conv_w, ssm_conv_b, ssm_dt_bias, ssm_a_log, ssm_d, ssm_gn_w, ssm_w_out, kv_norm_w, w_kv, b_f, k_norm_w, att_norm_w, att_w_q, q_norm_w, att_w_o, ffn_norm_w, ffn_w_gate, ffn_w_up, ffn_w_down, moe_norm_w, moe_w_router, moe_w_gate, moe_w_up, moe_w_down, ple_norm_w, ple_w_gate, ple_w_proj):
    b, l, d = x.shape
    t = b * l
    n_a = ssm_norm_w.shape[0]
    depth = ple_norm_w.shape[0]
    d_inner = ssm_gn_w.shape[1]
    n_heads = ssm_dt_bias.shape[1]
    cdim = ssm_conv_w.shape[2]
    att_w = att_w_q.shape[2]
    n_att_heads = att_w // ATT_HEADDIM
    tm = min(1024, t)
    tq = min(256, l)

    h = x.reshape(t, d)
    p3 = p.reshape(depth, t, p.shape[-1])
    r = jnp.arange(LANES)[:, None]
    c = jnp.arange(d_inner)[None, :]
    expand = ((r < 3 * n_heads) & ((r % n_heads) == (c // SSM_HEADDIM))).astype(BF16)

    kv = ext_q = ext_k = None
    for i in range(depth):
        if i < n_a:
            w_in = ssm_w_in[i].astype(BF16)
            w_dt = w_in[:, d_inner + cdim:]
            w_dt3 = _pad_cols(jnp.concatenate([w_dt, w_dt, w_dt], axis=1), LANES)
            z, xbc, dt_raw = norm_matmul(h, [ssm_norm_w[i]], [w_in[:, :d_inner], w_in[:, d_inner:d_inner + cdim], w_dt3],
                                         [0, 0, 0], [BF16, BF16, F32], tm // 2)
            y = ssd(z.reshape(b, l, d_inner), xbc.reshape(b, l, cdim), dt_raw.reshape(b, l, LANES),
                    ssm_conv_w[i], ssm_conv_b[i].reshape(1, cdim), _tile3(ssm_dt_bias[i], n_heads),
                    _tile3(ssm_a_log[i], n_heads), jnp.repeat(ssm_d[i], SSM_HEADDIM).reshape(1, d_inner),
                    ssm_gn_w[i].reshape(1, d_inner), expand, 2)
            h = matmul_residual(h, y.reshape(t, d_inner), ssm_w_out[i].astype(BF16), tm)
        else:
            j = i - n_a
            w_q = att_w_q[j].astype(BF16)
            if i == n_a:
                wkv = w_kv.astype(BF16)
                w_f = wkv[:, 2 * att_w:]
                w_f3 = _pad_cols(jnp.concatenate([w_f, w_f, w_f], axis=1), LANES)
                kv, f, qp = norm_matmul(h, [kv_norm_w, att_norm_w[j]], [wkv[:, :2 * att_w], w_f3, w_q], [0, 0, 1],
                                        [BF16, F32, BF16], tm)
                kv = kv.reshape(b, l, 2 * att_w)
                ext_q, ext_k = forget_ext(f.reshape(b, l, LANES), _tile3(b_f, n_att_heads), n_att_heads)
            else:
                qp, = norm_matmul(h, [att_norm_w[j]], [w_q], [0], [BF16], tm)
            qp = qp.reshape(b, l, att_w)
            qnw2 = jnp.tile(q_norm_w[j], 2).reshape(1, LANES)
            knw2 = jnp.tile(k_norm_w, 2).reshape(1, LANES)
            o = fox_attention(qp, kv, ext_q, ext_k, qnw2, knw2, att_w, tq)
            h = matmul_residual(h, o.reshape(t, att_w), att_w_o[j].astype(BF16), tm)
        ple_wg, ple_wp = ple_w_gate[i].astype(BF16), ple_w_proj[i].astype(BF16)
        if i % 2 == 0:
            dd = i // 2
            h = ffn_ple(h, ffn_norm_w[dd], ffn_w_gate[dd].astype(BF16), ffn_w_up[dd].astype(BF16),
                        ffn_w_down[dd].astype(BF16), p3, i, ple_norm_w[i], ple_wg, ple_wp, tm // 2, 1792)
        else:
            m = i // 2
            h = moe_ple(h, moe_norm_w[m], moe_w_router[m], moe_w_gate[m], moe_w_up[m], moe_w_down[m], p3, i,
                        ple_norm_w[i], ple_wg, ple_wp, tm, tm, min(512, t), 512)
    return h.reshape(b, l, d)
```

```python
import functools
import math

import numpy as np
import jax
import jax.numpy as jnp
from jax import lax
from jax.experimental import pallas as pl
from jax.experimental.pallas import tpu as pltpu
from jax.experimental.pallas import tpu_sc as plsc

F32 = jnp.float32
BF16 = jnp.bfloat16
I32 = jnp.int32
U32 = jnp.uint32
EPS = 1e-6
HIGHEST = lax.Precision.HIGHEST
LOG2E = math.log2(math.e)

LANES = 128
SUBLANES = 8
MIB = 1024 * 1024

SSM_HEADDIM = 64
SSM_GROUPS = 4
SSM_STATE = 128
CONV_WIDTH = 4
CHUNK = 128
ATT_HEADDIM = 64
N_EXPERTS = 8
TOP_K = 2


def _params(sem, vmem_mib):
    return pltpu.CompilerParams(dimension_semantics=sem, vmem_limit_bytes=vmem_mib * MIB)


def _rms_scale(x):
    return lax.rsqrt(jnp.mean(x * x, axis=-1, keepdims=True) + EPS)


def _sigmoid(x):
    return 1.0 / (1.0 + jnp.exp2(x * (-LOG2E)))


def _softplus(x):
    return jnp.maximum(x, 0.0) + jnp.log(1.0 + jnp.exp(-jnp.abs(x)))


def _pack_bf16_pairs(x):
    n = x.shape[1] // 2
    bits = lax.bitcast_convert_type(x.astype(BF16).astype(F32), U32)
    return lax.shift_right_logical(bits[:, :n], jnp.uint32(16)) | (bits[:, n:] & jnp.uint32(0xFFFF0000))


def _unpack_bf16_pairs(w):
    lo = lax.bitcast_convert_type(lax.shift_left(w, jnp.uint32(16)), F32)
    hi = lax.bitcast_convert_type(w & jnp.uint32(0xFFFF0000), F32)
    return jnp.concatenate([lo, hi], axis=1)


def _norm_matmul_kernel(h_ref, nw_ref, *refs, norm_of):
    n = len(refs) // 2
    x = h_ref[...]
    xs = x * _rms_scale(x)
    xn = [(xs * nw_ref[g:g + 1, :]).astype(BF16) for g in range(nw_ref.shape[0])]
    for w_ref, o_ref, g in zip(refs[:n], refs[n:], norm_of):
        o_ref[...] = jnp.dot(xn[g], w_ref[...], preferred_element_type=F32).astype(o_ref.dtype)


def norm_matmul(h, nws, ws, norm_of, out_dtypes, tm):
    t, d = h.shape
    resident = lambda shape: pl.BlockSpec(shape, lambda i: (0, 0), pipeline_mode=pl.Buffered(1))
    return pl.pallas_call(
        functools.partial(_norm_matmul_kernel, norm_of=tuple(norm_of)),
        grid=(t // tm,),
        in_specs=[pl.BlockSpec((tm, d), lambda i: (i, 0)), resident((len(nws), d))] + [resident(w.shape) for w in ws],
        out_specs=[pl.BlockSpec((tm, w.shape[1]), lambda i: (i, 0)) for w in ws],
        out_shape=[jax.ShapeDtypeStruct((t, w.shape[1]), dt) for w, dt in zip(ws, out_dtypes)],
        compiler_params=_params(("parallel",), 48),
        name="norm_matmul",
    )(h, jnp.stack(nws), *ws)


def _matmul_residual_kernel(h_ref, y_ref, w_ref, o_ref):
    o_ref[...] = h_ref[...] + jnp.dot(y_ref[...], w_ref[...], preferred_element_type=F32)


def matmul_residual(h, y, w, tm):
    t, n = h.shape
    k = y.shape[1]
    return pl.pallas_call(
        _matmul_residual_kernel,
        grid=(t // tm,),
        in_specs=[
            pl.BlockSpec((tm, n), lambda i: (i, 0)),
            pl.BlockSpec((tm, k), lambda i: (i, 0)),
            pl.BlockSpec((k, n), lambda i: (0, 0)),
        ],
        out_specs=pl.BlockSpec((tm, n), lambda i: (i, 0)),
        out_shape=jax.ShapeDtypeStruct((t, n), F32),
        compiler_params=_params(("parallel",), 48),
        name="matmul_residual",
    )(h, y, w)


def _ple_update(x, p, nw, wg, wp):
    xn = (x * _rms_scale(x) * nw).astype(BF16)
    gate = _sigmoid(jnp.dot(xn, wg, preferred_element_type=F32))
    proj = jnp.dot(p.astype(BF16), wp, preferred_element_type=F32)
    return x + proj * gate


def _ffn_ple_kernel(h_ref, nw_ref, wg_ref, wu_ref, wd_ref, p_ref, pnw_ref, pwg_ref, pwp_ref, o_ref, xn_ref, acc_ref):
    j = pl.program_id(1)

    @pl.when(j == 0)
    def _():
        x = h_ref[...]
        xn_ref[...] = (x * _rms_scale(x) * nw_ref[...]).astype(xn_ref.dtype)
        acc_ref[...] = x

    xn = xn_ref[...]
    g = jnp.dot(xn, wg_ref[...], preferred_element_type=F32)
    u = jnp.dot(xn, wu_ref[...], preferred_element_type=F32)
    a = (g * _sigmoid(g) * u).astype(BF16)
    acc_ref[...] += jnp.dot(a, wd_ref[...], preferred_element_type=F32)

    @pl.when(j == pl.num_programs(1) - 1)
    def _():
        o_ref[...] = _ple_update(acc_ref[...], p_ref[...], pnw_ref[...], pwg_ref[...], pwp_ref[...])


def ffn_ple(h, nw, wg, wu, wd, p, layer, ple_nw, ple_wg, ple_wp, tm, tf):
    t, d = h.shape
    f = wg.shape[1]
    pd = p.shape[2]
    resident = lambda shape: pl.BlockSpec(shape, lambda i, j: (0, 0), pipeline_mode=pl.Buffered(1))
    return pl.pallas_call(
        _ffn_ple_kernel,
        grid=(t // tm, f // tf),
        in_specs=[
            pl.BlockSpec((tm, d), lambda i, j: (i, 0)),
            resident((1, d)),
            pl.BlockSpec((d, tf), lambda i, j: (0, j)),
            pl.BlockSpec((d, tf), lambda i, j: (0, j)),
            pl.BlockSpec((tf, d), lambda i, j: (j, 0)),
            pl.BlockSpec((None, tm, pd), lambda i, j: (layer, i, 0)),
            resident((1, d)), resident((d, d)), resident((pd, d)),
        ],
        out_specs=pl.BlockSpec((tm, d), lambda i, j: (i, 0)),
        out_shape=jax.ShapeDtypeStruct((t, d), F32),
        scratch_shapes=[pltpu.VMEM((tm, d), BF16), pltpu.VMEM((tm, d), F32)],
        compiler_params=_params(("parallel", "arbitrary"), 52),
        name="ffn_ple",
    )(h, nw.reshape(1, d), wg, wu, wd, p, ple_nw.reshape(1, d), ple_wg, ple_wp)


def _split3(v, n):
    v1 = v.astype(BF16)
    r1 = v - v1.astype(F32)
    v2 = r1.astype(BF16)
    v3 = (r1 - v2.astype(F32)).astype(BF16)
    lane = lax.broadcasted_iota(I32, v.shape, 1)
    return jnp.where(lane < n, v1, jnp.where(lane < 2 * n, v2, v3))


def _ssd_kernel(z_ref, xbc_ref, dt_ref, cw_ref, cb_ref, dtb_ref, alog_ref, dskip_ref, gnw_ref, exp_ref,
                o_ref, cbuf, xact, state, acs_t, y_scr):
    q = CHUNK
    tail = cbuf.shape[0] - q

    @pl.when(pl.program_id(1) == 0)
    def _():
        cbuf[0:tail, :] = jnp.zeros((tail, cbuf.shape[1]), BF16)
        state[...] = jnp.zeros(state.shape, F32)

    for sub in range(z_ref.shape[1] // q):
        rows = slice(sub * q, (sub + 1) * q)
        _ssd_chunk(z_ref.at[0, rows], xbc_ref.at[0, rows], dt_ref.at[0, rows], cw_ref, cb_ref, dtb_ref, alog_ref,
                   dskip_ref, gnw_ref, exp_ref, o_ref.at[0, rows], cbuf, xact, state, acs_t, y_scr)


def _ssd_chunk(z_ref, xbc_ref, dt_ref, cw_ref, cb_ref, dtb_ref, alog_ref, dskip_ref, gnw_ref, exp_ref,
               o_ref, cbuf, xact, state, acs_t, y_scr):
    d_inner = z_ref.shape[-1]
    gw = d_inner // SSM_GROUPS
    n_heads = d_inner // SSM_HEADDIM
    q = CHUNK
    tail = cbuf.shape[0] - q

    cbuf[tail:, :] = xbc_ref[...]
    cdim = cbuf.shape[1]
    nshift = CONV_WIDTH - 1
    srow = lax.broadcasted_iota(I32, (nshift * q, tail + q), 0)
    scol = lax.broadcasted_iota(I32, (nshift * q, tail + q), 1)
    shift = jnp.where(scol == srow - (srow // q) * (q - 1) + (tail - nshift), 1.0, 0.0).astype(BF16)
    cw_lanes = 2 * LANES
    for j in range(cdim // cw_lanes):
        sl = slice(j * cw_lanes, (j + 1) * cw_lanes)
        shifted = jnp.dot(shift, cbuf[:, sl], preferred_element_type=F32)
        acc = cb_ref[:, sl] + cbuf[tail:, sl].astype(F32) * cw_ref[nshift:nshift + 1, sl]
        for w in range(nshift):
            acc = acc + shifted[w * q:(w + 1) * q] * cw_ref[w:w + 1, sl]
        xact[:, sl] = acc * _sigmoid(acc)
    cbuf[0:tail, :] = cbuf[q:q + tail, :]

    dt = _softplus(dt_ref[...] + dtb_ref[...])
    a = dt * (-LOG2E * jnp.exp(alog_ref[...]))
    row = lax.broadcasted_iota(I32, (q, q), 0)
    col = lax.broadcasted_iota(I32, (q, q), 1)
    causal = row >= col
    tril = jnp.where(causal, 1.0, 0.0).astype(F32)
    acs = jnp.dot(tril, a, preferred_element_type=F32, precision=HIGHEST)
    acs_t[...] = acs.T
    a_last = acs[q - 1:q, :]
    dt3 = _split3(dt, n_heads)
    dec3 = _split3(jnp.exp2(a_last - acs), n_heads)
    e3 = _split3(jnp.exp2(acs), n_heads)

    for g in range(SSM_GROUPS):
        gl = slice(g * gw, (g + 1) * gw)
        ex = exp_ref[:, gl]
        xs = xact[:, gl]
        b_g = xact[:, d_inner + g * SSM_STATE:d_inner + (g + 1) * SSM_STATE].astype(BF16)
        c_g = xact[:, d_inner + (SSM_GROUPS + g) * SSM_STATE:d_inner + (SSM_GROUPS + g + 1) * SSM_STATE].astype(BF16)
        dt_e = jnp.dot(dt3, ex, preferred_element_type=F32)
        dec_e = jnp.dot(dec3, ex, preferred_element_type=F32)
        e_e = jnp.dot(e3, ex, preferred_element_type=F32)
        xc = xs * dt_e
        xc_b = xc.astype(BF16)
        xdec_b = (xc * dec_e).astype(BF16)
        scores = lax.dot_general(c_g, b_g, (((1,), (1,)), ((), ())), preferred_element_type=F32)
        st = state[g]
        y = jnp.dot(c_g, st.astype(BF16), preferred_element_type=F32) * e_e
        y = y + xs * dskip_ref[:, gl]
        lane = lax.broadcasted_iota(I32, (q, LANES), 1)
        ydiag = []
        for pr in range(gw // LANES):
            ms = []
            for hh in range(2):
                h = g * (gw // SSM_HEADDIM) + 2 * pr + hh
                seg = acs[:, h:h + 1] - acs_t[h:h + 1, :]
                ms.append(scores * jnp.exp2(jnp.where(causal, seg, -jnp.inf)))
            lhs = jnp.concatenate(ms, axis=1).astype(BF16)
            xp = xc_b[:, pr * LANES:(pr + 1) * LANES]
            zero = jnp.zeros_like(xp)
            rhs = jnp.concatenate([jnp.where(lane < SSM_HEADDIM, xp, zero),
                                   jnp.where(lane >= SSM_HEADDIM, xp, zero)], axis=0)
            ydiag.append(jnp.dot(lhs, rhs, preferred_element_type=F32))
        y_scr[:, gl] = y + jnp.concatenate(ydiag, axis=1)
        upd = lax.dot_general(b_g, xdec_b, (((0,), (0,)), ((), ())), preferred_element_type=F32)
        state[g] = st * e_e[q - 1:q, :] + upd

    zz = z_ref[...].astype(F32)
    y = y_scr[...] * (zz * _sigmoid(zz))
    o_ref[...] = (y * _rms_scale(y) * gnw_ref[...]).astype(o_ref.dtype)


def ssd(z, xbc, dt_raw, conv_w, conv_b, dt_bias3, a_log3, dskip_e, gn_w, expand, chunks_per_step):
    b, l, d_inner = z.shape
    cdim = xbc.shape[-1]
    gw = d_inner // SSM_GROUPS
    rows = chunks_per_step * CHUNK
    nc = l // rows
    full = lambda shape: pl.BlockSpec(shape, lambda i, c: (0,) * len(shape))
    return pl.pallas_call(
        _ssd_kernel,
        grid=(b, nc),
        in_specs=[
            pl.BlockSpec((1, rows, d_inner), lambda i, c: (i, c, 0)),
            pl.BlockSpec((1, rows, cdim), lambda i, c: (i, c, 0)),
            pl.BlockSpec((1, rows, LANES), lambda i, c: (i, c, 0)),
            full((CONV_WIDTH, cdim)), full((1, cdim)), full((1, LANES)), full((1, LANES)),
            full((1, d_inner)), full((1, d_inner)), full((LANES, d_inner)),
        ],
        out_specs=pl.BlockSpec((1, rows, d_inner), lambda i, c: (i, c, 0)),
        out_shape=jax.ShapeDtypeStruct((b, l, d_inner), BF16),
        scratch_shapes=[
            pltpu.VMEM((CHUNK + 16, cdim), BF16),
            pltpu.VMEM((CHUNK, cdim), F32),
            pltpu.VMEM((SSM_GROUPS, SSM_STATE, gw), F32),
            pltpu.VMEM((LANES, CHUNK), F32),
            pltpu.VMEM((CHUNK, d_inner), F32),
        ],
        compiler_params=_params(("parallel", "arbitrary"), 40),
        name="ssd",
    )(z, xbc, dt_raw, conv_w, conv_b, dt_bias3, a_log3, dskip_e, gn_w, expand)


def _cum_kernel(f_ref, bf_ref, pq_ref, pk_ref, eq_ref, ek_ref, *, blk, n_heads):
    l = f_ref.shape[1]
    row = lax.broadcasted_iota(I32, (blk, blk), 0)
    col = lax.broadcasted_iota(I32, (blk, blk), 1)
    tril = jnp.where(row >= col, 1.0, 0.0).astype(F32)
    lane = lax.broadcasted_iota(I32, (blk, LANES), 1)
    carry = jnp.zeros((1, LANES), F32)
    for i in range(l // blk):
        rs = slice(i * blk, (i + 1) * blk)
        x = f_ref[0, rs, :] + bf_ref[...]
        lf = -_softplus(-x)
        cs = jnp.dot(tril, lf, preferred_element_type=F32, precision=HIGHEST) + carry
        carry = cs[blk - 1:blk, :]
        lhs = jnp.where(lane == 3 * n_heads, jnp.ones((), BF16), _split3(cs * LOG2E, n_heads))
        eq_ref[0, rs, :] = jnp.dot(lhs, pq_ref[...], preferred_element_type=F32).astype(BF16)
        ek_ref[0, rs, :] = jnp.dot(lhs, pk_ref[...], preferred_element_type=F32).astype(BF16)


def _ext_placement(n_heads):
    w = n_heads * ATT_HEADDIM
    pq = np.zeros((LANES, w), np.float32)
    pk = np.zeros((LANES, w), np.float32)
    one = 3 * n_heads
    for h in range(n_heads):
        base = (h // 2) * LANES + (ATT_HEADDIM if h % 2 == 0 else 0)
        for piece in range(3):
            pq[piece * n_heads + h, base + piece] = 1.0
            pq[one, base + 3 + piece] = 1.0
            pk[one, base + piece] = 1.0
            pk[piece * n_heads + h, base + 3 + piece] = -1.0
    return jnp.asarray(pq, BF16), jnp.asarray(pk, BF16)


def forget_ext(f, b_f3, n_heads):
    b, l, _ = f.shape
    w = n_heads * ATT_HEADDIM
    pq, pk = _ext_placement(n_heads)
    out = jax.ShapeDtypeStruct((b, l, w), BF16)
    return pl.pallas_call(
        functools.partial(_cum_kernel, blk=256, n_heads=n_heads),
        grid=(b,),
        in_specs=[
            pl.BlockSpec((1, l, LANES), lambda i: (i, 0, 0)),
            pl.BlockSpec((1, LANES), lambda i: (0, 0)),
            pl.BlockSpec((LANES, w), lambda i: (0, 0)),
            pl.BlockSpec((LANES, w), lambda i: (0, 0)),
        ],
        out_specs=[pl.BlockSpec((1, l, w), lambda i: (i, 0, 0)), pl.BlockSpec((1, l, w), lambda i: (i, 0, 0))],
        out_shape=[out, out],
        compiler_params=_params(("parallel",), 40),
        name="forget_ext",
    )(f, b_f3, pq, pk)


def _head_rms(x, lane_lo):
    sq = x * x
    s_lo = jnp.sum(jnp.where(lane_lo, sq, 0.0), axis=-1, keepdims=True)
    s_hi = jnp.sum(jnp.where(lane_lo, 0.0, sq), axis=-1, keepdims=True)
    inv = 1.0 / ATT_HEADDIM
    return jnp.where(lane_lo, lax.rsqrt(s_lo * inv + EPS), lax.rsqrt(s_hi * inv + EPS))


def _attn_kernel(q_ref, k_ref, v_ref, eq_ref, ek_ref, qnw_ref, knw_ref, o_ref, kaug, vaugt, s_scr, p_scr, *, tq):
    l = q_ref.shape[1]
    nblk = l // tq
    hd = ATT_HEADDIM
    lane_lo = lax.broadcasted_iota(I32, (tq, LANES), 1) < hd
    sub_lo = lax.broadcasted_iota(I32, (LANES, tq), 0) < hd
    row = lax.broadcasted_iota(I32, (tq, tq), 0)
    col = lax.broadcasted_iota(I32, (tq, tq), 1)
    visible = row <= col
    eye_q = jnp.where(row == col, 1.0, 0.0).astype(BF16)
    eye_l = eye_q[:LANES, :LANES]
    nt = (((1,), (1,)), ((), ()))
    qscale = qnw_ref[...] * (hd ** -0.5 * LOG2E)

    for i in range(nblk):
        rs = slice(i * tq, (i + 1) * tq)
        kk = k_ref[0, rs, :].astype(F32)
        kn = (kk * _head_rms(kk, lane_lo) * knw_ref[...]).astype(BF16)
        ek = ek_ref[0, rs, :]
        kaug[0, rs, :] = jnp.where(lane_lo, kn, ek)
        kaug[1, rs, :] = jnp.where(lane_lo, ek, kn)
        vt = lax.dot_general(eye_l, v_ref[0, rs, :], nt, preferred_element_type=F32).astype(BF16)
        one = jnp.ones_like(vt)
        vaugt[0, :, rs] = jnp.where(sub_lo, vt, one)
        vaugt[1, :, rs] = jnp.where(sub_lo, one, vt)

    def scores(qi):
        qs = slice(qi * tq, (qi + 1) * tq)
        nk = (qi + 1) * tq
        qq = q_ref[0, qs, :].astype(F32)
        qn = (qq * _head_rms(qq, lane_lo) * qscale).astype(BF16)
        eq = eq_ref[0, qs, :]
        for hh in range(2):
            qa = jnp.where(lane_lo, qn, eq) if hh == 0 else jnp.where(lane_lo, eq, qn)
            qt = lax.dot_general(eye_l, qa, nt, preferred_element_type=F32).astype(BF16)
            s = jnp.dot(kaug[hh, :nk, :], qt, preferred_element_type=F32)
            s_scr[qi % 2, hh, :nk, :] = s

    def softmax_pv(qi):
        qs = slice(qi * tq, (qi + 1) * tq)
        nk = (qi + 1) * tq
        par = qi % 2
        accs = []
        for hh in range(2):
            s_scr[par, hh, nk - tq:nk, :] = jnp.where(visible, s_scr[par, hh, nk - tq:nk, :], -jnp.inf)
            m = jnp.max(s_scr[par, hh, :nk, :], axis=0, keepdims=True)
            p_scr[par, hh, :nk, :] = jnp.exp2(s_scr[par, hh, :nk, :] - m).astype(BF16)
            accs.append(jnp.dot(vaugt[hh, :, :nk], p_scr[par, hh, :nk, :], preferred_element_type=F32))
        a0, a1 = accs
        o_t = jnp.concatenate([a0[:hd] / a0[hd:], a1[hd:] / a1[:hd]], axis=0).astype(BF16)
        o_ref[0, qs, :] = lax.dot_general(eye_q, o_t, nt, preferred_element_type=F32).astype(o_ref.dtype)

    scores(0)
    for qi in range(nblk):
        if qi + 1 < nblk:
            scores(qi + 1)
        softmax_pv(qi)


def fox_attention(qp, kv, ext_q, ext_k, qnw2, knw2, att_w, tq):
    b, l, _ = qp.shape
    npair = att_w // LANES
    nblk = l // tq
    blk = lambda off: pl.BlockSpec((1, l, LANES), lambda i, j: (i, 0, j + off))
    return pl.pallas_call(
        functools.partial(_attn_kernel, tq=tq),
        grid=(b, npair),
        in_specs=[
            blk(0), blk(0), blk(npair), blk(0), blk(0),
            pl.BlockSpec((1, LANES), lambda i, j: (0, 0)),
            pl.BlockSpec((1, LANES), lambda i, j: (0, 0)),
        ],
        out_specs=blk(0),
        out_shape=jax.ShapeDtypeStruct((b, l, att_w), BF16),
        scratch_shapes=[
            pltpu.VMEM((2, l, LANES), BF16),
            pltpu.VMEM((2, LANES, l), BF16),
            pltpu.VMEM((2, 2, l, tq), F32),
            pltpu.VMEM((2, 2, l, tq), BF16),
        ],
        compiler_params=_params(("parallel", "parallel"), 40),
        name="fox_attention",
    )(qp, kv, kv, ext_q, ext_k, qnw2, knw2)


def _router_kernel(h_ref, nw_ref, wr_ref, xpk_ref, topt_ref, sel_ref, top_ref):
    x = h_ref[...]
    u = x * _rms_scale(x) * nw_ref[...]
    xpk_ref[...] = lax.bitcast_convert_type(_pack_bf16_pairs(u), I32)
    logits = jnp.dot(u, wr_ref[...], preferred_element_type=F32, precision=HIGHEST)
    lt = logits.T[:N_EXPERTS, :]
    sub = lax.broadcasted_iota(I32, lt.shape, 0)
    neg = jnp.float32(-jnp.inf)
    m1 = jnp.max(lt, axis=0, keepdims=True)
    i1 = jnp.min(jnp.where(lt == m1, sub, N_EXPERTS), axis=0, keepdims=True)
    rest = jnp.where(sub == i1, neg, lt)
    m2 = jnp.max(rest, axis=0, keepdims=True)
    i2 = jnp.min(jnp.where(rest == m2, sub, N_EXPERTS), axis=0, keepdims=True)
    e2 = jnp.exp(m2 - m1)
    g1 = 1.0 / (1.0 + e2)
    g2 = e2 / (1.0 + e2)
    sel_ref[...] = jnp.where((sub == i1) | (sub == i2), 1.0, 0.0).astype(F32)
    topt = jnp.where(sub == 0, i1.astype(F32),
                     jnp.where(sub == 1, i2.astype(F32), jnp.where(sub == 2, g1, jnp.where(sub == 3, g2, 0.0))))
    topt_ref[...] = topt
    pad = jnp.zeros((LANES - N_EXPERTS, topt.shape[1]), F32)
    top_ref[...] = jnp.concatenate([topt, pad], axis=0).T


def router(h, nw, wr_pad, tm):
    t, d = h.shape
    return pl.pallas_call(
        _router_kernel,
        grid=(t // tm,),
        in_specs=[
            pl.BlockSpec((tm, d), lambda i: (i, 0)),
            pl.BlockSpec((1, d), lambda i: (0, 0)),
            pl.BlockSpec((d, LANES), lambda i: (0, 0)),
        ],
        out_specs=[
            pl.BlockSpec((tm, d // 2), lambda i: (i, 0)),
            pl.BlockSpec((N_EXPERTS, tm), lambda i: (0, i)),
            pl.BlockSpec((N_EXPERTS, tm), lambda i: (0, i)),
            pl.BlockSpec((tm, LANES), lambda i: (i, 0)),
        ],
        out_shape=[
            jax.ShapeDtypeStruct((t, d // 2), I32),
            jax.ShapeDtypeStruct((N_EXPERTS, t), F32),
            jax.ShapeDtypeStruct((N_EXPERTS, t), F32),
            jax.ShapeDtypeStruct((t, LANES), F32),
        ],
        compiler_params=_params(("parallel",), 40),
        name="router",
    )(h, nw.reshape(1, d), wr_pad)


def _rank_kernel(sel_ref, rank_ref, cnt_ref, carry):
    @pl.when(pl.program_id(0) == 0)
    def _():
        carry[...] = jnp.zeros(carry.shape, F32)

    s = sel_ref[...]
    tb = s.shape[1]
    row = lax.broadcasted_iota(I32, (tb, tb), 0)
    col = lax.broadcasted_iota(I32, (tb, tb), 1)
    before = jnp.where(row < col, 1.0, 0.0).astype(BF16)
    c = carry[...]
    rank_ref[...] = jnp.dot(s.astype(BF16), before, preferred_element_type=F32) + c[:, :1]
    c = c + jnp.sum(s, axis=1, keepdims=True)
    carry[...] = c
    cnt_ref[...] = c


def moe_rank(sel, tb):
    ne, t = sel.shape
    return pl.pallas_call(
        _rank_kernel,
        grid=(t // tb,),
        in_specs=[pl.BlockSpec((ne, tb), lambda i: (0, i))],
        out_specs=[pl.BlockSpec((ne, tb), lambda i: (0, i)), pl.BlockSpec((ne, LANES), lambda i: (0, 0))],
        out_shape=[jax.ShapeDtypeStruct((ne, t), F32), jax.ShapeDtypeStruct((ne, LANES), F32)],
        scratch_shapes=[pltpu.VMEM((ne, LANES), F32)],
        compiler_params=_params(("arbitrary",), 32),
        name="moe_rank",
    )(sel)


def _pos_kernel(topt_ref, rank_ref, off_ref, pos_ref):
    posv = rank_ref[...] + off_ref[...][:, :1]
    sub = lax.broadcasted_iota(I32, posv.shape, 0).astype(F32)
    for k in range(TOP_K):
        ek = topt_ref[k:k + 1, :]
        pos_ref[0, k:k + 1, :] = jnp.sum(jnp.where(sub == ek, posv, 0.0), axis=0, keepdims=True).astype(I32)


def moe_pos(topt, rank, off, tb):
    ne, t = rank.shape
    return pl.pallas_call(
        _pos_kernel,
        grid=(t // tb,),
        in_specs=[
            pl.BlockSpec((ne, tb), lambda i: (0, i)),
            pl.BlockSpec((ne, tb), lambda i: (0, i)),
            pl.BlockSpec((ne, LANES), lambda i: (0, 0)),
        ],
        out_specs=pl.BlockSpec((1, TOP_K, tb), lambda i: (i, 0, 0)),
        out_shape=jax.ShapeDtypeStruct((t // tb, TOP_K, tb), I32),
        compiler_params=_params(("parallel",), 32),
        name="moe_pos",
    )(topt, rank, off)


def _expert_ffn_kernel(te_ref, nv_ref, nr_ref, x_ref, wg_ref, wu_ref, wd_ref, o_ref, xn_ref, acc_ref):
    del te_ref
    i = pl.program_id(0)
    j = pl.program_id(1)

    @pl.when(i < nv_ref[0])
    def _():
        @pl.when(j == 0)
        def _():
            xw = x_ref[...]
            live = lax.broadcasted_iota(I32, xw.shape, 0) < nr_ref[i]
            xw = lax.bitcast_convert_type(jnp.where(live, xw, 0), U32)
            xn_ref[...] = _unpack_bf16_pairs(xw).astype(BF16)
            acc_ref[...] = jnp.zeros(acc_ref.shape, F32)

        xn = xn_ref[...]
        g = jnp.dot(xn, wg_ref[0].astype(BF16), preferred_element_type=F32)
        u = jnp.dot(xn, wu_ref[0].astype(BF16), preferred_element_type=F32)
        a = (g * _sigmoid(g) * u).astype(BF16)
        acc_ref[...] += jnp.dot(a, wd_ref[0].astype(BF16), preferred_element_type=F32)

        @pl.when(j == pl.num_programs(1) - 1)
        def _():
            o_ref[...] = lax.bitcast_convert_type(_pack_bf16_pairs(acc_ref[...]), I32)

    @pl.when((i >= nv_ref[0]) & (j == pl.num_programs(1) - 1))
    def _():
        o_ref[...] = jnp.zeros(o_ref.shape, o_ref.dtype)


def expert_ffn(xs, tile_expert, n_valid, tile_rows, wg, wu, wd, tm, tf):
    rows, dw = xs.shape
    ne, d, f = wg.shape
    nf = f // tf
    row_blk = lambda i, j, te, nv, nr: (jnp.maximum(jnp.minimum(i, nv[0] - 1), 0), 0)
    out_blk = lambda i, j, te, nv, nr: (i, 0)
    col_j = lambda i, j, nv: jnp.where(i < nv[0], j, nf - 1)
    return pl.pallas_call(
        _expert_ffn_kernel,
        grid_spec=pltpu.PrefetchScalarGridSpec(
            num_scalar_prefetch=3,
            grid=(rows // tm, nf),
            in_specs=[
                pl.BlockSpec((tm, dw), row_blk),
                pl.BlockSpec((1, d, tf), lambda i, j, te, nv, nr: (te[i], 0, col_j(i, j, nv))),
                pl.BlockSpec((1, d, tf), lambda i, j, te, nv, nr: (te[i], 0, col_j(i, j, nv))),
                pl.BlockSpec((1, tf, d), lambda i, j, te, nv, nr: (te[i], col_j(i, j, nv), 0)),
            ],
            out_specs=pl.BlockSpec((tm, dw), out_blk),
            scratch_shapes=[pltpu.VMEM((tm, d), BF16), pltpu.VMEM((tm, d), F32)],
        ),
        out_shape=jax.ShapeDtypeStruct((rows, dw), I32),
        compiler_params=_params(("arbitrary", "arbitrary"), 48),
        name="expert_ffn",
    )(tile_expert, n_valid, tile_rows, xs, wg, wu, wd)


SC_CORES = 2
SC_SUBCORES = 16
SC_INDEX_WINDOW = 128
SC_GATHER_ROWS = 64


def sc_row_gather(data, idx):
    n = idx.shape[0]
    d = data.shape[1]
    n_workers = SC_CORES * SC_SUBCORES
    per_worker = n // n_workers
    win, half = SC_INDEX_WINDOW, SC_GATHER_ROWS
    assert n % n_workers == 0 and per_worker % win == 0 and win == 2 * half
    mesh = plsc.VectorSubcoreMesh(core_axis_name="c", subcore_axis_name="s")

    @functools.partial(
        pl.kernel, out_type=jax.ShapeDtypeStruct((n, d), data.dtype), mesh=mesh,
        scratch_types=[pltpu.VMEM((win,), I32), pltpu.VMEM((half, d), data.dtype), pltpu.VMEM((half, d), data.dtype),
                       pltpu.SemaphoreType.DMA, pltpu.SemaphoreType.DMA])
    def gather_kernel(x_hbm, i_hbm, o_hbm, idx_v, rows_a, rows_b, sem_a, sem_b):
        wid = lax.axis_index("s") * SC_CORES + lax.axis_index("c")
        base = wid * per_worker

        @pl.loop(0, per_worker // win)
        def _(c):
            off = base + c * win
            pltpu.sync_copy(i_hbm.at[pl.ds(off, win)], idx_v)
            cp_a = pltpu.async_copy(x_hbm.at[idx_v.at[pl.ds(0, half)]], rows_a, sem_a)
            cp_b = pltpu.async_copy(x_hbm.at[idx_v.at[pl.ds(half, half)]], rows_b, sem_b)
            cp_a.wait()
            pltpu.sync_copy(rows_a, o_hbm.at[pl.ds(off, half)])
            cp_b.wait()
            pltpu.sync_copy(rows_b, o_hbm.at[pl.ds(off + half, half)])

    return gather_kernel(data, idx)


def sc_row_scatter(rows, idx, n_out):
    r, d = rows.shape
    n = idx.shape[0]
    n_workers = SC_CORES * SC_SUBCORES
    per_worker = n // n_workers
    win = SC_INDEX_WINDOW
    assert n % n_workers == 0 and per_worker % win == 0 and r % per_worker == 0
    mesh = plsc.VectorSubcoreMesh(core_axis_name="c", subcore_axis_name="s")

    @functools.partial(
        pl.kernel, out_type=jax.ShapeDtypeStruct((n_out, d), rows.dtype), mesh=mesh,
        scratch_types=[pltpu.VMEM((win,), I32), pltpu.VMEM((win, d), rows.dtype)])
    def scatter_kernel(x_hbm, i_hbm, o_hbm, idx_v, rows_v):
        wid = lax.axis_index("s") * SC_CORES + lax.axis_index("c")
        base = wid * per_worker
        src_base = lax.rem(base, r)

        @pl.loop(0, per_worker // win)
        def _(c):
            pltpu.sync_copy(i_hbm.at[pl.ds(base + c * win, win)], idx_v)
            pltpu.sync_copy(x_hbm.at[pl.ds(src_base + c * win, win)], rows_v)
            pltpu.sync_copy(rows_v, o_hbm.at[idx_v])

    return scatter_kernel(rows, idx)


def _combine_dense_ple_kernel(h_ref, top_ref, y1_ref, y2_ref, p_ref, nw_ref, wg_ref, wp_ref, *rest):
    o_ref = rest[-1]
    top = top_ref[...]
    x = h_ref[...]
    for k, y_ref in enumerate((y1_ref, y2_ref)):
        y = _unpack_bf16_pairs(lax.bitcast_convert_type(y_ref[...], U32))
        x = x + top[:, TOP_K + k:TOP_K + k + 1] * y
    o_ref[...] = _ple_update(x, p_ref[...], nw_ref[...], wg_ref[...], wp_ref[...])


def moe_combine_dense_ple(h, top, yg, p, layer, nw, wg, wp, tm, tile0, prev):
    t, d = h.shape
    dw = yg.shape[1]
    pd = p.shape[2]
    nt = yg.shape[0] // (TOP_K * tm)
    resident = lambda shape: pl.BlockSpec(shape, lambda i: (0, 0), pipeline_mode=pl.Buffered(1))
    in_specs = [
        pl.BlockSpec((tm, d), lambda i: (i + tile0, 0)),
        pl.BlockSpec((tm, LANES), lambda i: (i + tile0, 0)),
        pl.BlockSpec((tm, dw), lambda i: (i, 0)),
        pl.BlockSpec((tm, dw), lambda i: (i + nt, 0)),
        pl.BlockSpec((None, tm, pd), lambda i: (layer, i + tile0, 0)),
        resident((1, d)), resident((d, d)), resident((pd, d)),
    ]
    args = [h, top, yg, yg, p, nw.reshape(1, d), wg, wp]
    aliases = {}
    if prev is not None:
        in_specs.append(pl.BlockSpec(memory_space=pl.ANY))
        args.append(prev)
        aliases = {len(args) - 1: 0}
    return pl.pallas_call(
        _combine_dense_ple_kernel,
        grid=(nt,),
        in_specs=in_specs,
        out_specs=pl.BlockSpec((tm, d), lambda i: (i + tile0, 0)),
        out_shape=jax.ShapeDtypeStruct((t, d), F32),
        input_output_aliases=aliases,
        compiler_params=_params(("parallel",), 48),
        name="moe_combine_dense_ple",
    )(*args)


def moe_ple(h, nw, w_router, wg, wu, wd, p, layer, ple_nw, ple_wg, ple_wp, tm_route, tm_expert, tb, tf):
    t, d = h.shape
    xpk, topt, sel, top = router(h, nw, _pad_cols(w_router, LANES), tm_route)
    rank, cnt = moe_rank(sel, tb)
    counts = cnt[:, 0].astype(I32)
    padded = ((counts + tm_expert - 1) // tm_expert) * tm_expert
    ends = jnp.cumsum(padded)
    n_tiles = (TOP_K * t) // tm_expert + N_EXPERTS
    n_valid = (ends[-1] // tm_expert).astype(I32)
    tile_start = jnp.arange(n_tiles, dtype=I32) * tm_expert
    first_row = jnp.minimum(tile_start, ends[-1] - 1)
    tile_expert = jnp.minimum(jnp.sum(ends[None, :] <= first_row[:, None], axis=1), N_EXPERTS - 1).astype(I32)
    starts = ends - padded
    tile_rows = jnp.clip(counts[tile_expert] - (tile_start - starts[tile_expert]), 0, tm_expert)
    tile_rows = jnp.where(tile_start < ends[-1], tile_rows, 0).astype(I32)
    off = jnp.broadcast_to(starts.astype(F32)[:, None], (N_EXPERTS, LANES))
    pos = moe_pos(topt, rank, off, tb)
    pos_slot_major = jnp.transpose(pos, (1, 0, 2)).reshape(-1)
    xs = sc_row_scatter(xpk, pos_slot_major, n_tiles * tm_expert)
    ys = expert_ffn(xs, tile_expert, n_valid.reshape(1), tile_rows, wg, wu, wd, tm_expert, tf)
    n_parts = 2
    part = t // n_parts
    tm_c = min(tm_route, part)
    pos_by_slot = pos_slot_major.reshape(TOP_K, t)
    out = None
    for k in range(n_parts):
        yg = sc_row_gather(ys, pos_by_slot[:, k * part:(k + 1) * part].reshape(-1))
        out = moe_combine_dense_ple(h, top, yg, p, layer, ple_nw, ple_wg, ple_wp, tm_c, k * (part // tm_c), out)
    return out


def _pad_cols(w, n):
    return jnp.pad(w, ((0, 0), (0, n - w.shape[1])))


def _tile3(v, n):
    return jnp.pad(jnp.tile(v, 3), (0, LANES - 3 * n)).reshape(1, LANES)


def kernel(x, p, ssm_norm_w, ssm_w_in, ssm_conv_w, ssm_conv_b, ssm_dt_bias, ssm_a_log, ssm_d, ssm_gn_w, ssm_w_out, kv_norm_w, w_kv, b_f, k_norm_w, att_norm_w, att_w_q, q_norm_w, att_w_o, ffn_norm_w, ffn_w_gate, ffn_w_up, ffn_w_down, moe_norm_w, moe_w_router, moe_w_gate, moe_w_up, moe_w_down, ple_norm_w, ple_w_gate, ple_w_proj):
    b, l, d = x.shape
    t = b * l
    n_a = ssm_norm_w.shape[0]
    depth = ple_norm_w.shape[0]
    d_inner = ssm_gn_w.shape[1]
    n_heads = ssm_dt_bias.shape[1]
    cdim = ssm_conv_w.shape[2]
    att_w = att_w_q.shape[2]
    n_att_heads = att_w // ATT_HEADDIM
    tm = min(1024, t)
    tq = min(256, l)

    h = x.reshape(t, d)
    p3 = p.reshape(depth, t, p.shape[-1])
    r = jnp.arange(LANES)[:, None]
    c = jnp.arange(d_inner)[None, :]
    expand = ((r < 3 * n_heads) & ((r % n_heads) == (c // SSM_HEADDIM))).astype(BF16)

    kv = ext_q = ext_k = None
    for i in range(depth):
        if i < n_a:
            w_in = ssm_w_in[i].astype(BF16)
            w_dt = w_in[:, d_inner + cdim:]
            w_dt3 = _pad_cols(jnp.concatenate([w_dt, w_dt, w_dt], axis=1), LANES)
            z, xbc, dt_raw = norm_matmul(h, [ssm_norm_w[i]], [w_in[:, :d_inner], w_in[:, d_inner:d_inner + cdim], w_dt3],
                                         [0, 0, 0], [BF16, BF16, F32], tm // 2)
            y = ssd(z.reshape(b, l, d_inner), xbc.reshape(b, l, cdim), dt_raw.reshape(b, l, LANES),
                    ssm_conv_w[i], ssm_conv_b[i].reshape(1, cdim), _tile3(ssm_dt_bias[i], n_heads),
                    _tile3(ssm_a_log[i], n_heads), jnp.repeat(ssm_d[i], SSM_HEADDIM).reshape(1, d_inner),
                    ssm_gn_w[i].reshape(1, d_inner), expand, 2)
            h = matmul_residual(h, y.reshape(t, d_inner), ssm_w_out[i].astype(BF16), tm)
        else:
            j = i - n_a
            w_q = att_w_q[j].astype(BF16)
            if i == n_a:
                wkv = w_kv.astype(BF16)
                w_f = wkv[:, 2 * att_w:]
                w_f3 = _pad_cols(jnp.concatenate([w_f, w_f, w_f], axis=1), LANES)
                kv, f, qp = norm_matmul(h, [kv_norm_w, att_norm_w[j]], [wkv[:, :2 * att_w], w_f3, w_q], [0, 0, 1],
                                        [BF16, F32, BF16], tm)
                kv = kv.reshape(b, l, 2 * att_w)
                ext_q, ext_k = forget_ext(f.reshape(b, l, LANES), _tile3(b_f, n_att_heads), n_att_heads)
            else:
                qp, = norm_matmul(h, [att_norm_w[j]], [w_q], [0], [BF16], tm)
            qp = qp.reshape(b, l, att_w)
            qnw2 = jnp.tile(q_norm_w[j], 2).reshape(1, LANES)
            knw2 = jnp.tile(k_norm_w, 2).reshape(1, LANES)
            o = fox_attention(qp, kv, ext_q, ext_k, qnw2, knw2, att_w, tq)
            h = matmul_residual(h, o.reshape(t, att_w), att_w_o[j].astype(BF16), tm)
        ple_wg, ple_wp = ple_w_gate[i].astype(BF16), ple_w_proj[i].astype(BF16)
        if i % 2 == 0:
            dd = i // 2
            h = ffn_ple(h, ffn_norm_w[dd], ffn_w_gate[dd].astype(BF16), ffn_w_up[dd].astype(BF16),
                        ffn_w_down[dd].astype(BF16), p3, i, ple_norm_w[i], ple_wg, ple_wp, tm // 2, 1792)
        else:
            m = i // 2
            h = moe_ple(h, moe_norm_w[m], moe_w_router[m], moe_w_gate[m], moe_w_up[m], moe_w_down[m], p3, i,
                        ple_norm_w[i], ple_wg, ple_wp, tm, tm, min(512, t), 512)
    return h.reshape(b, l, d)
```

```python
import functools
import math

import numpy as np
import jax
import jax.numpy as jnp
from jax import lax
from jax.experimental import pallas as pl
from jax.experimental.pallas import tpu as pltpu
from jax.experimental.pallas import tpu_sc as plsc

F32 = jnp.float32
BF16 = jnp.bfloat16
I32 = jnp.int32
U32 = jnp.uint32
EPS = 1e-6
HIGHEST = lax.Precision.HIGHEST
LOG2E = math.log2(math.e)

LANES = 128
SUBLANES = 8
MIB = 1024 * 1024

SSM_HEADDIM = 64
SSM_GROUPS = 4
SSM_STATE = 128
CONV_WIDTH = 4
CHUNK = 128
ATT_HEADDIM = 64
N_EXPERTS = 8
TOP_K = 2


def _params(sem, vmem_mib):
    return pltpu.CompilerParams(dimension_semantics=sem, vmem_limit_bytes=vmem_mib * MIB)


def _rms_scale(x):
    return lax.rsqrt(jnp.mean(x * x, axis=-1, keepdims=True) + EPS)


def _sigmoid(x):
    return 1.0 / (1.0 + jnp.exp2(x * (-LOG2E)))


def _softplus(x):
    return jnp.maximum(x, 0.0) + jnp.log(1.0 + jnp.exp(-jnp.abs(x)))


def _pack_bf16_pairs(x):
    n = x.shape[1] // 2
    bits = lax.bitcast_convert_type(x.astype(BF16).astype(F32), U32)
    return lax.shift_right_logical(bits[:, :n], jnp.uint32(16)) | (bits[:, n:] & jnp.uint32(0xFFFF0000))


def _unpack_bf16_pairs(w):
    lo = lax.bitcast_convert_type(lax.shift_left(w, jnp.uint32(16)), F32)
    hi = lax.bitcast_convert_type(w & jnp.uint32(0xFFFF0000), F32)
    return jnp.concatenate([lo, hi], axis=1)


def _norm_matmul_kernel(h_ref, nw_ref, *refs, norm_of):
    n = len(refs) // 2
    x = h_ref[...]
    xs = x * _rms_scale(x)
    xn = [(xs * nw_ref[g:g + 1, :]).astype(BF16) for g in range(nw_ref.shape[0])]
    for w_ref, o_ref, g in zip(refs[:n], refs[n:], norm_of):
        o_ref[...] = jnp.dot(xn[g], w_ref[...], preferred_element_type=F32).astype(o_ref.dtype)


def norm_matmul(h, nws, ws, norm_of, out_dtypes, tm):
    t, d = h.shape
    resident = lambda shape: pl.BlockSpec(shape, lambda i: (0, 0), pipeline_mode=pl.Buffered(1))
    return pl.pallas_call(
        functools.partial(_norm_matmul_kernel, norm_of=tuple(norm_of)),
        grid=(t // tm,),
        in_specs=[pl.BlockSpec((tm, d), lambda i: (i, 0)), resident((len(nws), d))] + [resident(w.shape) for w in ws],
        out_specs=[pl.BlockSpec((tm, w.shape[1]), lambda i: (i, 0)) for w in ws],
        out_shape=[jax.ShapeDtypeStruct((t, w.shape[1]), dt) for w, dt in zip(ws, out_dtypes)],
        compiler_params=_params(("parallel",), 48),
        name="norm_matmul",
    )(h, jnp.stack(nws), *ws)


def _matmul_residual_kernel(h_ref, y_ref, w_ref, o_ref):
    o_ref[...] = h_ref[...] + jnp.dot(y_ref[...], w_ref[...], preferred_element_type=F32)


def matmul_residual(h, y, w, tm):
    t, n = h.shape
    k = y.shape[1]
    return pl.pallas_call(
        _matmul_residual_kernel,
        grid=(t // tm,),
        in_specs=[
            pl.BlockSpec((tm, n), lambda i: (i, 0)),
            pl.BlockSpec((tm, k), lambda i: (i, 0)),
            pl.BlockSpec((k, n), lambda i: (0, 0)),
        ],
        out_specs=pl.BlockSpec((tm, n), lambda i: (i, 0)),
        out_shape=jax.ShapeDtypeStruct((t, n), F32),
        compiler_params=_params(("parallel",), 48),
        name="matmul_residual",
    )(h, y, w)


def _ple_update(x, p, nw, wg, wp):
    xn = (x * _rms_scale(x) * nw).astype(BF16)
    gate = _sigmoid(jnp.dot(xn, wg, preferred_element_type=F32))
    proj = jnp.dot(p.astype(BF16), wp, preferred_element_type=F32)
    return x + proj * gate


def _ffn_ple_kernel(h_ref, nw_ref, wg_ref, wu_ref, wd_ref, p_ref, pnw_ref, pwg_ref, pwp_ref, o_ref, xn_ref, acc_ref):
    j = pl.program_id(1)

    @pl.when(j == 0)
    def _():
        x = h_ref[...]
        xn_ref[...] = (x * _rms_scale(x) * nw_ref[...]).astype(xn_ref.dtype)
        acc_ref[...] = x

    xn = xn_ref[...]
    g = jnp.dot(xn, wg_ref[...], preferred_element_type=F32)
    u = jnp.dot(xn, wu_ref[...], preferred_element_type=F32)
    a = (g * _sigmoid(g) * u).astype(BF16)
    acc_ref[...] += jnp.dot(a, wd_ref[...], preferred_element_type=F32)

    @pl.when(j == pl.num_programs(1) - 1)
    def _():
        o_ref[...] = _ple_update(acc_ref[...], p_ref[...], pnw_ref[...], pwg_ref[...], pwp_ref[...])


def ffn_ple(h, nw, wg, wu, wd, p, layer, ple_nw, ple_wg, ple_wp, tm, tf):
    t, d = h.shape
    f = wg.shape[1]
    pd = p.shape[2]
    resident = lambda shape: pl.BlockSpec(shape, lambda i, j: (0, 0), pipeline_mode=pl.Buffered(1))
    return pl.pallas_call(
        _ffn_ple_kernel,
        grid=(t // tm, f // tf),
        in_specs=[
            pl.BlockSpec((tm, d), lambda i, j: (i, 0)),
            resident((1, d)),
            pl.BlockSpec((d, tf), lambda i, j: (0, j)),
            pl.BlockSpec((d, tf), lambda i, j: (0, j)),
            pl.BlockSpec((tf, d), lambda i, j: (j, 0)),
            pl.BlockSpec((None, tm, pd), lambda i, j: (layer, i, 0)),
            resident((1, d)), resident((d, d)), resident((pd, d)),
        ],
        out_specs=pl.BlockSpec((tm, d), lambda i, j: (i, 0)),
        out_shape=jax.ShapeDtypeStruct((t, d), F32),
        scratch_shapes=[pltpu.VMEM((tm, d), BF16), pltpu.VMEM((tm, d), F32)],
        compiler_params=_params(("parallel", "arbitrary"), 52),
        name="ffn_ple",
    )(h, nw.reshape(1, d), wg, wu, wd, p, ple_nw.reshape(1, d), ple_wg, ple_wp)


def _split3(v, n):
    v1 = v.astype(BF16)
    r1 = v - v1.astype(F32)
    v2 = r1.astype(BF16)
    v3 = (r1 - v2.astype(F32)).astype(BF16)
    lane = lax.broadcasted_iota(I32, v.shape, 1)
    return jnp.where(lane < n, v1, jnp.where(lane < 2 * n, v2, v3))


def _ssd_kernel(z_ref, xbc_ref, dt_ref, cw_ref, cb_ref, dtb_ref, alog_ref, dskip_ref, gnw_ref, exp_ref,
                o_ref, cbuf, xact, state, acs_t, y_scr):
    q = CHUNK
    tail = cbuf.shape[0] - q

    @pl.when(pl.program_id(1) == 0)
    def _():
        cbuf[0:tail, :] = jnp.zeros((tail, cbuf.shape[1]), BF16)
        state[...] = jnp.zeros(state.shape, F32)

    for sub in range(z_ref.shape[1] // q):
        rows = slice(sub * q, (sub + 1) * q)
        _ssd_chunk(z_ref.at[0, rows], xbc_ref.at[0, rows], dt_ref.at[0, rows], cw_ref, cb_ref, dtb_ref, alog_ref,
                   dskip_ref, gnw_ref, exp_ref, o_ref.at[0, rows], cbuf, xact, state, acs_t, y_scr)


def _ssd_chunk(z_ref, xbc_ref, dt_ref, cw_ref, cb_ref, dtb_ref, alog_ref, dskip_ref, gnw_ref, exp_ref,
               o_ref, cbuf, xact, state, acs_t, y_scr):
    d_inner = z_ref.shape[-1]
    gw = d_inner // SSM_GROUPS
    n_heads = d_inner // SSM_HEADDIM
    q = CHUNK
    tail = cbuf.shape[0] - q

    cbuf[tail:, :] = xbc_ref[...]
    cdim = cbuf.shape[1]
    nshift = CONV_WIDTH - 1
    srow = lax.broadcasted_iota(I32, (nshift * q, tail + q), 0)
    scol = lax.broadcasted_iota(I32, (nshift * q, tail + q), 1)
    shift = jnp.where(scol == srow - (srow // q) * (q - 1) + (tail - nshift), 1.0, 0.0).astype(BF16)
    cw_lanes = 2 * LANES
    for j in range(cdim // cw_lanes):
        sl = slice(j * cw_lanes, (j + 1) * cw_lanes)
        shifted = jnp.dot(shift, cbuf[:, sl], preferred_element_type=F32)
        acc = cb_ref[:, sl] + cbuf[tail:, sl].astype(F32) * cw_ref[nshift:nshift + 1, sl]
        for w in range(nshift):
            acc = acc + shifted[w * q:(w + 1) * q] * cw_ref[w:w + 1, sl]
        xact[:, sl] = acc * _sigmoid(acc)
    cbuf[0:tail, :] = cbuf[q:q + tail, :]

    dt = _softplus(dt_ref[...] + dtb_ref[...])
    a = dt * (-LOG2E * jnp.exp(alog_ref[...]))
    row = lax.broadcasted_iota(I32, (q, q), 0)
    col = lax.broadcasted_iota(I32, (q, q), 1)
    causal = row >= col
    tril = jnp.where(causal, 1.0, 0.0).astype(F32)
    acs = jnp.dot(tril, a, preferred_element_type=F32, precision=HIGHEST)
    acs_t[...] = acs.T
    a_last = acs[q - 1:q, :]
    dt3 = _split3(dt, n_heads)
    dec3 = _split3(jnp.exp2(a_last - acs), n_heads)
    e3 = _split3(jnp.exp2(acs), n_heads)

    for g in range(SSM_GROUPS):
        gl = slice(g * gw, (g + 1) * gw)
        ex = exp_ref[:, gl]
        xs = xact[:, gl]
        b_g = xact[:, d_inner + g * SSM_STATE:d_inner + (g + 1) * SSM_STATE].astype(BF16)
        c_g = xact[:, d_inner + (SSM_GROUPS + g) * SSM_STATE:d_inner + (SSM_GROUPS + g + 1) * SSM_STATE].astype(BF16)
        dt_e = jnp.dot(dt3, ex, preferred_element_type=F32)
        dec_e = jnp.dot(dec3, ex, preferred_element_type=F32)
        e_e = jnp.dot(e3, ex, preferred_element_type=F32)
        xc = xs * dt_e
        xc_b = xc.astype(BF16)
        xdec_b = (xc * dec_e).astype(BF16)
        scores = lax.dot_general(c_g, b_g, (((1,), (1,)), ((), ())), preferred_element_type=F32)
        st = state[g]
        y = jnp.dot(c_g, st.astype(BF16), preferred_element_type=F32) * e_e
        y = y + xs * dskip_ref[:, gl]
        lane = lax.broadcasted_iota(I32, (q, LANES), 1)
        ydiag = []
        for pr in range(gw // LANES):
            ms = []
            for hh in range(2):
                h = g * (gw // SSM_HEADDIM) + 2 * pr + hh
                seg = acs[:, h:h + 1] - acs_t[h:h + 1, :]
                ms.append(scores * jnp.exp2(jnp.where(causal, seg, -jnp.inf)))
            lhs = jnp.concatenate(ms, axis=1).astype(BF16)
            xp = xc_b[:, pr * LANES:(pr + 1) * LANES]
            zero = jnp.zeros_like(xp)
            rhs = jnp.concatenate([jnp.where(lane < SSM_HEADDIM, xp, zero),
                                   jnp.where(lane >= SSM_HEADDIM, xp, zero)], axis=0)
            ydiag.append(jnp.dot(lhs, rhs, preferred_element_type=F32))
        y_scr[:, gl] = y + jnp.concatenate(ydiag, axis=1)
        upd = lax.dot_general(b_g, xdec_b, (((0,), (0,)), ((), ())), preferred_element_type=F32)
        state[g] = st * e_e[q - 1:q, :] + upd

    zz = z_ref[...].astype(F32)
    y = y_scr[...] * (zz * _sigmoid(zz))
    o_ref[...] = (y * _rms_scale(y) * gnw_ref[...]).astype(o_ref.dtype)


def ssd(z, xbc, dt_raw, conv_w, conv_b, dt_bias3, a_log3, dskip_e, gn_w, expand, chunks_per_step):
    b, l, d_inner = z.shape
    cdim = xbc.shape[-1]
    gw = d_inner // SSM_GROUPS
    rows = chunks_per_step * CHUNK
    nc = l // rows
    full = lambda shape: pl.BlockSpec(shape, lambda i, c: (0,) * len(shape))
    return pl.pallas_call(
        _ssd_kernel,
        grid=(b, nc),
        in_specs=[
            pl.BlockSpec((1, rows, d_inner), lambda i, c: (i, c, 0)),
            pl.BlockSpec((1, rows, cdim), lambda i, c: (i, c, 0)),
            pl.BlockSpec((1, rows, LANES), lambda i, c: (i, c, 0)),
            full((CONV_WIDTH, cdim)), full((1, cdim)), full((1, LANES)), full((1, LANES)),
            full((1, d_inner)), full((1, d_inner)), full((LANES, d_inner)),
        ],
        out_specs=pl.BlockSpec((1, rows, d_inner), lambda i, c: (i, c, 0)),
        out_shape=jax.ShapeDtypeStruct((b, l, d_inner), BF16),
        scratch_shapes=[
            pltpu.VMEM((CHUNK + 16, cdim), BF16),
            pltpu.VMEM((CHUNK, cdim), F32),
            pltpu.VMEM((SSM_GROUPS, SSM_STATE, gw), F32),
            pltpu.VMEM((LANES, CHUNK), F32),
            pltpu.VMEM((CHUNK, d_inner), F32),
        ],
        compiler_params=_params(("parallel", "arbitrary"), 40),
        name="ssd",
    )(z, xbc, dt_raw, conv_w, conv_b, dt_bias3, a_log3, dskip_e, gn_w, expand)


def _cum_kernel(f_ref, bf_ref, pq_ref, pk_ref, eq_ref, ek_ref, *, blk, n_heads):
    l = f_ref.shape[1]
    row = lax.broadcasted_iota(I32, (blk, blk), 0)
    col = lax.broadcasted_iota(I32, (blk, blk), 1)
    tril = jnp.where(row >= col, 1.0, 0.0).astype(F32)
    lane = lax.broadcasted_iota(I32, (blk, LANES), 1)
    carry = jnp.zeros((1, LANES), F32)
    for i in range(l // blk):
        rs = slice(i * blk, (i + 1) * blk)
        x = f_ref[0, rs, :] + bf_ref[...]
        lf = -_softplus(-x)
        cs = jnp.dot(tril, lf, preferred_element_type=F32, precision=HIGHEST) + carry
        carry = cs[blk - 1:blk, :]
        lhs = jnp.where(lane == 3 * n_heads, jnp.ones((), BF16), _split3(cs * LOG2E, n_heads))
        eq_ref[0, rs, :] = jnp.dot(lhs, pq_ref[...], preferred_element_type=F32).astype(BF16)
        ek_ref[0, rs, :] = jnp.dot(lhs, pk_ref[...], preferred_element_type=F32).astype(BF16)


def _ext_placement(n_heads):
    w = n_heads * ATT_HEADDIM
    pq = np.zeros((LANES, w), np.float32)
    pk = np.zeros((LANES, w), np.float32)
    one = 3 * n_heads
    for h in range(n_heads):
        base = (h // 2) * LANES + (ATT_HEADDIM if h % 2 == 0 else 0)
        for piece in range(3):
            pq[piece * n_heads + h, base + piece] = 1.0
            pq[one, base + 3 + piece] = 1.0
            pk[one, base + piece] = 1.0
            pk[piece * n_heads + h, base + 3 + piece] = -1.0
    return jnp.asarray(pq, BF16), jnp.asarray(pk, BF16)


def forget_ext(f, b_f3, n_heads):
    b, l, _ = f.shape
    w = n_heads * ATT_HEADDIM
    pq, pk = _ext_placement(n_heads)
    out = jax.ShapeDtypeStruct((b, l, w), BF16)
    return pl.pallas_call(
        functools.partial(_cum_kernel, blk=256, n_heads=n_heads),
        grid=(b,),
        in_specs=[
            pl.BlockSpec((1, l, LANES), lambda i: (i, 0, 0)),
            pl.BlockSpec((1, LANES), lambda i: (0, 0)),
            pl.BlockSpec((LANES, w), lambda i: (0, 0)),
            pl.BlockSpec((LANES, w), lambda i: (0, 0)),
        ],
        out_specs=[pl.BlockSpec((1, l, w), lambda i: (i, 0, 0)), pl.BlockSpec((1, l, w), lambda i: (i, 0, 0))],
        out_shape=[out, out],
        compiler_params=_params(("parallel",), 40),
        name="forget_ext",
    )(f, b_f3, pq, pk)


def _head_rms(x, lane_lo):
    sq = x * x
    s_lo = jnp.sum(jnp.where(lane_lo, sq, 0.0), axis=-1, keepdims=True)
    s_hi = jnp.sum(jnp.where(lane_lo, 0.0, sq), axis=-1, keepdims=True)
    inv = 1.0 / ATT_HEADDIM
    return jnp.where(lane_lo, lax.rsqrt(s_lo * inv + EPS), lax.rsqrt(s_hi * inv + EPS))


def _attn_kernel(q_ref, k_ref, v_ref, eq_ref, ek_ref, qnw_ref, knw_ref, o_ref, kaug, vaugt, s_scr, p_scr, *, tq):
    l = q_ref.shape[1]
    nblk = l // tq
    hd = ATT_HEADDIM
    lane_lo = lax.broadcasted_iota(I32, (tq, LANES), 1) < hd
    sub_lo = lax.broadcasted_iota(I32, (LANES, tq), 0) < hd
    row = lax.broadcasted_iota(I32, (tq, tq), 0)
    col = lax.broadcasted_iota(I32, (tq, tq), 1)
    visible = row <= col
    eye_q = jnp.where(row == col, 1.0, 0.0).astype(BF16)
    eye_l = eye_q[:LANES, :LANES]
    nt = (((1,), (1,)), ((), ()))
    qscale = qnw_ref[...] * (hd ** -0.5 * LOG2E)

    for i in range(nblk):
        rs = slice(i * tq, (i + 1) * tq)
        kk = k_ref[0, rs, :].astype(F32)
        kn = (kk * _head_rms(kk, lane_lo) * knw_ref[...]).astype(BF16)
        ek = ek_ref[0, rs, :]
        kaug[0, rs, :] = jnp.where(lane_lo, kn, ek)
        kaug[1, rs, :] = jnp.where(lane_lo, ek, kn)
        vt = lax.dot_general(eye_l, v_ref[0, rs, :], nt, preferred_element_type=F32).astype(BF16)
        one = jnp.ones_like(vt)
        vaugt[0, :, rs] = jnp.where(sub_lo, vt, one)
        vaugt[1, :, rs] = jnp.where(sub_lo, one, vt)

    def scores(qi):
        qs = slice(qi * tq, (qi + 1) * tq)
        nk = (qi + 1) * tq
        qq = q_ref[0, qs, :].astype(F32)
        qn = (qq * _head_rms(qq, lane_lo) * qscale).astype(BF16)
        eq = eq_ref[0, qs, :]
        for hh in range(2):
            qa = jnp.where(lane_lo, qn, eq) if hh == 0 else jnp.where(lane_lo, eq, qn)
            qt = lax.dot_general(eye_l, qa, nt, preferred_element_type=F32).astype(BF16)
            s = jnp.dot(kaug[hh, :nk, :], qt, preferred_element_type=F32)
            s_scr[qi % 2, hh, :nk, :] = s

    def softmax_pv(qi):
        qs = slice(qi * tq, (qi + 1) * tq)
        nk = (qi + 1) * tq
        par = qi % 2
        accs = []
        for hh in range(2):
            s_scr[par, hh, nk - tq:nk, :] = jnp.where(visible, s_scr[par, hh, nk - tq:nk, :], -jnp.inf)
            m = jnp.max(s_scr[par, hh, :nk, :], axis=0, keepdims=True)
            p_scr[par, hh, :nk, :] = jnp.exp2(s_scr[par, hh, :nk, :] - m).astype(BF16)
            accs.append(jnp.dot(vaugt[hh, :, :nk], p_scr[par, hh, :nk, :], preferred_element_type=F32))
        a0, a1 = accs
        o_t = jnp.concatenate([a0[:hd] / a0[hd:], a1[hd:] / a1[:hd]], axis=0).astype(BF16)
        o_ref[0, qs, :] = lax.dot_general(eye_q, o_t, nt, preferred_element_type=F32).astype(o_ref.dtype)

    scores(0)
    for qi in range(nblk):
        if qi + 1 < nblk:
            scores(qi + 1)
        softmax_pv(qi)


def fox_attention(qp, kv, ext_q, ext_k, qnw2, knw2, att_w, tq):
    b, l, _ = qp.shape
    npair = att_w // LANES
    nblk = l // tq
    blk = lambda off: pl.BlockSpec((1, l, LANES), lambda i, j: (i, 0, j + off))
    return pl.pallas_call(
        functools.partial(_attn_kernel, tq=tq),
        grid=(b, npair),
        in_specs=[
            blk(0), blk(0), blk(npair), blk(0), blk(0),
            pl.BlockSpec((1, LANES), lambda i, j: (0, 0)),
            pl.BlockSpec((1, LANES), lambda i, j: (0, 0)),
        ],
        out_specs=blk(0),
        out_shape=jax.ShapeDtypeStruct((b, l, att_w), BF16),
        scratch_shapes=[
            pltpu.VMEM((2, l, LANES), BF16),
            pltpu.VMEM((2, LANES, l), BF16),
            pltpu.VMEM((2, 2, l, tq), F32),
            pltpu.VMEM((2, 2, l, tq), BF16),
        ],
        compiler_params=_params(("parallel", "parallel"), 40),
        name="fox_attention",
    )(qp, kv, kv, ext_q, ext_k, qnw2, knw2)


def _router_kernel(h_ref, nw_ref, wr_ref, xpk_ref, topt_ref, sel_ref, top_ref):
    x = h_ref[...]
    u = x * _rms_scale(x) * nw_ref[...]
    xpk_ref[...] = lax.bitcast_convert_type(_pack_bf16_pairs(u), I32)
    logits = jnp.dot(u, wr_ref[...], preferred_element_type=F32, precision=HIGHEST)
    lt = logits.T[:N_EXPERTS, :]
    sub = lax.broadcasted_iota(I32, lt.shape, 0)
    neg = jnp.float32(-jnp.inf)
    m1 = jnp.max(lt, axis=0, keepdims=True)
    i1 = jnp.min(jnp.where(lt == m1, sub, N_EXPERTS), axis=0, keepdims=True)
    rest = jnp.where(sub == i1, neg, lt)
    m2 = jnp.max(rest, axis=0, keepdims=True)
    i2 = jnp.min(jnp.where(rest == m2, sub, N_EXPERTS), axis=0, keepdims=True)
    e2 = jnp.exp(m2 - m1)
    g1 = 1.0 / (1.0 + e2)
    g2 = e2 / (1.0 + e2)
    sel_ref[...] = jnp.where((sub == i1) | (sub == i2), 1.0, 0.0).astype(F32)
    topt = jnp.where(sub == 0, i1.astype(F32),
                     jnp.where(sub == 1, i2.astype(F32), jnp.where(sub == 2, g1, jnp.where(sub == 3, g2, 0.0))))
    topt_ref[...] = topt
    pad = jnp.zeros((LANES - N_EXPERTS, topt.shape[1]), F32)
    top_ref[...] = jnp.concatenate([topt, pad], axis=0).T


def router(h, nw, wr_pad, tm):
    t, d = h.shape
    return pl.pallas_call(
        _router_kernel,
        grid=(t // tm,),
        in_specs=[
            pl.BlockSpec((tm, d), lambda i: (i, 0)),
            pl.BlockSpec((1, d), lambda i: (0, 0)),
            pl.BlockSpec((d, LANES), lambda i: (0, 0)),
        ],
        out_specs=[
            pl.BlockSpec((tm, d // 2), lambda i: (i, 0)),
            pl.BlockSpec((N_EXPERTS, tm), lambda i: (0, i)),
            pl.BlockSpec((N_EXPERTS, tm), lambda i: (0, i)),
            pl.BlockSpec((tm, LANES), lambda i: (i, 0)),
        ],
        out_shape=[
            jax.ShapeDtypeStruct((t, d // 2), I32),
            jax.ShapeDtypeStruct((N_EXPERTS, t), F32),
            jax.ShapeDtypeStruct((N_EXPERTS, t), F32),
            jax.ShapeDtypeStruct((t, LANES), F32),
        ],
        compiler_params=_params(("parallel",), 40),
        name="router",
    )(h, nw.reshape(1, d), wr_pad)


def _rank_kernel(sel_ref, rank_ref, cnt_ref, carry):
    @pl.when(pl.program_id(0) == 0)
    def _():
        carry[...] = jnp.zeros(carry.shape, F32)

    s = sel_ref[...]
    tb = s.shape[1]
    row = lax.broadcasted_iota(I32, (tb, tb), 0)
    col = lax.broadcasted_iota(I32, (tb, tb), 1)
    before = jnp.where(row < col, 1.0, 0.0).astype(BF16)
    c = carry[...]
    rank_ref[...] = jnp.dot(s.astype(BF16), before, preferred_element_type=F32) + c[:, :1]
    c = c + jnp.sum(s, axis=1, keepdims=True)
    carry[...] = c
    cnt_ref[...] = c


def moe_rank(sel, tb):
    ne, t = sel.shape
    return pl.pallas_call(
        _rank_kernel,
        grid=(t // tb,),
        in_specs=[pl.BlockSpec((ne, tb), lambda i: (0, i))],
        out_specs=[pl.BlockSpec((ne, tb), lambda i: (0, i)), pl.BlockSpec((ne, LANES), lambda i: (0, 0))],
        out_shape=[jax.ShapeDtypeStruct((ne, t), F32), jax.ShapeDtypeStruct((ne, LANES), F32)],
        scratch_shapes=[pltpu.VMEM((ne, LANES), F32)],
        compiler_params=_params(("arbitrary",), 32),
        name="moe_rank",
    )(sel)


def _pos_kernel(topt_ref, rank_ref, off_ref, pos_ref):
    posv = rank_ref[...] + off_ref[...][:, :1]
    sub = lax.broadcasted_iota(I32, posv.shape, 0).astype(F32)
    for k in range(TOP_K):
        ek = topt_ref[k:k + 1, :]
        pos_ref[0, k:k + 1, :] = jnp.sum(jnp.where(sub == ek, posv, 0.0), axis=0, keepdims=True).astype(I32)


def moe_pos(topt, rank, off, tb):
    ne, t = rank.shape
    return pl.pallas_call(
        _pos_kernel,
        grid=(t // tb,),
        in_specs=[
            pl.BlockSpec((ne, tb), lambda i: (0, i)),
            pl.BlockSpec((ne, tb), lambda i: (0, i)),
            pl.BlockSpec((ne, LANES), lambda i: (0, 0)),
        ],
        out_specs=pl.BlockSpec((1, TOP_K, tb), lambda i: (i, 0, 0)),
        out_shape=jax.ShapeDtypeStruct((t // tb, TOP_K, tb), I32),
        compiler_params=_params(("parallel",), 32),
        name="moe_pos",
    )(topt, rank, off)


def _expert_ffn_kernel(te_ref, nv_ref, nr_ref, x_ref, wg_ref, wu_ref, wd_ref, o_ref, xn_ref, acc_ref):
    del te_ref
    i = pl.program_id(0)
    j = pl.program_id(1)

    @pl.when(i < nv_ref[0])
    def _():
        @pl.when(j == 0)
        def _():
            xw = x_ref[...]
            live = lax.broadcasted_iota(I32, xw.shape, 0) < nr_ref[i]
            xw = lax.bitcast_convert_type(jnp.where(live, xw, 0), U32)
            xn_ref[...] = _unpack_bf16_pairs(xw).astype(BF16)
            acc_ref[...] = jnp.zeros(acc_ref.shape, F32)

        xn = xn_ref[...]
        g = jnp.dot(xn, wg_ref[0].astype(BF16), preferred_element_type=F32)
        u = jnp.dot(xn, wu_ref[0].astype(BF16), preferred_element_type=F32)
        a = (g * _sigmoid(g) * u).astype(BF16)
        acc_ref[...] += jnp.dot(a, wd_ref[0].astype(BF16), preferred_element_type=F32)

        @pl.when(j == pl.num_programs(1) - 1)
        def _():
            o_ref[...] = lax.bitcast_convert_type(_pack_bf16_pairs(acc_ref[...]), I32)

    @pl.when((i >= nv_ref[0]) & (j == pl.num_programs(1) - 1))
    def _():
        o_ref[...] = jnp.zeros(o_ref.shape, o_ref.dtype)


def expert_ffn(xs, tile_expert, n_valid, tile_rows, wg, wu, wd, tm, tf):
    rows, dw = xs.shape
    ne, d, f = wg.shape
    nf = f // tf
    row_blk = lambda i, j, te, nv, nr: (jnp.maximum(jnp.minimum(i, nv[0] - 1), 0), 0)
    out_blk = lambda i, j, te, nv, nr: (i, 0)
    col_j = lambda i, j, nv: jnp.where(i < nv[0], j, nf - 1)
    return pl.pallas_call(
        _expert_ffn_kernel,
        grid_spec=pltpu.PrefetchScalarGridSpec(
            num_scalar_prefetch=3,
            grid=(rows // tm, nf),
            in_specs=[
                pl.BlockSpec((tm, dw), row_blk),
                pl.BlockSpec((1, d, tf), lambda i, j, te, nv, nr: (te[i], 0, col_j(i, j, nv))),
                pl.BlockSpec((1, d, tf), lambda i, j, te, nv, nr: (te[i], 0, col_j(i, j, nv))),
                pl.BlockSpec((1, tf, d), lambda i, j, te, nv, nr: (te[i], col_j(i, j, nv), 0)),
            ],
            out_specs=pl.BlockSpec((tm, dw), out_blk),
            scratch_shapes=[pltpu.VMEM((tm, d), BF16), pltpu.VMEM((tm, d), F32)],
        ),
        out_shape=jax.ShapeDtypeStruct((rows, dw), I32),
        compiler_params=_params(("arbitrary", "arbitrary"), 48),
        name="expert_ffn",
    )(tile_expert, n_valid, tile_rows, xs, wg, wu, wd)


SC_CORES = 2
SC_SUBCORES = 16
SC_INDEX_WINDOW = 128
SC_GATHER_ROWS = 64


def sc_row_gather(data, idx):
    n = idx.shape[0]
    d = data.shape[1]
    n_workers = SC_CORES * SC_SUBCORES
    per_worker = n // n_workers
    win, half = SC_INDEX_WINDOW, SC_GATHER_ROWS
    assert n % n_workers == 0 and per_worker % win == 0 and win == 2 * half
    mesh = plsc.VectorSubcoreMesh(core_axis_name="c", subcore_axis_name="s")

    @functools.partial(
        pl.kernel, out_type=jax.ShapeDtypeStruct((n, d), data.dtype), mesh=mesh,
        scratch_types=[pltpu.VMEM((win,), I32), pltpu.VMEM((half, d), data.dtype), pltpu.VMEM((half, d), data.dtype),
                       pltpu.SemaphoreType.DMA, pltpu.SemaphoreType.DMA])
    def gather_kernel(x_hbm, i_hbm, o_hbm, idx_v, rows_a, rows_b, sem_a, sem_b):
        wid = lax.axis_index("s") * SC_CORES + lax.axis_index("c")
        base = wid * per_worker

        @pl.loop(0, per_worker // win)
        def _(c):
            off = base + c * win
            pltpu.sync_copy(i_hbm.at[pl.ds(off, win)], idx_v)
            cp_a = pltpu.async_copy(x_hbm.at[idx_v.at[pl.ds(0, half)]], rows_a, sem_a)
            cp_b = pltpu.async_copy(x_hbm.at[idx_v.at[pl.ds(half, half)]], rows_b, sem_b)
            cp_a.wait()
            pltpu.sync_copy(rows_a, o_hbm.at[pl.ds(off, half)])
            cp_b.wait()
            pltpu.sync_copy(rows_b, o_hbm.at[pl.ds(off + half, half)])

    return gather_kernel(data, idx)


def sc_row_scatter(rows, idx, n_out):
    r, d = rows.shape
    n = idx.shape[0]
    n_workers = SC_CORES * SC_SUBCORES
    per_worker = n // n_workers
    win = SC_INDEX_WINDOW
    assert n % n_workers == 0 and per_worker % win == 0 and r % per_worker == 0
    mesh = plsc.VectorSubcoreMesh(core_axis_name="c", subcore_axis_name="s")

    @functools.partial(
        pl.kernel, out_type=jax.ShapeDtypeStruct((n_out, d), rows.dtype), mesh=mesh,
        scratch_types=[pltpu.VMEM((win,), I32), pltpu.VMEM((win, d), rows.dtype)])
    def scatter_kernel(x_hbm, i_hbm, o_hbm, idx_v, rows_v):
        wid = lax.axis_index("s") * SC_CORES + lax.axis_index("c")
        base = wid * per_worker
        src_base = lax.rem(base, r)

        @pl.loop(0, per_worker // win)
        def _(c):
            pltpu.sync_copy(i_hbm.at[pl.ds(base + c * win, win)], idx_v)
            pltpu.sync_copy(x_hbm.at[pl.ds(src_base + c * win, win)], rows_v)
            pltpu.sync_copy(rows_v, o_hbm.at[idx_v])

    return scatter_kernel(rows, idx)


def _combine_dense_ple_kernel(h_ref, top_ref, y1_ref, y2_ref, p_ref, nw_ref, wg_ref, wp_ref, *rest):
    o_ref = rest[-1]
    top = top_ref[...]
    x = h_ref[...]
    for k, y_ref in enumerate((y1_ref, y2_ref)):
        y = _unpack_bf16_pairs(lax.bitcast_convert_type(y_ref[...], U32))
        x = x + top[:, TOP_K + k:TOP_K + k + 1] * y
    o_ref[...] = _ple_update(x, p_ref[...], nw_ref[...], wg_ref[...], wp_ref[...])


def moe_combine_dense_ple(h, top, yg, p, layer, nw, wg, wp, tm, tile0, prev):
    t, d = h.shape
    dw = yg.shape[1]
    pd = p.shape[2]
    nt = yg.shape[0] // (TOP_K * tm)
    resident = lambda shape: pl.BlockSpec(shape, lambda i: (0, 0), pipeline_mode=pl.Buffered(1))
    in_specs = [
        pl.BlockSpec((tm, d), lambda i: (i + tile0, 0)),
        pl.BlockSpec((tm, LANES), lambda i: (i + tile0, 0)),
        pl.BlockSpec((tm, dw), lambda i: (i, 0)),
        pl.BlockSpec((tm, dw), lambda i: (i + nt, 0)),
        pl.BlockSpec((None, tm, pd), lambda i: (layer, i + tile0, 0)),
        resident((1, d)), resident((d, d)), resident((pd, d)),
    ]
    args = [h, top, yg, yg, p, nw.reshape(1, d), wg, wp]
    aliases = {}
    if prev is not None:
        in_specs.append(pl.BlockSpec(memory_space=pl.ANY))
        args.append(prev)
        aliases = {len(args) - 1: 0}
    return pl.pallas_call(
        _combine_dense_ple_kernel,
        grid=(nt,),
        in_specs=in_specs,
        out_specs=pl.BlockSpec((tm, d), lambda i: (i + tile0, 0)),
        out_shape=jax.ShapeDtypeStruct((t, d), F32),
        input_output_aliases=aliases,
        compiler_params=_params(("parallel",), 48),
        name="moe_combine_dense_ple",
    )(*args)


def moe_ple(h, nw, w_router, wg, wu, wd, p, layer, ple_nw, ple_wg, ple_wp, tm_route, tm_expert, tb, tf):
    t, d = h.shape
    xpk, topt, sel, top = router(h, nw, _pad_cols(w_router, LANES), tm_route)
    rank, cnt = moe_rank(sel, tb)
    counts = cnt[:, 0].astype(I32)
    padded = ((counts + tm_expert - 1) // tm_expert) * tm_expert
    ends = jnp.cumsum(padded)
    n_tiles = (TOP_K * t) // tm_expert + N_EXPERTS
    n_valid = (ends[-1] // tm_expert).astype(I32)
    tile_start = jnp.arange(n_tiles, dtype=I32) * tm_expert
    first_row = jnp.minimum(tile_start, ends[-1] - 1)
    tile_expert = jnp.minimum(jnp.sum(ends[None, :] <= first_row[:, None], axis=1), N_EXPERTS - 1).astype(I32)
    starts = ends - padded
    tile_rows = jnp.clip(counts[tile_expert] - (tile_start - starts[tile_expert]), 0, tm_expert)
    tile_rows = jnp.where(tile_start < ends[-1], tile_rows, 0).astype(I32)
    off = jnp.broadcast_to(starts.astype(F32)[:, None], (N_EXPERTS, LANES))
    pos = moe_pos(topt, rank, off, tb)
    pos_slot_major = jnp.transpose(pos, (1, 0, 2)).reshape(-1)
    xs = sc_row_scatter(xpk, pos_slot_major, n_tiles * tm_expert)
    ys = expert_ffn(xs, tile_expert, n_valid.reshape(1), tile_rows, wg, wu, wd, tm_expert, tf)
    n_parts = 4
    part = t // n_parts
    tm_c = min(tm_route, part)
    pos_by_slot = pos_slot_major.reshape(TOP_K, t)
    out = None
    for k in range(n_parts):
        yg = sc_row_gather(ys, pos_by_slot[:, k * part:(k + 1) * part].reshape(-1))
        out = moe_combine_dense_ple(h, top, yg, p, layer, ple_nw, ple_wg, ple_wp, tm_c, k * (part // tm_c), out)
    return out


def _pad_cols(w, n):
    return jnp.pad(w, ((0, 0), (0, n - w.shape[1])))


def _tile3(v, n):
    return jnp.pad(jnp.tile(v, 3), (0, LANES - 3 * n)).reshape(1, LANES)


def kernel(x, p, ssm_norm_w, ssm_w_in, ssm_conv_w, ssm_conv_b, ssm_dt_bias, ssm_a_log, ssm_d, ssm_gn_w, ssm_w_out, kv_norm_w, w_kv, b_f, k_norm_w, att_norm_w, att_w_q, q_norm_w, att_w_o, ffn_norm_w, ffn_w_gate, ffn_w_up, ffn_w_down, moe_norm_w, moe_w_router, moe_w_gate, moe_w_up, moe_w_down, ple_norm_w, ple_w_gate, ple_w_proj):
    b, l, d = x.shape
    t = b * l
    n_a = ssm_norm_w.shape[0]
    depth = ple_norm_w.shape[0]
    d_inner = ssm_gn_w.shape[1]
    n_heads = ssm_dt_bias.shape[1]
    cdim = ssm_conv_w.shape[2]
    att_w = att_w_q.shape[2]
    n_att_heads = att_w // ATT_HEADDIM
    tm = min(1024, t)
    tq = min(256, l)

    h = x.reshape(t, d)
    p3 = p.reshape(depth, t, p.shape[-1])
    r = jnp.arange(LANES)[:, None]
    c = jnp.arange(d_inner)[None, :]
    expand = ((r < 3 * n_heads) & ((r % n_heads) == (c // SSM_HEADDIM))).astype(BF16)

    kv = ext_q = ext_k = None
    for i in range(depth):
        if i < n_a:
            w_in = ssm_w_in[i].astype(BF16)
            w_dt = w_in[:, d_inner + cdim:]
            w_dt3 = _pad_cols(jnp.concatenate([w_dt, w_dt, w_dt], axis=1), LANES)
            z, xbc, dt_raw = norm_matmul(h, [ssm_norm_w[i]], [w_in[:, :d_inner], w_in[:, d_inner:d_inner + cdim], w_dt3],
                                         [0, 0, 0], [BF16, BF16, F32], tm // 2)
            y = ssd(z.reshape(b, l, d_inner), xbc.reshape(b, l, cdim), dt_raw.reshape(b, l, LANES),
                    ssm_conv_w[i], ssm_conv_b[i].reshape(1, cdim), _tile3(ssm_dt_bias[i], n_heads),
                    _tile3(ssm_a_log[i], n_heads), jnp.repeat(ssm_d[i], SSM_HEADDIM).reshape(1, d_inner),
                    ssm_gn_w[i].reshape(1, d_inner), expand, 2)
            h = matmul_residual(h, y.reshape(t, d_inner), ssm_w_out[i].astype(BF16), tm)
        else:
            j = i - n_a
            w_q = att_w_q[j].astype(BF16)
            if i == n_a:
                wkv = w_kv.astype(BF16)
                w_f = wkv[:, 2 * att_w:]
                w_f3 = _pad_cols(jnp.concatenate([w_f, w_f, w_f], axis=1), LANES)
                kv, f, qp = norm_matmul(h, [kv_norm_w, att_norm_w[j]], [wkv[:, :2 * att_w], w_f3, w_q], [0, 0, 1],
                                        [BF16, F32, BF16], tm)
                kv = kv.reshape(b, l, 2 * att_w)
                ext_q, ext_k = forget_ext(f.reshape(b, l, LANES), _tile3(b_f, n_att_heads), n_att_heads)
            else:
                qp, = norm_matmul(h, [att_norm_w[j]], [w_q], [0], [BF16], tm)
            qp = qp.reshape(b, l, att_w)
            qnw2 = jnp.tile(q_norm_w[j], 2).reshape(1, LANES)
            knw2 = jnp.tile(k_norm_w, 2).reshape(1, LANES)
            o = fox_attention(qp, kv, ext_q, ext_k, qnw2, knw2, att_w, tq)
            h = matmul_residual(h, o.reshape(t, att_w), att_w_o[j].astype(BF16), tm)
        ple_wg, ple_wp = ple_w_gate[i].astype(BF16), ple_w_proj[i].astype(BF16)
        if i % 2 == 0:
            dd = i // 2
            h = ffn_ple(h, ffn_norm_w[dd], ffn_w_gate[dd].astype(BF16), ffn_w_up[dd].astype(BF16),
                        ffn_w_down[dd].astype(BF16), p3, i, ple_norm_w[i], ple_wg, ple_wp, tm // 2, 1792)
        else:
            m = i // 2
            h = moe_ple(h, moe_norm_w[m], moe_w_router[m], moe_w_gate[m], moe_w_up[m], moe_w_down[m], p3, i,
                        ple_norm_w[i], ple_wg, ple_wp, tm, tm, min(512, t), 512)
    return h.reshape(b, l, d)
```
